```python
import jax
import jax.numpy as jnp
from jax import lax
import numpy as np

D_MODEL = 1024
BATCH = 16
SEQ = 2048
DEPTH = 1

ML_HEADS = 4
ML_DQK = 128
ML_DV = 128
ML_W = ML_HEADS * ML_DQK
ML_CHUNK = 64
ML_CONV = 4
NSA_QH = 8
NSA_KVH = 2
NSA_DH = 64
NSA_GROUP = NSA_QH // NSA_KVH
NSA_W = NSA_QH * NSA_DH
NSA_KV_W = NSA_KVH * NSA_DH
CMP_BLOCK = 32
CMP_STRIDE = 16
CMP_HIDDEN = 256
SEL_BLOCK = 64
SEL_TOPN = 8
SEL_FORCE_BONUS = 1e4
WINDOW = 512
Q_BLOCK = 128
MEM_LEN = 256
MEM_HEADS = 4
MEM_DH = 128
MEM_W = MEM_HEADS * MEM_DH
N_EXPERTS = 32
TOP_K = 4
D_FF = D_MODEL
SWIGLU_ALPHA = 1.702
SWIGLU_LIMIT = 7.0
MOE_BLOCK = 256
N_BRANCH = 3
ROPE_THETA = 10000.0
EPS = 1e-6
NEG = -1e30

IN_SPLITS = (ML_W, ML_W, ML_HEADS * ML_DV, ML_HEADS * ML_DV, ML_HEADS, ML_HEADS,
             NSA_W, NSA_KV_W, NSA_KV_W, NSA_KV_W, NSA_KV_W, NSA_KV_W, NSA_KV_W, NSA_QH * 3,
             MEM_W, N_BRANCH * D_MODEL)
D_IN = sum(IN_SPLITS)

kernel_name = "hybrid_mlstm_nsa_memxattn_moe_block"


def _rmsnorm(t, g):
    t32 = t.astype(jnp.float32)
    y = t32 * lax.rsqrt(jnp.mean(t32 * t32, axis=-1, keepdims=True) + EPS)
    return (y * g.astype(jnp.float32)).astype(t.dtype)


def _rope(t, pos):
    half = t.shape[-1] // 2
    inv_freq = jnp.power(ROPE_THETA, -jnp.arange(half, dtype=jnp.float32) / half)
    ang = pos.astype(jnp.float32)[:, None] * inv_freq[None, :]
    cos = jnp.cos(ang)[:, None, :]
    sin = jnp.sin(ang)[:, None, :]
    t32 = t.astype(jnp.float32)
    t1, t2 = t32[..., :half], t32[..., half:]
    return jnp.concatenate([t1 * cos - t2 * sin, t2 * cos + t1 * sin], axis=-1).astype(t.dtype)


def _masked_softmax(s, mask):
    s = jnp.where(mask, s.astype(jnp.float32), NEG)
    e = jnp.exp(s - jnp.max(s, axis=-1, keepdims=True)) * mask
    return e / jnp.maximum(jnp.sum(e, axis=-1, keepdims=True), 1e-30)


def _split_cols(t, sizes):
    out, off = [], 0
    for s in sizes:
        out.append(t[..., off:off + s])
        off += s
    return out


def _causal_conv(t, w, b):
    c = t.shape[-1]
    y = lax.conv_general_dilated(t, w.astype(t.dtype)[:, None, :], window_strides=(1,),
                                 padding=[(ML_CONV - 1, 0)],
                                 dimension_numbers=("NWC", "WIO", "NWC"),
                                 feature_group_count=c)
    return y + b.astype(t.dtype)


def _mlstm(q, k, v, i_pre, f_pre):
    bsz, seq, nh, _ = q.shape
    nc = seq // ML_CHUNK
    f32 = jnp.float32

    def chunks(t):
        return t.astype(f32).reshape(bsz, nc, ML_CHUNK, nh, -1).transpose(0, 3, 1, 2, 4)

    def gchunks(t):
        return t.astype(f32).reshape(bsz, nc, ML_CHUNK, nh).transpose(0, 3, 1, 2)

    q = chunks(q) * (ML_DQK ** -0.5)
    k = chunks(k)
    v = chunks(v)
    ig = gchunks(i_pre)
    b = jnp.cumsum(jax.nn.log_sigmoid(gchunks(f_pre)), axis=-1)
    b_last = b[..., -1]

    g = b_last[..., None] - b + ig
    m_c = jnp.max(g, axis=-1)
    w = jnp.exp(g - m_c[..., None])
    kv_c = jnp.einsum("bhclk,bhclv->bhckv", k * w[..., None], v)
    n_c = jnp.einsum("bhcl,bhclk->bhck", w, k)

    def step(carry, xs):
        c_st, n_st, m_st = carry
        kv, nn, mc, bl = xs
        m_new = jnp.maximum(bl + m_st, mc)
        a_old = jnp.exp(bl + m_st - m_new)
        a_new = jnp.exp(mc - m_new)
        c_next = a_old[..., None, None] * c_st + a_new[..., None, None] * kv
        n_next = a_old[..., None] * n_st + a_new[..., None] * nn
        return (c_next, n_next, m_new), (c_st, n_st, m_st)

    init = (jnp.zeros((bsz, nh, ML_DQK, ML_DV), f32), jnp.zeros((bsz, nh, ML_DQK), f32),
            jnp.zeros((bsz, nh), f32))
    xs = (jnp.moveaxis(kv_c, 2, 0), jnp.moveaxis(n_c, 2, 0), jnp.moveaxis(m_c, 2, 0),
          jnp.moveaxis(b_last, 2, 0))
    _, (c_prev, n_prev, m_prev) = lax.scan(step, init, xs)
    c_prev = jnp.moveaxis(c_prev, 0, 2)
    n_prev = jnp.moveaxis(n_prev, 0, 2)
    m_prev = jnp.moveaxis(m_prev, 0, 2)

    tri = jnp.tril(jnp.ones((ML_CHUNK, ML_CHUNK), bool))
    d_log = jnp.where(tri, b[..., :, None] - b[..., None, :] + ig[..., None, :], NEG)
    a_log = b + m_prev[..., None]
    m_t = jnp.maximum(a_log, jnp.max(d_log, axis=-1))
    wd = jnp.exp(d_log - m_t[..., None]) * jnp.einsum("bhctd,bhcsd->bhcts", q, k)
    inter = jnp.exp(a_log - m_t)
    num = inter[..., None] * jnp.einsum("bhctk,bhckv->bhctv", q, c_prev) + \
        jnp.einsum("bhcts,bhcsv->bhctv", wd, v)
    den = inter * jnp.einsum("bhctk,bhck->bhct", q, n_prev) + jnp.sum(wd, axis=-1)
    h = num / jnp.maximum(jnp.abs(den), jnp.exp(-m_t))[..., None]
    return h.transpose(0, 2, 3, 1, 4).reshape(bsz, seq, nh, ML_DV)


def _nsa(q, k_cmp, v_cmp, k_sel, v_sel, k_win, v_win, gate, g_q, g_k,
         pos_k, w1_k, b1_k, w2_k, pos_v, w1_v, b1_v, w2_v):
    bsz, seq = q.shape[:2]
    pos = jnp.arange(seq)

    def heads(t, h):
        return t.reshape(bsz, seq, h, NSA_DH)

    q = _rope(_rmsnorm(heads(q, NSA_QH), g_q), pos) * (NSA_DH ** -0.5)
    qg = q.reshape(bsz, seq, NSA_KVH, NSA_GROUP, NSA_DH)

    n_cmp = (seq - CMP_BLOCK) // CMP_STRIDE + 1
    cmp_start = jnp.arange(n_cmp) * CMP_STRIDE
    cmp_end = cmp_start + CMP_BLOCK - 1
    win_idx = cmp_start[:, None] + jnp.arange(CMP_BLOCK)[None, :]

    def compress(t, pe, w1, b1, w2):
        blocks = heads(t, NSA_KVH)[:, win_idx] + pe[None, None, :, None, :]
        hid = jax.nn.silu(jnp.einsum("bnlhd,ldf->bnhf", blocks, w1) + b1)
        return hid @ w2

    kc = _rope(_rmsnorm(compress(k_cmp, pos_k, w1_k, b1_k, w2_k), g_k[0]), cmp_end)
    vc = compress(v_cmp, pos_v, w1_v, b1_v, w2_v)
    s_cmp = jnp.einsum("bshgd,bnhd->bshgn", qg, kc)
    mask_cmp = (cmp_end[None, :] <= pos[:, None])[None, :, None, None, :]
    p_cmp = _masked_softmax(s_cmp, mask_cmp)
    o_cmp = jnp.einsum("bshgn,bnhd->bshgd", p_cmp.astype(vc.dtype), vc)

    n_sel = seq // SEL_BLOCK
    sel_start = jnp.arange(n_sel) * SEL_BLOCK
    cmp_to_sel = ((cmp_start[:, None] < sel_start[None, :] + SEL_BLOCK) &
                  (cmp_start[:, None] + CMP_BLOCK > sel_start[None, :])).astype(jnp.float32)
    importance = jnp.einsum("bshn,nj->bshj", jnp.sum(p_cmp, axis=3), cmp_to_sel)
    blk = jnp.arange(n_sel)[None, :]
    cur = (pos // SEL_BLOCK)[:, None]
    forced = (blk == 0) | (blk == cur) | (blk == cur - 1)
    score = jnp.where((blk > cur)[None, :, None, :], NEG,
                      importance + SEL_FORCE_BONUS * forced[None, :, None, :])
    _, sel_idx = lax.top_k(score, min(SEL_TOPN, n_sel))
    n_top = sel_idx.shape[-1]

    ks = _rope(_rmsnorm(heads(k_sel, NSA_KVH), g_k[1]), pos)
    ks_blk = ks.reshape(bsz, n_sel, SEL_BLOCK, NSA_KVH, NSA_DH).transpose(0, 3, 1, 2, 4)
    vs_blk = heads(v_sel, NSA_KVH).reshape(bsz, n_sel, SEL_BLOCK, NSA_KVH, NSA_DH).transpose(0, 3, 1, 2, 4)
    kw = _rope(_rmsnorm(heads(k_win, NSA_KVH), g_k[2]), pos)
    pad = ((0, 0), (WINDOW, 0), (0, 0), (0, 0))
    kw_pad = jnp.pad(kw, pad)
    vw_pad = jnp.pad(heads(v_win, NSA_KVH), pad)
    b_ix = jnp.arange(bsz)[:, None, None, None]
    h_ix = jnp.arange(NSA_KVH)[None, None, :, None]

    def query_block(args):
        qb_i, q_blk, idx_blk = args
        start = qb_i * Q_BLOCK
        tq = start + jnp.arange(Q_BLOCK)
        kg = ks_blk[b_ix, h_ix, idx_blk]
        vg = vs_blk[b_ix, h_ix, idx_blk]
        s_sel = jnp.einsum("bqhgd,bqhnld->bqhgnl", q_blk, kg)
        kpos = idx_blk[..., None] * SEL_BLOCK + jnp.arange(SEL_BLOCK)
        m_sel = (kpos <= tq[None, :, None, None, None]).reshape(bsz, Q_BLOCK, NSA_KVH, 1, n_top * SEL_BLOCK)
        p_sel = _masked_softmax(s_sel.reshape(bsz, Q_BLOCK, NSA_KVH, NSA_GROUP, -1), m_sel)
        o_sel = jnp.einsum("bqhgk,bqhkd->bqhgd", p_sel.astype(vg.dtype),
                           vg.reshape(bsz, Q_BLOCK, NSA_KVH, n_top * SEL_BLOCK, NSA_DH))
        kwb = lax.dynamic_slice_in_dim(kw_pad, start, Q_BLOCK + WINDOW, axis=1)
        vwb = lax.dynamic_slice_in_dim(vw_pad, start, Q_BLOCK + WINDOW, axis=1)
        kpos_w = start - WINDOW + jnp.arange(Q_BLOCK + WINDOW)
        diff = tq[:, None] - kpos_w[None, :]
        m_win = (diff >= 0) & (diff < WINDOW) & (kpos_w[None, :] >= 0)
        s_win = jnp.einsum("bqhgd,bkhd->bqhgk", q_blk, kwb)
        p_win = _masked_softmax(s_win, m_win[None, :, None, None, :])
        o_win = jnp.einsum("bqhgk,bkhd->bqhgd", p_win.astype(vwb.dtype), vwb)
        return o_sel, o_win

    n_qb = seq // Q_BLOCK
    q_blocks = qg.reshape(bsz, n_qb, Q_BLOCK, NSA_KVH, NSA_GROUP, NSA_DH).swapaxes(0, 1)
    idx_blocks = sel_idx.reshape(bsz, n_qb, Q_BLOCK, NSA_KVH, n_top).swapaxes(0, 1)
    o_sel, o_win = lax.map(query_block, (jnp.arange(n_qb), q_blocks, idx_blocks))
    o_sel = o_sel.swapaxes(0, 1).reshape(bsz, seq, NSA_KVH, NSA_GROUP, NSA_DH)
    o_win = o_win.swapaxes(0, 1).reshape(bsz, seq, NSA_KVH, NSA_GROUP, NSA_DH)

    g = jax.nn.sigmoid(gate.reshape(bsz, seq, NSA_KVH, NSA_GROUP, 3))
    o = g[..., 0:1] * o_cmp + g[..., 1:2] * o_sel + g[..., 2:3] * o_win
    return o.reshape(bsz, seq, NSA_W)


def _mem_attn(q, mem, g_mem, w_mem_kv, g_q, g_k):
    bsz, seq = q.shape[:2]
    m_len = mem.shape[1]
    kv = _rmsnorm(mem, g_mem) @ w_mem_kv
    k = _rmsnorm(kv[..., :MEM_W].reshape(bsz, m_len, MEM_HEADS, MEM_DH), g_k)
    v = kv[..., MEM_W:].reshape(bsz, m_len, MEM_HEADS, MEM_DH)
    q = _rmsnorm(q.reshape(bsz, seq, MEM_HEADS, MEM_DH), g_q) * (MEM_DH ** -0.5)
    p = jax.nn.softmax(jnp.einsum("bshd,bmhd->bhsm", q, k).astype(jnp.float32), axis=-1)
    return jnp.einsum("bhsm,bmhd->bshd", p.astype(v.dtype), v).reshape(bsz, seq, MEM_W)


def _moe(t, w_router, b_router, w_gu, b_gu, w_down, b_down):
    bsz, seq, d = t.shape
    tok = t.reshape(-1, d)
    n_tok = tok.shape[0]
    n_rows = n_tok * TOP_K
    logits = (tok @ w_router + b_router).astype(jnp.float32)
    top_val, top_idx = lax.top_k(logits, TOP_K)
    top_w = jax.nn.softmax(top_val, axis=-1)
    flat_e = top_idx.reshape(-1)
    flat_w = top_w.reshape(-1)
    order = jnp.argsort(flat_e)
    sorted_e = flat_e[order]
    counts = jnp.bincount(flat_e, length=N_EXPERTS)
    padded = (counts + MOE_BLOCK - 1) // MOE_BLOCK * MOE_BLOCK
    start = jnp.cumsum(counts) - counts
    pad_end = jnp.cumsum(padded)
    pad_start = pad_end - padded
    dest = pad_start[sorted_e] + jnp.arange(n_rows) - start[sorted_e]
    n_blocks = -(-(n_rows + N_EXPERTS * (MOE_BLOCK - 1)) // MOE_BLOCK)
    cap = n_blocks * MOE_BLOCK
    row_tok = jnp.zeros((cap,), jnp.int32).at[dest].set((order // TOP_K).astype(jnp.int32))
    row_w = jnp.zeros((cap,), jnp.float32).at[dest].set(flat_w[order])
    blk_e = jnp.minimum(jnp.searchsorted(pad_end, jnp.arange(n_blocks) * MOE_BLOCK, side="right"),
                        N_EXPERTS - 1)

    def expert_block(args):
        rows, wts, e = args
        hgu = tok[rows] @ w_gu[e] + b_gu[e]
        gate = jnp.minimum(hgu[:, 0::2], SWIGLU_LIMIT)
        up = jnp.clip(hgu[:, 1::2], -SWIGLU_LIMIT, SWIGLU_LIMIT)
        act = (up + 1.0) * (gate * jax.nn.sigmoid(SWIGLU_ALPHA * gate))
        y = act @ w_down[e] + b_down[e]
        return y.astype(jnp.float32) * wts[:, None]

    ys = lax.map(expert_block, (row_tok.reshape(n_blocks, MOE_BLOCK),
                                row_w.reshape(n_blocks, MOE_BLOCK), blk_e))
    out = jnp.zeros((n_tok, d), jnp.float32).at[row_tok].add(ys.reshape(cap, d))
    return out.astype(t.dtype).reshape(bsz, seq, d)


def _hybrid_layer(x, mem, g_mix, w_in, ml_conv_w, ml_conv_b, ml_b_i, ml_b_f, ml_g_out,
                  nsa_g_q, nsa_g_k, cmp_pos_k, cmp_w1_k, cmp_b1_k, cmp_w2_k,
                  cmp_pos_v, cmp_w1_v, cmp_b1_v, cmp_w2_v, g_mem, w_mem_kv, mem_g_q, mem_g_k,
                  w_br_ml, w_br_nsa, w_br_mem, w_out, g_ffn, w_router, b_router,
                  w_gu, b_gu, w_down, b_down):
    bsz, seq, d = x.shape
    u = _rmsnorm(x, g_mix)
    (ml_q, ml_k, ml_v, ml_o, ml_i, ml_f, n_q, n_kc, n_vc, n_ks, n_vs, n_kw, n_vw, n_gate,
     m_q, br_gate) = _split_cols(u @ w_in, IN_SPLITS)

    qk = jax.nn.silu(_causal_conv(jnp.concatenate([ml_q, ml_k], axis=-1), ml_conv_w, ml_conv_b))

    def mh(t):
        return t.reshape(bsz, seq, ML_HEADS, -1)

    h_ml = _mlstm(mh(qk[..., :ML_W]), mh(qk[..., ML_W:]), mh(ml_v), ml_i + ml_b_i, ml_f + ml_b_f)
    h_ml = (_rmsnorm(h_ml, ml_g_out).astype(x.dtype) * jax.nn.sigmoid(mh(ml_o))).reshape(bsz, seq, -1)

    h_nsa = _nsa(n_q, n_kc, n_vc, n_ks, n_vs, n_kw, n_vw, n_gate, nsa_g_q, nsa_g_k,
                 cmp_pos_k, cmp_w1_k, cmp_b1_k, cmp_w2_k, cmp_pos_v, cmp_w1_v, cmp_b1_v, cmp_w2_v)

    h_mem = _mem_attn(m_q, mem, g_mem, w_mem_kv, mem_g_q, mem_g_k)

    gates = jax.nn.sigmoid(br_gate.reshape(bsz, seq, N_BRANCH, d))
    merged = (gates[:, :, 0] * (h_ml @ w_br_ml) + gates[:, :, 1] * (h_nsa @ w_br_nsa)
              + gates[:, :, 2] * (h_mem @ w_br_mem))
    x = x + merged @ w_out
    return x + _moe(_rmsnorm(x, g_ffn), w_router, b_router, w_gu, b_gu, w_down, b_down)


def setup_inputs(seed: int = 0) -> dict:
    key = jax.random.key(seed)
    keys = iter(jax.random.split(key, 40))
    L, D = DEPTH, D_MODEL

    def normal(shape, scale):
        return jax.random.normal(next(keys), shape, jnp.float32) * scale

    def gain(shape):
        return 1.0 + normal(shape, 0.02)

    return {
        "x": normal((BATCH, SEQ, D), 1.0),
        "mem": normal((BATCH, MEM_LEN, D), 1.0),
        "g_mix": gain((L, D)),
        "w_in": normal((L, D, D_IN), D ** -0.5),
        "ml_conv_w": normal((L, ML_CONV, 2 * ML_W), ML_CONV ** -0.5),
        "ml_conv_b": normal((L, 2 * ML_W), 0.02),
        "ml_b_i": normal((L, ML_HEADS), 0.1),
        "ml_b_f": jnp.linspace(3.0, 6.0, ML_HEADS, dtype=jnp.float32)[None, :] + normal((L, ML_HEADS), 0.1),
        "ml_g_out": gain((L, ML_HEADS, ML_DV)),
        "nsa_g_q": gain((L, NSA_DH)),
        "nsa_g_k": gain((L, 3, NSA_DH)),
        "cmp_pos_k": normal((L, CMP_BLOCK, NSA_DH), 0.1),
        "cmp_w1_k": normal((L, CMP_BLOCK, NSA_DH, CMP_HIDDEN), (CMP_BLOCK * NSA_DH) ** -0.5),
        "cmp_b1_k": normal((L, CMP_HIDDEN), 0.02),
        "cmp_w2_k": normal((L, CMP_HIDDEN, NSA_DH), CMP_HIDDEN ** -0.5),
        "cmp_pos_v": normal((L, CMP_BLOCK, NSA_DH), 0.1),
        "cmp_w1_v": normal((L, CMP_BLOCK, NSA_DH, CMP_HIDDEN), (CMP_BLOCK * NSA_DH) ** -0.5),
        "cmp_b1_v": normal((L, CMP_HIDDEN), 0.02),
        "cmp_w2_v": normal((L, CMP_HIDDEN, NSA_DH), CMP_HIDDEN ** -0.5),
        "g_mem": gain((L, D)),
        "w_mem_kv": normal((L, D, 2 * MEM_W), D ** -0.5),
        "mem_g_q": gain((L, MEM_DH)),
        "mem_g_k": gain((L, MEM_DH)),
        "w_br_ml": normal((L, ML_HEADS * ML_DV, D), (ML_HEADS * ML_DV) ** -0.5),
        "w_br_nsa": normal((L, NSA_W, D), NSA_W ** -0.5),
        "w_br_mem": normal((L, MEM_W, D), MEM_W ** -0.5),
        "w_out": normal((L, D, D), D ** -0.5),
        "g_ffn": gain((L, D)),
        "w_router": normal((L, D, N_EXPERTS), D ** -0.5),
        "b_router": normal((L, N_EXPERTS), 0.01),
        "w_gu": normal((L, N_EXPERTS, D, 2 * D_FF), D ** -0.5),
        "b_gu": normal((L, N_EXPERTS, 2 * D_FF), 0.01),
        "w_down": normal((L, N_EXPERTS, D_FF, D), D_FF ** -0.5),
        "b_down": normal((L, N_EXPERTS, D), 0.01),
    }


def reference(x, mem, g_mix, w_in, ml_conv_w, ml_conv_b, ml_b_i, ml_b_f, ml_g_out,
              nsa_g_q, nsa_g_k, cmp_pos_k, cmp_w1_k, cmp_b1_k, cmp_w2_k,
              cmp_pos_v, cmp_w1_v, cmp_b1_v, cmp_w2_v, g_mem, w_mem_kv, mem_g_q, mem_g_k,
              w_br_ml, w_br_nsa, w_br_mem, w_out, g_ffn, w_router, b_router,
              w_gu, b_gu, w_down, b_down):
    h = x
    for l in range(DEPTH):
        h = _hybrid_layer(h, mem, g_mix[l], w_in[l], ml_conv_w[l], ml_conv_b[l], ml_b_i[l], ml_b_f[l],
                          ml_g_out[l], nsa_g_q[l], nsa_g_k[l], cmp_pos_k[l], cmp_w1_k[l], cmp_b1_k[l],
                          cmp_w2_k[l], cmp_pos_v[l], cmp_w1_v[l], cmp_b1_v[l], cmp_w2_v[l], g_mem[l],
                          w_mem_kv[l], mem_g_q[l], mem_g_k[l], w_br_ml[l], w_br_nsa[l], w_br_mem[l],
                          w_out[l], g_ffn[l], w_router[l], b_router[l], w_gu[l], b_gu[l], w_down[l],
                          b_down[l])
    return h
```

```python
import functools

import jax
import jax.numpy as jnp
import numpy as np
from jax import lax
from jax.experimental import pallas as pl
from jax.experimental.pallas import tpu as pltpu

F32 = jnp.float32
BF16 = jnp.bfloat16

D_MODEL = 1024
ML_HEADS = 4
ML_DQK = 128
ML_W = ML_HEADS * ML_DQK
ML_CHUNK = 64
ML_CONV = 4
NSA_QH = 8
NSA_KVH = 2
NSA_DH = 64
NSA_GROUP = NSA_QH // NSA_KVH
NSA_W = NSA_QH * NSA_DH
NSA_KV_W = NSA_KVH * NSA_DH
CMP_BLOCK = 32
CMP_STRIDE = 16
CMP_HIDDEN = 256
SEL_BLOCK = 64
SEL_TOPN = 8
SEL_FORCE_BONUS = 1e4
WINDOW = 512
MEM_HEADS = 4
MEM_DH = 128
MEM_W = MEM_HEADS * MEM_DH
N_EXPERTS = 32
TOP_K = 4
D_FF = D_MODEL
SWIGLU_ALPHA = 1.702
SWIGLU_LIMIT = 7.0
N_BRANCH = 3
ROPE_THETA = 10000.0
EPS = 1e-6
NEG = -1e30

LANES = 128
VMEM_LIMIT = 56 * 1024 * 1024

IN_SPLITS = (ML_W, ML_W, ML_W, ML_W, ML_HEADS, ML_HEADS,
             NSA_W, NSA_KV_W, NSA_KV_W, NSA_KV_W, NSA_KV_W, NSA_KV_W, NSA_KV_W, NSA_QH * 3,
             MEM_W, N_BRANCH * D_MODEL)

ML_COLS = 4 * ML_W + LANES
NSA_COLS = NSA_W + 6 * NSA_KV_W + LANES


def _dot(a, b):
    return jnp.dot(a, b, preferred_element_type=F32)


def _dot_nt(a, b):
    return lax.dot_general(a, b, (((1,), (1,)), ((), ())), preferred_element_type=F32)


def _dot_tn(a, b):
    return lax.dot_general(a, b, (((0,), (0,)), ((), ())), preferred_element_type=F32)


def _split3(v):
    a = v.astype(BF16)
    r = v - a.astype(F32)
    b = r.astype(BF16)
    c = (r - b.astype(F32)).astype(BF16)
    return a, b, c


def _sigmoid(v):
    return 1.0 / (1.0 + jnp.exp(-v))


def _log_sigmoid(v):
    return jnp.minimum(v, 0.0) - jnp.log1p(jnp.exp(-jnp.abs(v)))


def _params(sem):
    return pltpu.CompilerParams(dimension_semantics=sem, vmem_limit_bytes=VMEM_LIMIT)


PROJ_TM = 512


def _proj_kernel(x_ref, g_ref, wml_ref, wnsa_ref, oml_ref, onsa_ref):
    x = x_ref[...]
    u = x * lax.rsqrt(jnp.mean(x * x, axis=-1, keepdims=True) + EPS) * g_ref[...]
    ub = u.astype(BF16)
    oml_ref[...] = _dot(ub, wml_ref[...])
    onsa_ref[...] = _dot(ub, wnsa_ref[...])


def _proj(x2, g_mix, w_ml, w_nsa):
    n = x2.shape[0]
    tm = min(PROJ_TM, n)
    return pl.pallas_call(
        _proj_kernel,
        out_shape=(jax.ShapeDtypeStruct((n, ML_COLS), F32), jax.ShapeDtypeStruct((n, NSA_COLS), F32)),
        grid=(n // tm,),
        in_specs=[pl.BlockSpec((tm, D_MODEL), lambda i: (i, 0)),
                  pl.BlockSpec((1, D_MODEL), lambda i: (0, 0)),
                  pl.BlockSpec((D_MODEL, ML_COLS), lambda i: (0, 0)),
                  pl.BlockSpec((D_MODEL, NSA_COLS), lambda i: (0, 0))],
        out_specs=(pl.BlockSpec((tm, ML_COLS), lambda i: (i, 0)),
                   pl.BlockSpec((tm, NSA_COLS), lambda i: (i, 0))),
        compiler_params=_params(("parallel",)),
        name="in_proj",
    )(x2, g_mix, w_ml, w_nsa)


ML_T = 256
CONV_PAD = 8


def _mlstm_kernel(q_ref, k_ref, v_ref, o_ref, g_ref, cw_ref, cb_ref, gb_ref, gout_ref, tri_ref,
                  out_ref, buf_ref, c_ref, n_ref, m_ref):
    T, L = ML_T, ML_CHUNK

    @pl.when(pl.program_id(1) == 0)
    def _():
        buf_ref[0:CONV_PAD, :] = jnp.zeros((CONV_PAD, 2 * ML_W), F32)
        c_ref[...] = jnp.zeros(c_ref.shape, F32)
        n_ref[...] = jnp.zeros(n_ref.shape, F32)
        m_ref[...] = jnp.zeros(m_ref.shape, F32)

    buf_ref[CONV_PAD:CONV_PAD + T, 0:ML_W] = q_ref[...]
    buf_ref[CONV_PAD:CONV_PAD + T, ML_W:2 * ML_W] = k_ref[...]
    y = cb_ref[...]
    for j in range(ML_CONV):
        lo = CONV_PAD - (ML_CONV - 1) + j
        y = y + cw_ref[j:j + 1, :] * buf_ref[lo:lo + T, :]
    buf_ref[0:CONV_PAD, :] = buf_ref[T:T + CONV_PAD, :]
    qk = y * _sigmoid(y)
    qf = qk[:, 0:ML_W] * (ML_DQK ** -0.5)
    kf = qk[:, ML_W:2 * ML_W]

    gg = g_ref[...] + gb_ref[...]
    gt = gg.T
    tri = tri_ref[...]
    lf1, lf2, lf3 = _split3(_log_sigmoid(gg))
    b_col = _dot(tri, lf1) + _dot(tri, lf2) + _dot(tri, lf3)
    lt1, lt2, lt3 = _split3(_log_sigmoid(gt[0:8, :]))
    b_row = _dot_nt(lt1, tri) + _dot_nt(lt2, tri) + _dot_nt(lt3, tri)

    r_i = lax.broadcasted_iota(jnp.int32, (L, L), 0)
    c_i = lax.broadcasted_iota(jnp.int32, (L, L), 1)
    causal = c_i <= r_i

    for c in range(T // L):
        rows = slice(c * L, (c + 1) * L)
        for h in range(ML_HEADS):
            cols = slice(h * ML_DQK, (h + 1) * ML_DQK)
            bc = b_col[rows, ML_HEADS + h:ML_HEADS + h + 1]
            igc = gg[rows, h:h + 1]
            br = b_row[ML_HEADS + h:ML_HEADS + h + 1, rows]
            igr = gt[h:h + 1, rows]
            q_c = qf[rows, cols]
            k_c = kf[rows, cols]
            v_c = v_ref[rows, cols]
            q_b = q_c.astype(BF16)
            c_prev = c_ref[h]
            n_prev = n_ref[h]
            m_prev = m_ref[h][:, 0:1]

            d_log = jnp.where(causal, bc - br + igr, NEG)
            a_log = bc + m_prev
            m_t = jnp.maximum(a_log, jnp.max(d_log, axis=1, keepdims=True))
            wd = jnp.exp(d_log - m_t) * _dot_nt(q_b, k_c.astype(BF16))
            inter = jnp.exp(a_log - m_t)
            num = inter * _dot(q_b, c_prev.astype(BF16)) + _dot(wd.astype(BF16), v_c.astype(BF16))
            den = inter * jnp.sum(q_c * n_prev, axis=1, keepdims=True) + jnp.sum(wd, axis=1, keepdims=True)
            hh = num / jnp.maximum(jnp.abs(den), jnp.exp(-m_t))

            bl = bc[L - 1:L, :]
            g_col = bl - bc + igc
            m_c = jnp.max(bl - br + igr, axis=1, keepdims=True)
            w_col = jnp.exp(g_col - m_c)
            kw = k_c * w_col
            kv_c = _dot_tn(kw.astype(BF16), v_c.astype(BF16))
            n_c = jnp.sum(kw, axis=0, keepdims=True)
            m_new = jnp.maximum(bl + m_prev, m_c)
            a_old = jnp.exp(bl + m_prev - m_new)
            a_new = jnp.exp(m_c - m_new)
            c_ref[h] = a_old * c_prev + a_new * kv_c
            n_ref[h] = a_old * n_prev + a_new * n_c
            m_ref[h] = jnp.broadcast_to(m_new, (1, LANES))

            hn = hh * lax.rsqrt(jnp.mean(hh * hh, axis=-1, keepdims=True) + EPS) * gout_ref[h:h + 1, :]
            out_ref[rows, cols] = (hn * _sigmoid(o_ref[rows, cols])).astype(BF16)


def _mlstm(ml, conv_w, conv_b, gate_b, g_out, bsz, seq):
    T = min(ML_T, seq)
    assert T == ML_T and seq % T == 0
    t_i = np.arange(T)
    tri = ((t_i[:, None] // ML_CHUNK == t_i[None, :] // ML_CHUNK) & (t_i[None, :] <= t_i[:, None]))
    tri = jnp.asarray(tri, BF16)
    wblk = lambda j: pl.BlockSpec((None, T, ML_W), lambda b, i, j=j: (b, i, j))
    full = lambda shape: pl.BlockSpec(shape, lambda b, i: (0,) * len(shape))
    return pl.pallas_call(
        _mlstm_kernel,
        out_shape=jax.ShapeDtypeStruct((bsz, seq, ML_W), BF16),
        grid=(bsz, seq // T),
        in_specs=[wblk(0), wblk(1), wblk(2), wblk(3),
                  pl.BlockSpec((None, T, LANES), lambda b, i: (b, i, 4 * ML_W // LANES)),
                  full((ML_CONV, 2 * ML_W)), full((1, 2 * ML_W)), full((1, LANES)),
                  full((ML_HEADS, ML_DQK)), full((T, T))],
        out_specs=pl.BlockSpec((None, T, ML_W), lambda b, i: (b, i, 0)),
        scratch_shapes=[pltpu.VMEM((CONV_PAD + T, 2 * ML_W), F32),
                        pltpu.VMEM((ML_HEADS, ML_DQK, ML_DQK), F32),
                        pltpu.VMEM((ML_HEADS, 1, ML_DQK), F32),
                        pltpu.VMEM((ML_HEADS, 1, LANES), F32)],
        compiler_params=_params(("parallel", "arbitrary")),
        name="mlstm",
    )(ml, ml, ml, ml, ml, conv_w, conv_b, gate_b, g_out, tri)


NSA_TP = 256
NSA_TQ = 256
NSA_OFF_KC, NSA_OFF_VC, NSA_OFF_KS, NSA_OFF_VS, NSA_OFF_KW, NSA_OFF_VW, NSA_OFF_GATE = (
    NSA_W + k * NSA_KV_W for k in range(7))


def _rope_tables(pos):
    half = NSA_DH // 2
    inv_freq = jnp.power(ROPE_THETA, -jnp.arange(half, dtype=F32) / half)
    ang = pos.astype(F32)[:, None] * inv_freq[None, :]
    cos, sin = jnp.cos(ang), jnp.sin(ang)
    cos = jnp.concatenate([cos, cos], axis=1)
    sin = jnp.concatenate([-sin, sin], axis=1)
    return jnp.tile(cos, (1, 2)), jnp.tile(sin, (1, 2))


def _head_norm_rope(xs, gsum, gain, cos, sin):
    a, b, c = _split3(xs * xs)
    ss = _dot(a, gsum) + _dot(b, gsum) + _dot(c, gsum)
    xn = xs * lax.rsqrt(ss * (1.0 / NSA_DH) + EPS) * gain
    lane = lax.broadcasted_iota(jnp.int32, xs.shape, 1)
    first = (lane % NSA_DH) < NSA_DH // 2
    swapped = jnp.where(first, pltpu.roll(xn, LANES - NSA_DH // 2, axis=1), pltpu.roll(xn, NSA_DH // 2, axis=1))
    return xn * cos + swapped * sin


def _dup_heads(slab):
    lane = lax.broadcasted_iota(jnp.int32, slab.shape, 1)
    low = lane < NSA_DH
    rolled = pltpu.roll(slab, NSA_DH, axis=1)
    return jnp.where(low, slab, rolled), jnp.where(low, rolled, slab)


def _nsa_prep_kernel(x_ref, cos_ref, sin_ref, gq_ref, gks_ref, gkw_ref, gsum_ref,
                     q_out, ks_out, vs_out, kw_out, vw_out, kc_out, vc_out):
    cos, sin, gsum = cos_ref[...], sin_ref[...], gsum_ref[...]
    for c in range(NSA_W // LANES):
        xs = x_ref[:, c * LANES:(c + 1) * LANES]
        qn = _head_norm_rope(xs, gsum, gq_ref[...], cos, sin) * (NSA_DH ** -0.5)
        q_out[:, c * LANES:(c + 1) * LANES] = qn.astype(BF16)
    ks = _head_norm_rope(x_ref[:, NSA_OFF_KS:NSA_OFF_KS + LANES], gsum, gks_ref[...], cos, sin)
    kw = _head_norm_rope(x_ref[:, NSA_OFF_KW:NSA_OFF_KW + LANES], gsum, gkw_ref[...], cos, sin)
    for out, val in ((ks_out, ks), (vs_out, x_ref[:, NSA_OFF_VS:NSA_OFF_VS + LANES]),
                     (kw_out, kw), (vw_out, x_ref[:, NSA_OFF_VW:NSA_OFF_VW + LANES])):
        d0, d1 = _dup_heads(val)
        out[0] = d0.astype(BF16)
        out[1] = d1.astype(BF16)
    kc_out[...] = x_ref[:, NSA_OFF_KC:NSA_OFF_KC + LANES]
    vc_out[...] = x_ref[:, NSA_OFF_VC:NSA_OFF_VC + LANES]


def _nsa_cmp_kernel(xk_ref, xv_ref, pek_ref, pev_ref, w1ka_ref, w1kb_ref, w1va_ref, w1vb_ref,
                    b1k_ref, b1v_ref, w2k_ref, w2v_ref, gk_ref, cos_ref, sin_ref, gsum_ref,
                    kc_out, vc_out):
    rows = xk_ref.shape[0]

    def compress(x, pe, wa, wb, b1, w2):
        first = _dot((x + pe[0:1, :]).astype(BF16), wa[...])
        second = _dot((x + pe[1:2, :]).astype(BF16), wb[...])
        hid = first + pltpu.roll(second, rows - 1, axis=0) + b1[...]
        hid = hid * _sigmoid(hid)
        return _dot(hid.astype(BF16), w2[...])

    kc = compress(xk_ref[...], pek_ref, w1ka_ref, w1kb_ref, b1k_ref, w2k_ref)
    vc = compress(xv_ref[...], pev_ref, w1va_ref, w1vb_ref, b1v_ref, w2v_ref)
    kc = _head_norm_rope(kc, gsum_ref[...], gk_ref[...], cos_ref[...], sin_ref[...])
    for out, val in ((kc_out, kc), (vc_out, vc)):
        d0, d1 = _dup_heads(val)
        out[0] = d0.astype(BF16)
        out[1] = d1.astype(BF16)


def _online_update(s, maskf, v, m_ref, l_ref, acc_ref, g):
    m_old = m_ref[g]
    m_new = jnp.maximum(m_old, jnp.max(s, axis=1, keepdims=True))
    alpha = jnp.exp(m_old - m_new)
    e = jnp.exp(s - m_new) * maskf
    l_ref[g] = alpha * l_ref[g] + jnp.sum(e, axis=1, keepdims=True)
    acc_ref[g] = alpha * acc_ref[g] + _dot(e.astype(BF16), v)
    m_ref[g] = m_new


def _nsa_attn_kernel(q_ref, kc_ref, vc_ref, ks_ref, vs_ref, kw_ref, vw_ref, gate_ref, mt_ref, esel_ref, eg_ref,
                     out_ref, qh_s, ocmp_s, ms_s, ls_s, accs_s, mw_s, lw_s, accw_s, *, n_cmp, n_sel):
    TQ = NSA_TQ
    i = pl.program_id(2)
    t0 = i * TQ
    lane = lax.broadcasted_iota(jnp.int32, (TQ, LANES), 1)
    low = lane < NSA_DH
    t_col = t0 + lax.broadcasted_iota(jnp.int32, (TQ, 1), 0)

    for g in range(NSA_GROUP):
        slab = q_ref[:, (g // 2) * LANES:(g // 2 + 1) * LANES]
        keep = low if g % 2 == 0 else jnp.logical_not(low)
        qh_s[g] = jnp.where(keep, slab, jnp.zeros_like(slab))

    rows = kc_ref.shape[0]
    n_i = lax.broadcasted_iota(jnp.int32, (TQ, rows), 1)
    cmask = (n_i < n_cmp) & (n_i * CMP_STRIDE + (CMP_BLOCK - 1) <= t_col)
    cmf = cmask.astype(F32)
    psum = jnp.zeros((TQ, rows), F32)
    for g in range(NSA_GROUP):
        s = jnp.where(cmask, _dot_nt(qh_s[g], kc_ref[...]), NEG)
        e = jnp.exp(s - jnp.max(s, axis=1, keepdims=True)) * cmf
        p = e / jnp.maximum(jnp.sum(e, axis=1, keepdims=True), 1e-30)
        psum = psum + p
        ocmp_s[g] = _dot(p.astype(BF16), vc_ref[...])
    p1, p2, p3 = _split3(psum)
    imp = _dot_nt(mt_ref[...], p1) + _dot_nt(mt_ref[...], p2) + _dot_nt(mt_ref[...], p3)
    j_i = lax.broadcasted_iota(jnp.int32, (n_sel, TQ), 0)
    cur = (t0 + lax.broadcasted_iota(jnp.int32, (n_sel, TQ), 1)) // SEL_BLOCK
    forced = (j_i == 0) | (j_i == cur) | (j_i == cur - 1)
    score = jnp.where(j_i > cur, NEG, imp[0:n_sel, :] + SEL_FORCE_BONUS * forced.astype(F32))
    rank = jnp.zeros((n_sel, TQ), jnp.int32)
    for jp in range(n_sel):
        row = score[jp:jp + 1, :]
        beats = (row > score) | ((row == score) & (j_i > jp))
        rank = rank + beats.astype(jnp.int32)
    sel_t = (rank < min(SEL_TOPN, n_sel)).astype(F32)
    sel_t = jnp.concatenate([sel_t, jnp.zeros((LANES - n_sel, TQ), F32)], axis=0)
    sel_b = sel_t.T.astype(BF16)

    for m_ref, l_ref, acc_ref in ((ms_s, ls_s, accs_s), (mw_s, lw_s, accw_s)):
        m_ref[...] = jnp.full(m_ref.shape, NEG, F32)
        l_ref[...] = jnp.zeros(l_ref.shape, F32)
        acc_ref[...] = jnp.zeros(acc_ref.shape, F32)

    def key_tile(jt, carry):
        k0 = pl.multiple_of(jt * TQ, TQ)
        diff = t_col - (k0 + lax.broadcasted_iota(jnp.int32, (1, TQ), 1))
        causal = diff >= 0
        smask = (_dot(sel_b, esel_ref[jt]) > 0.5) & causal
        smf = smask.astype(F32)
        ks = ks_ref[pl.ds(k0, TQ), :]
        vs = vs_ref[pl.ds(k0, TQ), :]
        for g in range(NSA_GROUP):
            s = jnp.where(smask, _dot_nt(qh_s[g], ks), NEG)
            _online_update(s, smf, vs, ms_s, ls_s, accs_s, g)

        @pl.when(jt * TQ + TQ > t0 - WINDOW + 1)
        def _():
            wmask = causal & (diff < WINDOW)
            wmf = wmask.astype(F32)
            kw = kw_ref[pl.ds(k0, TQ), :]
            vw = vw_ref[pl.ds(k0, TQ), :]
            for g in range(NSA_GROUP):
                s = jnp.where(wmask, _dot_nt(qh_s[g], kw), NEG)
                _online_update(s, wmf, vw, mw_s, lw_s, accw_s, g)
        return carry

    lax.fori_loop(0, i + 1, key_tile, 0)

    g1, g2, g3 = _split3(_sigmoid(gate_ref[...]))
    gexp = _dot(g1, eg_ref[...]) + _dot(g2, eg_ref[...]) + _dot(g3, eg_ref[...])
    half_w = NSA_GROUP * NSA_DH
    for c in range(NSA_GROUP // 2):
        ge, go = 2 * c, 2 * c + 1
        o_cmp = jnp.where(low, ocmp_s[ge], ocmp_s[go])
        o_sel = jnp.where(low, accs_s[ge] / jnp.maximum(ls_s[ge], 1e-30), accs_s[go] / jnp.maximum(ls_s[go], 1e-30))
        o_win = jnp.where(low, accw_s[ge] / jnp.maximum(lw_s[ge], 1e-30), accw_s[go] / jnp.maximum(lw_s[go], 1e-30))
        cs = slice(c * LANES, (c + 1) * LANES)
        o = (gexp[:, c * LANES:(c + 1) * LANES] * o_cmp
             + gexp[:, half_w + c * LANES:half_w + (c + 1) * LANES] * o_sel
             + gexp[:, 2 * half_w + c * LANES:2 * half_w + (c + 1) * LANES] * o_win)
        out_ref[:, cs] = o.astype(BF16)


def _nsa_constants(seq):
    n_cmp = (seq - CMP_BLOCK) // CMP_STRIDE + 1
    n_sel = seq // SEL_BLOCK
    rows = seq // CMP_STRIDE
    n = np.arange(rows)[None, :]
    j = np.arange(LANES)[:, None]
    mt = ((n * CMP_STRIDE < j * SEL_BLOCK + SEL_BLOCK) & (n * CMP_STRIDE + CMP_BLOCK > j * SEL_BLOCK)
          & (n < n_cmp) & (j < n_sel))
    s = np.arange(seq)
    esel = (s[None, :] // SEL_BLOCK == np.arange(LANES)[:, None])
    esel = esel.reshape(LANES, seq // NSA_TQ, NSA_TQ).transpose(1, 0, 2)
    eg = np.zeros((NSA_KVH, LANES, N_BRANCH * NSA_GROUP * NSA_DH), np.float32)
    for h in range(NSA_KVH):
        for g in range(NSA_GROUP):
            for br in range(N_BRANCH):
                lane = (h * NSA_GROUP + g) * N_BRANCH + br
                c0 = br * NSA_GROUP * NSA_DH + g * NSA_DH
                eg[h, lane, c0:c0 + NSA_DH] = 1.0
    gsum = np.kron(np.eye(LANES // NSA_DH), np.ones((NSA_DH, NSA_DH)))
    return (n_cmp, n_sel, jnp.asarray(mt, BF16), jnp.asarray(esel, BF16), jnp.asarray(eg, BF16),
            jnp.asarray(gsum, BF16))


def _nsa(nsa, w, bsz, seq):
    assert seq % NSA_TQ == 0 and seq % NSA_TP == 0
    n_cmp, n_sel, mt, esel, eg, gsum = _nsa_constants(seq)
    rows = seq // CMP_STRIDE
    cos, sin = _rope_tables(jnp.arange(seq))
    cos_c, sin_c = _rope_tables(jnp.arange(rows) * CMP_STRIDE + (CMP_BLOCK - 1))
    c2 = lambda shape: pl.BlockSpec(shape, lambda b, i: (0,) * len(shape))
    kv_shape = jax.ShapeDtypeStruct((bsz, NSA_KVH, seq, LANES), BF16)
    kv_spec = pl.BlockSpec((None, NSA_KVH, NSA_TP, LANES), lambda b, i: (b, 0, i, 0))
    raw_shape = jax.ShapeDtypeStruct((bsz, seq, LANES), F32)
    raw_spec = pl.BlockSpec((None, NSA_TP, LANES), lambda b, i: (b, i, 0))
    qn, ksd, vsd, kwd, vwd, kc_raw, vc_raw = pl.pallas_call(
        _nsa_prep_kernel,
        out_shape=(jax.ShapeDtypeStruct((bsz, seq, NSA_W), BF16), kv_shape, kv_shape, kv_shape, kv_shape,
                   raw_shape, raw_shape),
        grid=(bsz, seq // NSA_TP),
        in_specs=[pl.BlockSpec((None, NSA_TP, NSA_COLS), lambda b, i: (b, i, 0)),
                  pl.BlockSpec((NSA_TP, LANES), lambda b, i: (i, 0)),
                  pl.BlockSpec((NSA_TP, LANES), lambda b, i: (i, 0)),
                  c2((1, LANES)), c2((1, LANES)), c2((1, LANES)), c2((LANES, LANES))],
        out_specs=(pl.BlockSpec((None, NSA_TP, NSA_W), lambda b, i: (b, i, 0)),
                   kv_spec, kv_spec, kv_spec, kv_spec, raw_spec, raw_spec),
        compiler_params=_params(("parallel", "parallel")),
        name="nsa_prep",
    )(nsa, cos, sin, w["nsa_gq"], w["nsa_gks"], w["nsa_gkw"], gsum)

    wide = CMP_STRIDE * LANES
    xk = kc_raw.reshape(bsz, rows, wide)
    xv = vc_raw.reshape(bsz, rows, wide)
    c1 = lambda shape: pl.BlockSpec(shape, lambda b: (0,) * len(shape))
    cmp_shape = jax.ShapeDtypeStruct((bsz, NSA_KVH, rows, LANES), BF16)
    cmp_spec = pl.BlockSpec((None, NSA_KVH, rows, LANES), lambda b: (b, 0, 0, 0))
    hid2 = NSA_KVH * CMP_HIDDEN
    kcd, vcd = pl.pallas_call(
        _nsa_cmp_kernel,
        out_shape=(cmp_shape, cmp_shape),
        grid=(bsz,),
        in_specs=[pl.BlockSpec((None, rows, wide), lambda b: (b, 0, 0)),
                  pl.BlockSpec((None, rows, wide), lambda b: (b, 0, 0)),
                  c1((2, wide)), c1((2, wide)),
                  c1((wide, hid2)), c1((wide, hid2)), c1((wide, hid2)), c1((wide, hid2)),
                  c1((1, hid2)), c1((1, hid2)), c1((hid2, LANES)), c1((hid2, LANES)),
                  c1((1, LANES)), c1((rows, LANES)), c1((rows, LANES)), c1((LANES, LANES))],
        out_specs=(cmp_spec, cmp_spec),
        compiler_params=_params(("parallel",)),
        name="nsa_compress",
    )(xk, xv, w["cmp_pe_k"], w["cmp_pe_v"], w["cmp_w1k_a"], w["cmp_w1k_b"], w["cmp_w1v_a"], w["cmp_w1v_b"],
      w["cmp_b1_k"], w["cmp_b1_v"], w["cmp_w2_k"], w["cmp_w2_v"], w["nsa_gkc"], cos_c, sin_c, gsum)

    TQ = NSA_TQ
    half_w = NSA_GROUP * NSA_DH
    c3 = lambda shape: pl.BlockSpec(shape, lambda b, h, i: (0,) * len(shape))
    res = lambda r: pl.BlockSpec((None, None, r, LANES), lambda b, h, i: (b, h, 0, 0))
    stat = lambda: pltpu.VMEM((NSA_GROUP, TQ, 1), F32)
    accu = lambda: pltpu.VMEM((NSA_GROUP, TQ, LANES), F32)
    return pl.pallas_call(
        functools.partial(_nsa_attn_kernel, n_cmp=n_cmp, n_sel=n_sel),
        out_shape=jax.ShapeDtypeStruct((bsz, seq, NSA_W), BF16),
        grid=(bsz, NSA_KVH, seq // TQ),
        in_specs=[pl.BlockSpec((None, TQ, half_w), lambda b, h, i: (b, i, h)),
                  res(rows), res(rows), res(seq), res(seq), res(seq), res(seq),
                  pl.BlockSpec((None, TQ, LANES), lambda b, h, i: (b, i, NSA_OFF_GATE // LANES)),
                  c3((LANES, rows)), c3((seq // TQ, LANES, TQ)),
                  pl.BlockSpec((None, LANES, N_BRANCH * half_w), lambda b, h, i: (h, 0, 0))],
        out_specs=pl.BlockSpec((None, TQ, half_w), lambda b, h, i: (b, i, h)),
        scratch_shapes=[pltpu.VMEM((NSA_GROUP, TQ, LANES), BF16), accu(),
                        stat(), stat(), accu(), stat(), stat(), accu()],
        compiler_params=_params(("parallel", "parallel", "arbitrary")),
        name="nsa_attention",
    )(qn, kcd, vcd, ksd, vsd, kwd, vwd, nsa, mt, esel, eg)


POST_TM = 256


def _rms(v, gain):
    return v * lax.rsqrt(jnp.mean(v * v, axis=-1, keepdims=True) + EPS) * gain


def _memkv_kernel(mem_ref, g_ref, w_ref, gk_ref, k_out, v_out):
    u = _rms(mem_ref[...], g_ref[...]).astype(BF16)
    kv = _dot(u, w_ref[...])
    for h in range(MEM_HEADS):
        cs = slice(h * MEM_DH, (h + 1) * MEM_DH)
        k_out[:, cs] = _rms(kv[:, cs], gk_ref[...]).astype(BF16)
    v_out[...] = kv[:, MEM_W:2 * MEM_W].astype(BF16)


def _memkv(mem, g_mem, w_mem_kv, g_k):
    bsz, m_len, _ = mem.shape
    c1 = lambda shape: pl.BlockSpec(shape, lambda b: (0,) * len(shape))
    shape = jax.ShapeDtypeStruct((bsz, m_len, MEM_W), BF16)
    spec = pl.BlockSpec((None, m_len, MEM_W), lambda b: (b, 0, 0))
    return pl.pallas_call(
        _memkv_kernel, out_shape=(shape, shape), grid=(bsz,),
        in_specs=[pl.BlockSpec((None, m_len, D_MODEL), lambda b: (b, 0, 0)),
                  c1((1, D_MODEL)), c1((D_MODEL, 2 * MEM_W)), c1((1, MEM_DH))],
        out_specs=(spec, spec),
        compiler_params=_params(("parallel",)),
        name="mem_kv",
    )(mem, g_mem, w_mem_kv, g_k)


def _post_kernel(x_ref, hml_ref, hnsa_ref, mk_ref, mv_ref, gmix_ref, wmq_ref, gmq_ref, wgate_ref,
                 wml_ref, wnsa_ref, wmem_ref, wout_ref, gffn_ref, wrh_ref, wrl_ref, br_ref,
                 x1_out, t_out, logit_out):
    x = x_ref[...]
    ub = _rms(x, gmix_ref[...]).astype(BF16)
    mq = _dot(ub, wmq_ref[...])
    heads = []
    for h in range(MEM_HEADS):
        cs = slice(h * MEM_DH, (h + 1) * MEM_DH)
        qn = (_rms(mq[:, cs], gmq_ref[...]) * (MEM_DH ** -0.5)).astype(BF16)
        s = _dot_nt(qn, mk_ref[:, cs])
        e = jnp.exp(s - jnp.max(s, axis=1, keepdims=True))
        p = e / jnp.sum(e, axis=1, keepdims=True)
        heads.append(_dot(p.astype(BF16), mv_ref[:, cs]))
    hmem = jnp.concatenate(heads, axis=1).astype(BF16)
    gates = _sigmoid(_dot(ub, wgate_ref[...]))
    merged = (gates[:, 0:D_MODEL] * _dot(hml_ref[...], wml_ref[...])
              + gates[:, D_MODEL:2 * D_MODEL] * _dot(hnsa_ref[...], wnsa_ref[...])
              + gates[:, 2 * D_MODEL:3 * D_MODEL] * _dot(hmem, wmem_ref[...]))
    x1 = x + _dot(merged.astype(BF16), wout_ref[...])
    x1_out[...] = x1
    t = _rms(x1, gffn_ref[...])
    t_out[...] = t
    t1, t2, _ = _split3(t)
    logit_out[...] = _dot(t1, wrh_ref[...]) + _dot(t1, wrl_ref[...]) + _dot(t2, wrh_ref[...]) + br_ref[...]


def _post(x2, hml, hnsa, mk, mv, w, seq):
    n = x2.shape[0]
    tm = POST_TM
    assert seq % tm == 0
    per = seq // tm
    m_len = mk.shape[1]
    row = lambda width: pl.BlockSpec((tm, width), lambda i: (i, 0))
    c1 = lambda shape: pl.BlockSpec(shape, lambda i: (0,) * len(shape))
    memspec = pl.BlockSpec((None, m_len, MEM_W), lambda i: (i // per, 0, 0))
    return pl.pallas_call(
        _post_kernel,
        out_shape=(jax.ShapeDtypeStruct((n, D_MODEL), F32), jax.ShapeDtypeStruct((n, D_MODEL), F32),
                   jax.ShapeDtypeStruct((n, LANES), F32)),
        grid=(n // tm,),
        in_specs=[row(D_MODEL), row(ML_W), row(NSA_W), memspec, memspec,
                  c1((1, D_MODEL)), c1((D_MODEL, MEM_W)), c1((1, MEM_DH)), c1((D_MODEL, N_BRANCH * D_MODEL)),
                  c1((ML_W, D_MODEL)), c1((NSA_W, D_MODEL)), c1((MEM_W, D_MODEL)), c1((D_MODEL, D_MODEL)),
                  c1((1, D_MODEL)), c1((D_MODEL, LANES)), c1((D_MODEL, LANES)), c1((1, LANES))],
        out_specs=(row(D_MODEL), row(D_MODEL), row(LANES)),
        compiler_params=_params(("parallel",)),
        name="post",
    )(x2, hml, hnsa, mk, mv, w["g_mix"], w["w_mq"], w["mem_gq"], w["w_gate"], w["w_br_ml"], w["w_br_nsa"],
      w["w_br_mem"], w["w_out"], w["g_ffn"], w["w_router_hi"], w["w_router_lo"], w["b_router"])


ROUTE_T = 512
MOE_BLK = 256
MOE_TD = 256


def _route_kernel(logit_ref, triu_ref, e_out, w_out, p_out, wtm_out, cnt_out, carry_ref):
    T = ROUTE_T

    @pl.when(pl.program_id(0) == 0)
    def _():
        carry_ref[...] = jnp.zeros(carry_ref.shape, F32)

    sc = logit_ref[...].T[0:N_EXPERTS, :]
    j_i = lax.broadcasted_iota(jnp.int32, (N_EXPERTS, T), 0)
    rank = jnp.zeros((N_EXPERTS, T), jnp.int32)
    for jp in range(N_EXPERTS):
        row = sc[jp:jp + 1, :]
        beats = (row > sc) | ((row == sc) & (j_i > jp))
        rank = rank + beats.astype(jnp.int32)
    sel = rank < TOP_K
    e = jnp.where(sel, jnp.exp(sc - jnp.max(sc, axis=0, keepdims=True)), 0.0)
    wgt = e / jnp.sum(e, axis=0, keepdims=True)
    self = sel.astype(F32)
    pos = _dot(self.astype(BF16), triu_ref[...]) + carry_ref[:, 0:1]
    carry_ref[...] = carry_ref[...] + jnp.sum(self, axis=1, keepdims=True)
    e_rows, w_rows, p_rows = [], [], []
    for r in range(TOP_K):
        hit = rank == r
        e_rows.append(jnp.sum(jnp.where(hit, j_i, 0), axis=0, keepdims=True))
        w_rows.append(jnp.sum(jnp.where(hit, wgt, 0.0), axis=0, keepdims=True))
        p_rows.append(jnp.sum(jnp.where(hit, pos, 0.0), axis=0, keepdims=True))
    e_out[...] = jnp.concatenate(e_rows, axis=0)
    w4 = jnp.concatenate(w_rows, axis=0)
    w_out[...] = w4
    p_out[...] = jnp.concatenate(p_rows, axis=0).astype(jnp.int32)
    wtm_out[...] = jnp.concatenate([w4, jnp.zeros((LANES - TOP_K, T), F32)], axis=0).T
    cnt_out[...] = carry_ref[...]


def _route(logits):
    n = logits.shape[0]
    T = ROUTE_T
    assert n % T == 0
    s = np.arange(T)
    triu = jnp.asarray(s[:, None] < s[None, :], BF16)
    row4 = lambda: pl.BlockSpec((TOP_K, T), lambda i: (0, i))
    return pl.pallas_call(
        _route_kernel,
        out_shape=(jax.ShapeDtypeStruct((TOP_K, n), jnp.int32), jax.ShapeDtypeStruct((TOP_K, n), F32),
                   jax.ShapeDtypeStruct((TOP_K, n), jnp.int32), jax.ShapeDtypeStruct((n, LANES), F32),
                   jax.ShapeDtypeStruct((N_EXPERTS, LANES), F32)),
        grid=(n // T,),
        in_specs=[pl.BlockSpec((T, LANES), lambda i: (i, 0)), pl.BlockSpec((T, T), lambda i: (0, 0))],
        out_specs=(row4(), row4(), row4(), pl.BlockSpec((T, LANES), lambda i: (i, 0)),
                   pl.BlockSpec((N_EXPERTS, LANES), lambda i: (0, 0))),
        scratch_shapes=[pltpu.VMEM((N_EXPERTS, LANES), F32)],
        compiler_params=_params(("arbitrary",)),
        name="moe_route",
    )(logits, triu)


def _dispatch_kernel(dest_ref, t_ref, xs_in, xs_out, sem):
    del xs_in

    def row_copy(j, d):
        return pltpu.make_async_copy(t_ref.at[pl.ds(j, 1)], xs_out.at[pl.ds(d, 1)], sem)

    def issue(j, carry):
        for r in range(TOP_K):
            row_copy(j, dest_ref[r, j]).start()
        return carry

    lax.fori_loop(0, MOE_TD, issue, 0)

    def drain(j, carry):
        for r in range(TOP_K):
            row_copy(j, dest_ref[r, j]).wait()
        return carry

    lax.fori_loop(0, MOE_TD, drain, 0)


def _dispatch(t, dest3, cap):
    n = t.shape[0]
    zeros = jnp.zeros((cap, D_MODEL), F32)
    return pl.pallas_call(
        _dispatch_kernel,
        out_shape=jax.ShapeDtypeStruct((cap, D_MODEL), F32),
        grid=(n // MOE_TD,),
        in_specs=[pl.BlockSpec((None, TOP_K, MOE_TD), lambda i: (i, 0, 0), memory_space=pltpu.SMEM),
                  pl.BlockSpec((MOE_TD, D_MODEL), lambda i: (i, 0)),
                  pl.BlockSpec(memory_space=pl.ANY)],
        out_specs=pl.BlockSpec(memory_space=pl.ANY),
        scratch_shapes=[pltpu.SemaphoreType.DMA],
        input_output_aliases={2: 0},
        compiler_params=_params(("arbitrary",)),
        name="moe_dispatch",
    )(dest3, t, zeros)


def _ffn_kernel(be_ref, nu_ref, xs_ref, wg_ref, wu_ref, bg_ref, bu_ref, wd_ref, bd_ref, ys_ref):
    del be_ref
    used = pl.program_id(0) < nu_ref[0]

    @pl.when(used)
    def _():
        x = xs_ref[...].astype(BF16)
        gate = jnp.minimum(_dot(x, wg_ref[...]) + bg_ref[...], SWIGLU_LIMIT)
        up = jnp.clip(_dot(x, wu_ref[...]) + bu_ref[...], -SWIGLU_LIMIT, SWIGLU_LIMIT)
        act = (up + 1.0) * (gate * _sigmoid(SWIGLU_ALPHA * gate))
        ys_ref[...] = _dot(act.astype(BF16), wd_ref[...]) + bd_ref[...]

    @pl.when(jnp.logical_not(used))
    def _():
        ys_ref[...] = jnp.zeros(ys_ref.shape, F32)


def _ffn(xs, blk_e, n_used, w):
    cap = xs.shape[0]
    wspec = lambda: pl.BlockSpec((None, D_MODEL, D_FF), lambda i, be, nu: (be[i], 0, 0))
    bspec = lambda: pl.BlockSpec((None, 1, D_FF), lambda i, be, nu: (be[i], 0, 0))
    rows = pl.BlockSpec((MOE_BLK, D_MODEL), lambda i, be, nu: (i, 0))
    return pl.pallas_call(
        _ffn_kernel,
        out_shape=jax.ShapeDtypeStruct((cap, D_MODEL), F32),
        grid_spec=pltpu.PrefetchScalarGridSpec(
            num_scalar_prefetch=2, grid=(cap // MOE_BLK,),
            in_specs=[rows, wspec(), wspec(), bspec(), bspec(), wspec(), bspec()],
            out_specs=rows),
        compiler_params=_params(("arbitrary",)),
        name="moe_ffn",
    )(blk_e, n_used, xs, w["w_g"], w["w_u"], w["b_g"], w["b_u"], w["w_down"], w["b_down"])


def _combine_kernel(dest_ref, x1_ref, wtm_ref, ys_ref, out_ref, buf_ref, sem):
    def row_copy(r, j, d):
        return pltpu.make_async_copy(ys_ref.at[pl.ds(d, 1)], buf_ref.at[r, pl.ds(j, 1)], sem)

    def issue(j, carry):
        for r in range(TOP_K):
            row_copy(r, j, dest_ref[r, j]).start()
        return carry

    lax.fori_loop(0, MOE_TD, issue, 0)

    def drain(j, carry):
        for r in range(TOP_K):
            row_copy(r, j, dest_ref[r, j]).wait()
        return carry

    lax.fori_loop(0, MOE_TD, drain, 0)
    acc = x1_ref[...]
    wtm = wtm_ref[...]
    for r in range(TOP_K):
        acc = acc + wtm[:, r:r + 1] * buf_ref[r]
    out_ref[...] = acc


def _combine(x1, wtm, dest3, ys):
    n = x1.shape[0]
    row = lambda width: pl.BlockSpec((MOE_TD, width), lambda i: (i, 0))
    return pl.pallas_call(
        _combine_kernel,
        out_shape=jax.ShapeDtypeStruct((n, D_MODEL), F32),
        grid=(n // MOE_TD,),
        in_specs=[pl.BlockSpec((None, TOP_K, MOE_TD), lambda i: (i, 0, 0), memory_space=pltpu.SMEM),
                  row(D_MODEL), row(LANES), pl.BlockSpec(memory_space=pl.ANY)],
        out_specs=row(D_MODEL),
        scratch_shapes=[pltpu.VMEM((TOP_K, MOE_TD, D_MODEL), F32), pltpu.SemaphoreType.DMA],
        compiler_params=_params(("arbitrary",)),
        name="moe_combine",
    )(dest3, x1, wtm, ys)


def _moe(x1, t, logits, w):
    n = x1.shape[0]
    e_idx, _, pos, wtm, cnt = _route(logits)
    counts = cnt[:, 0].astype(jnp.int32)
    padded = (counts + MOE_BLK - 1) // MOE_BLK * MOE_BLK
    pad_end = jnp.cumsum(padded)
    pad_start = pad_end - padded
    cap = -(-(n * TOP_K + N_EXPERTS * (MOE_BLK - 1)) // MOE_BLK) * MOE_BLK
    dest = pad_start[e_idx] + pos
    dest3 = dest.reshape(TOP_K, n // MOE_TD, MOE_TD).transpose(1, 0, 2)
    blk_e = jnp.minimum(jnp.searchsorted(pad_end, jnp.arange(cap // MOE_BLK) * MOE_BLK, side="right"),
                        N_EXPERTS - 1).astype(jnp.int32)
    n_used = (pad_end[-1:] // MOE_BLK).astype(jnp.int32)
    xs = _dispatch(t, dest3, cap)
    ys = _ffn(xs, blk_e, n_used, w)
    return _combine(x1, wtm, dest3, ys)


def _cmp_weights(pos, w1, b1, w2):
    eye = jnp.eye(NSA_KVH, dtype=F32)
    halves = []
    for part in (w1[:CMP_STRIDE], w1[CMP_STRIDE:]):
        big = jnp.einsum("ldf,hg->lhdgf", part, eye)
        halves.append(big.reshape(CMP_STRIDE * LANES, NSA_KVH * CMP_HIDDEN).astype(BF16))
    pe = jnp.stack([jnp.broadcast_to(part[:, None, :], (CMP_STRIDE, NSA_KVH, NSA_DH)).reshape(-1)
                    for part in (pos[:CMP_STRIDE], pos[CMP_STRIDE:])])
    w2big = jnp.einsum("fd,hg->hfgd", w2, eye).reshape(NSA_KVH * CMP_HIDDEN, LANES).astype(BF16)
    return halves[0], halves[1], pe, jnp.tile(b1, NSA_KVH).reshape(1, -1), w2big


def _prep_weights(p):
    w = {}
    offs = np.concatenate([[0], np.cumsum(IN_SPLITS)])
    col = lambda k: p["w_in"][:, offs[k]:offs[k + 1]]
    zpad = lambda n: jnp.zeros((D_MODEL, n), F32)
    w["w_ml"] = jnp.concatenate([col(0), col(1), col(2), col(3), col(4), col(5),
                                 zpad(LANES - 2 * ML_HEADS)], axis=1).astype(BF16)
    w["w_nsa"] = jnp.concatenate([col(k) for k in range(6, 14)] + [zpad(LANES - NSA_QH * 3)], axis=1).astype(BF16)
    w["g_mix"] = p["g_mix"].reshape(1, D_MODEL)
    w["conv_w"] = p["ml_conv_w"]
    w["conv_b"] = p["ml_conv_b"].reshape(1, 2 * ML_W)
    w["ml_gate_b"] = jnp.concatenate([p["ml_b_i"], p["ml_b_f"], jnp.zeros((LANES - 2 * ML_HEADS,), F32)]).reshape(1, LANES)
    w["ml_g_out"] = p["ml_g_out"]

    pair = lambda g: jnp.tile(g, LANES // NSA_DH).reshape(1, LANES)
    w["nsa_gq"] = pair(p["nsa_g_q"])
    w["nsa_gkc"], w["nsa_gks"], w["nsa_gkw"] = (pair(p["nsa_g_k"][k]) for k in range(3))
    (w["cmp_w1k_a"], w["cmp_w1k_b"], w["cmp_pe_k"], w["cmp_b1_k"], w["cmp_w2_k"]) = _cmp_weights(
        p["cmp_pos_k"], p["cmp_w1_k"], p["cmp_b1_k"], p["cmp_w2_k"])
    (w["cmp_w1v_a"], w["cmp_w1v_b"], w["cmp_pe_v"], w["cmp_b1_v"], w["cmp_w2_v"]) = _cmp_weights(
        p["cmp_pos_v"], p["cmp_w1_v"], p["cmp_b1_v"], p["cmp_w2_v"])

    w["g_mem"] = p["g_mem"].reshape(1, D_MODEL)
    w["w_mem_kv"] = p["w_mem_kv"].astype(BF16)
    w["mem_gq"] = p["mem_g_q"].reshape(1, MEM_DH)
    w["mem_gk"] = p["mem_g_k"].reshape(1, MEM_DH)
    w["w_mq"] = col(14).astype(BF16)
    w["w_gate"] = col(15).astype(BF16)
    for k in ("w_br_ml", "w_br_nsa", "w_br_mem", "w_out"):
        w[k] = p[k].astype(BF16)
    w["g_ffn"] = p["g_ffn"].reshape(1, D_MODEL)
    w_r = jnp.concatenate([p["w_router"], jnp.zeros((D_MODEL, LANES - N_EXPERTS), F32)], axis=1)
    w["w_router_hi"] = w_r.astype(BF16)
    w["w_router_lo"] = (w_r - w["w_router_hi"].astype(F32)).astype(BF16)
    w["b_router"] = jnp.concatenate([p["b_router"], jnp.zeros((LANES - N_EXPERTS,), F32)]).reshape(1, LANES)
    w["w_g"] = p["w_gu"][:, :, 0::2].astype(BF16)
    w["w_u"] = p["w_gu"][:, :, 1::2].astype(BF16)
    w["b_g"] = p["b_gu"][:, None, 0::2]
    w["b_u"] = p["b_gu"][:, None, 1::2]
    w["w_down"] = p["w_down"].astype(BF16)
    w["b_down"] = p["b_down"][:, None, :]
    return w


_PARAM_NAMES = ("g_mix", "w_in", "ml_conv_w", "ml_conv_b", "ml_b_i", "ml_b_f", "ml_g_out", "nsa_g_q", "nsa_g_k",
                "cmp_pos_k", "cmp_w1_k", "cmp_b1_k", "cmp_w2_k", "cmp_pos_v", "cmp_w1_v", "cmp_b1_v", "cmp_w2_v",
                "g_mem", "w_mem_kv", "mem_g_q", "mem_g_k", "w_br_ml", "w_br_nsa", "w_br_mem", "w_out", "g_ffn",
                "w_router", "b_router", "w_gu", "b_gu", "w_down", "b_down")


def _layer(x, mem, p):
    bsz, seq, d = x.shape
    w = _prep_weights(p)
    x2 = x.reshape(bsz * seq, d)
    ml, nsa = _proj(x2, w["g_mix"], w["w_ml"], w["w_nsa"])
    h_ml = _mlstm(ml.reshape(bsz, seq, ML_COLS), w["conv_w"], w["conv_b"], w["ml_gate_b"], w["ml_g_out"], bsz, seq)
    h_nsa = _nsa(nsa.reshape(bsz, seq, NSA_COLS), w, bsz, seq)
    mk, mv = _memkv(mem, w["g_mem"], w["w_mem_kv"], w["mem_gk"])
    x1, t, logits = _post(x2, h_ml.reshape(bsz * seq, ML_W), h_nsa.reshape(bsz * seq, NSA_W), mk, mv, w, seq)
    return _moe(x1, t, logits, w).reshape(bsz, seq, d)


def kernel(x, mem, g_mix, w_in, ml_conv_w, ml_conv_b, ml_b_i, ml_b_f, ml_g_out, nsa_g_q, nsa_g_k, cmp_pos_k,
           cmp_w1_k, cmp_b1_k, cmp_w2_k, cmp_pos_v, cmp_w1_v, cmp_b1_v, cmp_w2_v, g_mem, w_mem_kv, mem_g_q,
           mem_g_k, w_br_ml, w_br_nsa, w_br_mem, w_out, g_ffn, w_router, b_router, w_gu, b_gu, w_down, b_down):
    stacked = (g_mix, w_in, ml_conv_w, ml_conv_b, ml_b_i, ml_b_f, ml_g_out, nsa_g_q, nsa_g_k, cmp_pos_k,
               cmp_w1_k, cmp_b1_k, cmp_w2_k, cmp_pos_v, cmp_w1_v, cmp_b1_v, cmp_w2_v, g_mem, w_mem_kv, mem_g_q,
               mem_g_k, w_br_ml, w_br_nsa, w_br_mem, w_out, g_ffn, w_router, b_router, w_gu, b_gu, w_down, b_down)
    h = x
    for layer in range(g_mix.shape[0]):
        h = _layer(h, mem, {name: v[layer] for name, v in zip(_PARAM_NAMES, stacked)})
    return h
```

```python
import functools

import jax
import jax.numpy as jnp
import numpy as np
from jax import lax
from jax.experimental import pallas as pl
from jax.experimental.pallas import tpu as pltpu

F32 = jnp.float32
BF16 = jnp.bfloat16

D_MODEL = 1024
ML_HEADS = 4
ML_DQK = 128
ML_W = ML_HEADS * ML_DQK
ML_CHUNK = 64
ML_CONV = 4
NSA_QH = 8
NSA_KVH = 2
NSA_DH = 64
NSA_GROUP = NSA_QH // NSA_KVH
NSA_W = NSA_QH * NSA_DH
NSA_KV_W = NSA_KVH * NSA_DH
CMP_BLOCK = 32
CMP_STRIDE = 16
CMP_HIDDEN = 256
SEL_BLOCK = 64
SEL_TOPN = 8
SEL_FORCE_BONUS = 1e4
WINDOW = 512
MEM_HEADS = 4
MEM_DH = 128
MEM_W = MEM_HEADS * MEM_DH
N_EXPERTS = 32
TOP_K = 4
D_FF = D_MODEL
SWIGLU_ALPHA = 1.702
SWIGLU_LIMIT = 7.0
N_BRANCH = 3
ROPE_THETA = 10000.0
EPS = 1e-6
NEG = -1e30

LANES = 128
VMEM_LIMIT = 56 * 1024 * 1024

IN_SPLITS = (ML_W, ML_W, ML_W, ML_W, ML_HEADS, ML_HEADS,
             NSA_W, NSA_KV_W, NSA_KV_W, NSA_KV_W, NSA_KV_W, NSA_KV_W, NSA_KV_W, NSA_QH * 3,
             MEM_W, N_BRANCH * D_MODEL)

ML_COLS = 4 * ML_W + LANES
NSA_COLS = NSA_W + 6 * NSA_KV_W + LANES


def _dot(a, b):
    return jnp.dot(a, b, preferred_element_type=F32)


def _dot_nt(a, b):
    return lax.dot_general(a, b, (((1,), (1,)), ((), ())), preferred_element_type=F32)


def _dot_tn(a, b):
    return lax.dot_general(a, b, (((0,), (0,)), ((), ())), preferred_element_type=F32)


def _split3(v):
    a = v.astype(BF16)
    r = v - a.astype(F32)
    b = r.astype(BF16)
    c = (r - b.astype(F32)).astype(BF16)
    return a, b, c


def _sigmoid(v):
    return 1.0 / (1.0 + jnp.exp(-v))


def _log_sigmoid(v):
    return jnp.minimum(v, 0.0) - jnp.log1p(jnp.exp(-jnp.abs(v)))


def _params(sem):
    return pltpu.CompilerParams(dimension_semantics=sem, vmem_limit_bytes=VMEM_LIMIT)


PROJ_TM = 512


def _proj_kernel(x_ref, g_ref, wml_ref, wnsa_ref, oml_ref, onsa_ref):
    x = x_ref[...]
    u = x * lax.rsqrt(jnp.mean(x * x, axis=-1, keepdims=True) + EPS) * g_ref[...]
    ub = u.astype(BF16)
    oml_ref[...] = _dot(ub, wml_ref[...])
    onsa_ref[...] = _dot(ub, wnsa_ref[...])


def _proj(x2, g_mix, w_ml, w_nsa):
    n = x2.shape[0]
    tm = min(PROJ_TM, n)
    return pl.pallas_call(
        _proj_kernel,
        out_shape=(jax.ShapeDtypeStruct((n, ML_COLS), F32), jax.ShapeDtypeStruct((n, NSA_COLS), F32)),
        grid=(n // tm,),
        in_specs=[pl.BlockSpec((tm, D_MODEL), lambda i: (i, 0)),
                  pl.BlockSpec((1, D_MODEL), lambda i: (0, 0)),
                  pl.BlockSpec((D_MODEL, ML_COLS), lambda i: (0, 0)),
                  pl.BlockSpec((D_MODEL, NSA_COLS), lambda i: (0, 0))],
        out_specs=(pl.BlockSpec((tm, ML_COLS), lambda i: (i, 0)),
                   pl.BlockSpec((tm, NSA_COLS), lambda i: (i, 0))),
        compiler_params=_params(("parallel",)),
        name="in_proj",
    )(x2, g_mix, w_ml, w_nsa)


ML_T = 256
CONV_PAD = 8


def _mlstm_kernel(q_ref, k_ref, v_ref, o_ref, g_ref, cw_ref, cb_ref, gb_ref, gout_ref, tri_ref,
                  out_ref, buf_ref, c_ref, n_ref, m_ref):
    T, L = ML_T, ML_CHUNK

    @pl.when(pl.program_id(1) == 0)
    def _():
        buf_ref[0:CONV_PAD, :] = jnp.zeros((CONV_PAD, 2 * ML_W), F32)
        c_ref[...] = jnp.zeros(c_ref.shape, F32)
        n_ref[...] = jnp.zeros(n_ref.shape, F32)
        m_ref[...] = jnp.zeros(m_ref.shape, F32)

    buf_ref[CONV_PAD:CONV_PAD + T, 0:ML_W] = q_ref[...]
    buf_ref[CONV_PAD:CONV_PAD + T, ML_W:2 * ML_W] = k_ref[...]
    y = cb_ref[...]
    for j in range(ML_CONV):
        lo = CONV_PAD - (ML_CONV - 1) + j
        y = y + cw_ref[j:j + 1, :] * buf_ref[lo:lo + T, :]
    buf_ref[0:CONV_PAD, :] = buf_ref[T:T + CONV_PAD, :]
    qk = y * _sigmoid(y)
    qf = qk[:, 0:ML_W] * (ML_DQK ** -0.5)
    kf = qk[:, ML_W:2 * ML_W]

    gg = g_ref[...] + gb_ref[...]
    gt = gg.T
    tri = tri_ref[...]
    lf1, lf2, lf3 = _split3(_log_sigmoid(gg))
    b_col = _dot(tri, lf1) + _dot(tri, lf2) + _dot(tri, lf3)
    lt1, lt2, lt3 = _split3(_log_sigmoid(gt[0:8, :]))
    b_row = _dot_nt(lt1, tri) + _dot_nt(lt2, tri) + _dot_nt(lt3, tri)

    r_i = lax.broadcasted_iota(jnp.int32, (L, L), 0)
    c_i = lax.broadcasted_iota(jnp.int32, (L, L), 1)
    causal = c_i <= r_i

    for c in range(T // L):
        rows = slice(c * L, (c + 1) * L)
        for h in range(ML_HEADS):
            cols = slice(h * ML_DQK, (h + 1) * ML_DQK)
            bc = b_col[rows, ML_HEADS + h:ML_HEADS + h + 1]
            igc = gg[rows, h:h + 1]
            br = b_row[ML_HEADS + h:ML_HEADS + h + 1, rows]
            igr = gt[h:h + 1, rows]
            q_c = qf[rows, cols]
            k_c = kf[rows, cols]
            v_c = v_ref[rows, cols]
            q_b = q_c.astype(BF16)
            c_prev = c_ref[h]
            n_prev = n_ref[h]
            m_prev = m_ref[h][:, 0:1]

            d_log = jnp.where(causal, bc - br + igr, NEG)
            a_log = bc + m_prev
            m_t = jnp.maximum(a_log, jnp.max(d_log, axis=1, keepdims=True))
            wd = jnp.exp(d_log - m_t) * _dot_nt(q_b, k_c.astype(BF16))
            inter = jnp.exp(a_log - m_t)
            num = inter * _dot(q_b, c_prev.astype(BF16)) + _dot(wd.astype(BF16), v_c.astype(BF16))
            den = inter * jnp.sum(q_c * n_prev, axis=1, keepdims=True) + jnp.sum(wd, axis=1, keepdims=True)
            hh = num / jnp.maximum(jnp.abs(den), jnp.exp(-m_t))

            bl = bc[L - 1:L, :]
            g_col = bl - bc + igc
            m_c = jnp.max(bl - br + igr, axis=1, keepdims=True)
            w_col = jnp.exp(g_col - m_c)
            kw = k_c * w_col
            kv_c = _dot_tn(kw.astype(BF16), v_c.astype(BF16))
            n_c = jnp.sum(kw, axis=0, keepdims=True)
            m_new = jnp.maximum(bl + m_prev, m_c)
            a_old = jnp.exp(bl + m_prev - m_new)
            a_new = jnp.exp(m_c - m_new)
            c_ref[h] = a_old * c_prev + a_new * kv_c
            n_ref[h] = a_old * n_prev + a_new * n_c
            m_ref[h] = jnp.broadcast_to(m_new, (1, LANES))

            hn = hh * lax.rsqrt(jnp.mean(hh * hh, axis=-1, keepdims=True) + EPS) * gout_ref[h:h + 1, :]
            out_ref[rows, cols] = (hn * _sigmoid(o_ref[rows, cols])).astype(BF16)


def _mlstm(ml, conv_w, conv_b, gate_b, g_out, bsz, seq):
    T = min(ML_T, seq)
    assert T == ML_T and seq % T == 0
    t_i = np.arange(T)
    tri = ((t_i[:, None] // ML_CHUNK == t_i[None, :] // ML_CHUNK) & (t_i[None, :] <= t_i[:, None]))
    tri = jnp.asarray(tri, BF16)
    wblk = lambda j: pl.BlockSpec((None, T, ML_W), lambda b, i, j=j: (b, i, j))
    full = lambda shape: pl.BlockSpec(shape, lambda b, i: (0,) * len(shape))
    return pl.pallas_call(
        _mlstm_kernel,
        out_shape=jax.ShapeDtypeStruct((bsz, seq, ML_W), BF16),
        grid=(bsz, seq // T),
        in_specs=[wblk(0), wblk(1), wblk(2), wblk(3),
                  pl.BlockSpec((None, T, LANES), lambda b, i: (b, i, 4 * ML_W // LANES)),
                  full((ML_CONV, 2 * ML_W)), full((1, 2 * ML_W)), full((1, LANES)),
                  full((ML_HEADS, ML_DQK)), full((T, T))],
        out_specs=pl.BlockSpec((None, T, ML_W), lambda b, i: (b, i, 0)),
        scratch_shapes=[pltpu.VMEM((CONV_PAD + T, 2 * ML_W), F32),
                        pltpu.VMEM((ML_HEADS, ML_DQK, ML_DQK), F32),
                        pltpu.VMEM((ML_HEADS, 1, ML_DQK), F32),
                        pltpu.VMEM((ML_HEADS, 1, LANES), F32)],
        compiler_params=_params(("parallel", "arbitrary")),
        name="mlstm",
    )(ml, ml, ml, ml, ml, conv_w, conv_b, gate_b, g_out, tri)


NSA_TP = 256
NSA_TQ = 256
NSA_OFF_KC, NSA_OFF_VC, NSA_OFF_KS, NSA_OFF_VS, NSA_OFF_KW, NSA_OFF_VW, NSA_OFF_GATE = (
    NSA_W + k * NSA_KV_W for k in range(7))


def _rope_tables(pos):
    half = NSA_DH // 2
    inv_freq = jnp.power(ROPE_THETA, -jnp.arange(half, dtype=F32) / half)
    ang = pos.astype(F32)[:, None] * inv_freq[None, :]
    cos, sin = jnp.cos(ang), jnp.sin(ang)
    cos = jnp.concatenate([cos, cos], axis=1)
    sin = jnp.concatenate([-sin, sin], axis=1)
    return jnp.tile(cos, (1, 2)), jnp.tile(sin, (1, 2))


def _head_norm_rope(xs, gsum, gain, cos, sin):
    a, b, c = _split3(xs * xs)
    ss = _dot(a, gsum) + _dot(b, gsum) + _dot(c, gsum)
    xn = xs * lax.rsqrt(ss * (1.0 / NSA_DH) + EPS) * gain
    lane = lax.broadcasted_iota(jnp.int32, xs.shape, 1)
    first = (lane % NSA_DH) < NSA_DH // 2
    swapped = jnp.where(first, pltpu.roll(xn, LANES - NSA_DH // 2, axis=1), pltpu.roll(xn, NSA_DH // 2, axis=1))
    return xn * cos + swapped * sin


def _head_norm_rope_t(xt, gsum, gain, cos_t, sin_t):
    a, b, c = _split3(xt * xt)
    ss = _dot(gsum, a) + _dot(gsum, b) + _dot(gsum, c)
    xn = xt * lax.rsqrt(ss * (1.0 / NSA_DH) + EPS) * gain
    row = lax.broadcasted_iota(jnp.int32, xt.shape, 0)
    first = (row % NSA_DH) < NSA_DH // 2
    swapped = jnp.where(first, pltpu.roll(xn, LANES - NSA_DH // 2, axis=0), pltpu.roll(xn, NSA_DH // 2, axis=0))
    return xn * cos_t + swapped * sin_t


def _store_keys(out, slab):
    lane = lax.broadcasted_iota(jnp.int32, slab.shape, 1)
    low = lane < NSA_DH
    rolled = pltpu.roll(slab, NSA_DH, axis=1)
    zero = jnp.zeros_like(slab)
    out[0, 0] = jnp.where(low, slab, zero).astype(BF16)
    out[0, 1] = jnp.where(low, zero, rolled).astype(BF16)
    out[1, 0] = jnp.where(low, rolled, zero).astype(BF16)
    out[1, 1] = jnp.where(low, zero, slab).astype(BF16)


def _store_values_t(out, slab):
    st = slab.T
    out[0] = st[0:NSA_DH, :].astype(BF16)
    out[1] = st[NSA_DH:2 * NSA_DH, :].astype(BF16)


def _nsa_prep_kernel(x_ref, cos_ref, sin_ref, cost_ref, sint_ref, gq_ref, gks_ref, gkw_ref, gsum_ref,
                     q_out, ks_out, vs_out, kw_out, vw_out, kc_out, vc_out):
    cos, sin, gsum = cos_ref[...], sin_ref[...], gsum_ref[...]
    for c in range(NSA_W // LANES):
        xt = x_ref[:, c * LANES:(c + 1) * LANES].T
        qn = _head_norm_rope_t(xt, gsum, gq_ref[...], cost_ref[...], sint_ref[...]) * (NSA_DH ** -0.5)
        q_out[c * LANES:(c + 1) * LANES, :] = qn.astype(BF16)
    ks = _head_norm_rope(x_ref[:, NSA_OFF_KS:NSA_OFF_KS + LANES], gsum, gks_ref[...], cos, sin)
    kw = _head_norm_rope(x_ref[:, NSA_OFF_KW:NSA_OFF_KW + LANES], gsum, gkw_ref[...], cos, sin)
    _store_keys(ks_out, ks)
    _store_keys(kw_out, kw)
    _store_values_t(vs_out, x_ref[:, NSA_OFF_VS:NSA_OFF_VS + LANES])
    _store_values_t(vw_out, x_ref[:, NSA_OFF_VW:NSA_OFF_VW + LANES])
    kc_out[...] = x_ref[:, NSA_OFF_KC:NSA_OFF_KC + LANES]
    vc_out[...] = x_ref[:, NSA_OFF_VC:NSA_OFF_VC + LANES]


def _nsa_cmp_kernel(xk_ref, xv_ref, pek_ref, pev_ref, w1ka_ref, w1kb_ref, w1va_ref, w1vb_ref,
                    b1k_ref, b1v_ref, w2k_ref, w2v_ref, gk_ref, cos_ref, sin_ref, gsum_ref,
                    kc_out, vc_out):
    rows = xk_ref.shape[0]

    def hidden(x, pe, wa, wb, b1):
        first = _dot((x + pe[0:1, :]).astype(BF16), wa[...])
        second = _dot((x + pe[1:2, :]).astype(BF16), wb[...])
        hid = first + pltpu.roll(second, rows - 1, axis=0) + b1[...]
        return (hid * _sigmoid(hid)).astype(BF16)

    kc = _dot(hidden(xk_ref[...], pek_ref, w1ka_ref, w1kb_ref, b1k_ref), w2k_ref[...])
    kc = _head_norm_rope(kc, gsum_ref[...], gk_ref[...], cos_ref[...], sin_ref[...])
    _store_keys(kc_out, kc)
    vc_t = _dot_nt(w2v_ref[...], hidden(xv_ref[...], pev_ref, w1va_ref, w1vb_ref, b1v_ref))
    vc_out[0] = vc_t[0:NSA_DH, :].astype(BF16)
    vc_out[1] = vc_t[NSA_DH:2 * NSA_DH, :].astype(BF16)


def _online_update(s, v_t, m_ref, l_ref, acc_ref, g):
    m_old = m_ref[g]
    m_new = jnp.maximum(m_old, jnp.max(s, axis=0, keepdims=True))
    alpha = jnp.exp(m_old - m_new)
    e = jnp.exp(s - jnp.maximum(m_new, 0.1 * NEG))
    l_ref[g] = alpha * l_ref[g] + jnp.sum(e, axis=0, keepdims=True)
    acc_ref[g] = alpha * acc_ref[g] + _dot(v_t, e.astype(BF16))
    m_ref[g] = m_new


def _nsa_attn_kernel(q_ref, kc_ref, vc_ref, ks_ref, vs_ref, kw_ref, vw_ref, gate_ref, mt_ref, esel_ref, eg_ref,
                     out_ref, ocmp_s, ms_s, ls_s, accs_s, mw_s, lw_s, accw_s, *, n_cmp, n_sel):
    TQ = NSA_TQ
    i = pl.program_id(2)
    t0 = i * TQ
    t_row = t0 + lax.broadcasted_iota(jnp.int32, (1, TQ), 1)

    def q_slab(g):
        return q_ref[(g // 2) * LANES:(g // 2 + 1) * LANES, :]

    rows = kc_ref.shape[1]
    n_i = lax.broadcasted_iota(jnp.int32, (rows, 1), 0)
    cmask = (n_i < n_cmp) & (n_i * CMP_STRIDE + (CMP_BLOCK - 1) <= t_row)
    cmf = cmask.astype(F32)
    psum = jnp.zeros((rows, TQ), F32)
    for g in range(NSA_GROUP):
        s = jnp.where(cmask, _dot(kc_ref[g % 2], q_slab(g)), NEG)
        e = jnp.exp(s - jnp.max(s, axis=0, keepdims=True)) * cmf
        p = e / jnp.maximum(jnp.sum(e, axis=0, keepdims=True), 1e-30)
        psum = psum + p
        ocmp_s[g] = _dot(vc_ref[...], p.astype(BF16))
    p1, p2, p3 = _split3(psum)
    imp = _dot(mt_ref[...], p1) + _dot(mt_ref[...], p2) + _dot(mt_ref[...], p3)
    j_i = lax.broadcasted_iota(jnp.int32, (n_sel, TQ), 0)
    cur = (t0 + lax.broadcasted_iota(jnp.int32, (n_sel, TQ), 1)) // SEL_BLOCK
    forced = (j_i == 0) | (j_i == cur) | (j_i == cur - 1)
    score = jnp.where(j_i > cur, NEG, imp[0:n_sel, :] + SEL_FORCE_BONUS * forced.astype(F32))
    rank = jnp.zeros((n_sel, TQ), jnp.int32)
    for jp in range(n_sel):
        row = score[jp:jp + 1, :]
        beats = (row > score) | ((row == score) & (j_i > jp))
        rank = rank + beats.astype(jnp.int32)
    sel_t = (rank < min(SEL_TOPN, n_sel)).astype(F32)
    sel_b = jnp.concatenate([sel_t, jnp.zeros((LANES - n_sel, TQ), F32)], axis=0).astype(BF16)

    for m_ref, l_ref, acc_ref in ((ms_s, ls_s, accs_s), (mw_s, lw_s, accw_s)):
        m_ref[...] = jnp.full(m_ref.shape, NEG, F32)
        l_ref[...] = jnp.zeros(l_ref.shape, F32)
        acc_ref[...] = jnp.zeros(acc_ref.shape, F32)

    def key_tile(jt, carry):
        k0 = pl.multiple_of(jt * TQ, TQ)
        diff = t_row - (k0 + lax.broadcasted_iota(jnp.int32, (TQ, 1), 0))
        causal = diff >= 0
        smask = (_dot(esel_ref[jt], sel_b) > 0.5) & causal
        for g in range(NSA_GROUP):
            s = jnp.where(smask, _dot(ks_ref[g % 2, pl.ds(k0, TQ), :], q_slab(g)), NEG)
            _online_update(s, vs_ref[jt], ms_s, ls_s, accs_s, g)

        @pl.when(jt * TQ + TQ > t0 - WINDOW + 1)
        def _():
            wmask = causal & (diff < WINDOW)
            for g in range(NSA_GROUP):
                s = jnp.where(wmask, _dot(kw_ref[g % 2, pl.ds(k0, TQ), :], q_slab(g)), NEG)
                _online_update(s, vw_ref[jt], mw_s, lw_s, accw_s, g)
        return carry

    lax.fori_loop(0, i + 1, key_tile, 0)

    g1, g2, g3 = _split3(_sigmoid(gate_ref[...]))
    gexp = _dot(g1, eg_ref[...]) + _dot(g2, eg_ref[...]) + _dot(g3, eg_ref[...])
    half_w = NSA_GROUP * NSA_DH

    def token_major(even, odd):
        return jnp.concatenate([even, odd], axis=0).T

    for c in range(NSA_GROUP // 2):
        ge, go = 2 * c, 2 * c + 1
        o_cmp = token_major(ocmp_s[ge], ocmp_s[go])
        o_sel = token_major(accs_s[ge] / jnp.maximum(ls_s[ge], 1e-30), accs_s[go] / jnp.maximum(ls_s[go], 1e-30))
        o_win = token_major(accw_s[ge] / jnp.maximum(lw_s[ge], 1e-30), accw_s[go] / jnp.maximum(lw_s[go], 1e-30))
        cs = slice(c * LANES, (c + 1) * LANES)
        o = (gexp[:, c * LANES:(c + 1) * LANES] * o_cmp
             + gexp[:, half_w + c * LANES:half_w + (c + 1) * LANES] * o_sel
             + gexp[:, 2 * half_w + c * LANES:2 * half_w + (c + 1) * LANES] * o_win)
        out_ref[:, cs] = o.astype(BF16)


def _nsa_constants(seq):
    n_cmp = (seq - CMP_BLOCK) // CMP_STRIDE + 1
    n_sel = seq // SEL_BLOCK
    rows = seq // CMP_STRIDE
    n = np.arange(rows)[None, :]
    j = np.arange(LANES)[:, None]
    mt = ((n * CMP_STRIDE < j * SEL_BLOCK + SEL_BLOCK) & (n * CMP_STRIDE + CMP_BLOCK > j * SEL_BLOCK)
          & (n < n_cmp) & (j < n_sel))
    s = np.arange(seq)
    esel = (s[:, None] // SEL_BLOCK == np.arange(LANES)[None, :])
    esel = esel.reshape(seq // NSA_TQ, NSA_TQ, LANES)
    eg = np.zeros((NSA_KVH, LANES, N_BRANCH * NSA_GROUP * NSA_DH), np.float32)
    for h in range(NSA_KVH):
        for g in range(NSA_GROUP):
            for br in range(N_BRANCH):
                lane = (h * NSA_GROUP + g) * N_BRANCH + br
                c0 = br * NSA_GROUP * NSA_DH + g * NSA_DH
                eg[h, lane, c0:c0 + NSA_DH] = 1.0
    gsum = np.kron(np.eye(LANES // NSA_DH), np.ones((NSA_DH, NSA_DH)))
    return (n_cmp, n_sel, jnp.asarray(mt, BF16), jnp.asarray(esel, BF16), jnp.asarray(eg, BF16),
            jnp.asarray(gsum, BF16))


def _nsa(nsa, w, bsz, seq):
    assert seq % NSA_TQ == 0 and seq % NSA_TP == 0
    n_cmp, n_sel, mt, esel, eg, gsum = _nsa_constants(seq)
    rows = seq // CMP_STRIDE
    cos, sin = _rope_tables(jnp.arange(seq))
    cos_c, sin_c = _rope_tables(jnp.arange(rows) * CMP_STRIDE + (CMP_BLOCK - 1))
    c2 = lambda shape: pl.BlockSpec(shape, lambda b, i: (0,) * len(shape))
    n_tiles = seq // NSA_TP
    k_shape = jax.ShapeDtypeStruct((bsz, NSA_KVH, 2, seq, LANES), BF16)
    k_spec = pl.BlockSpec((None, NSA_KVH, 2, NSA_TP, LANES), lambda b, i: (b, 0, 0, i, 0))
    v_shape = jax.ShapeDtypeStruct((bsz, NSA_KVH, n_tiles, NSA_DH, NSA_TP), BF16)
    v_spec = pl.BlockSpec((None, NSA_KVH, None, NSA_DH, NSA_TP), lambda b, i: (b, 0, i, 0, 0))
    raw_shape = jax.ShapeDtypeStruct((bsz, seq, LANES), F32)
    raw_spec = pl.BlockSpec((None, NSA_TP, LANES), lambda b, i: (b, i, 0))
    qn, ksd, vsd, kwd, vwd, kc_raw, vc_raw = pl.pallas_call(
        _nsa_prep_kernel,
        out_shape=(jax.ShapeDtypeStruct((bsz, NSA_W, seq), BF16), k_shape, v_shape, k_shape, v_shape,
                   raw_shape, raw_shape),
        grid=(bsz, n_tiles),
        in_specs=[pl.BlockSpec((None, NSA_TP, NSA_COLS), lambda b, i: (b, i, 0)),
                  pl.BlockSpec((NSA_TP, LANES), lambda b, i: (i, 0)),
                  pl.BlockSpec((NSA_TP, LANES), lambda b, i: (i, 0)),
                  pl.BlockSpec((LANES, NSA_TP), lambda b, i: (0, i)),
                  pl.BlockSpec((LANES, NSA_TP), lambda b, i: (0, i)),
                  c2((LANES, 1)), c2((1, LANES)), c2((1, LANES)), c2((LANES, LANES))],
        out_specs=(pl.BlockSpec((None, NSA_W, NSA_TP), lambda b, i: (b, 0, i)),
                   k_spec, v_spec, k_spec, v_spec, raw_spec, raw_spec),
        compiler_params=_params(("parallel", "parallel")),
        name="nsa_prep",
    )(nsa, cos, sin, cos.T, sin.T, w["nsa_gq"].reshape(LANES, 1), w["nsa_gks"], w["nsa_gkw"], gsum)

    wide = CMP_STRIDE * LANES
    xk = kc_raw.reshape(bsz, rows, wide)
    xv = vc_raw.reshape(bsz, rows, wide)
    c1 = lambda shape: pl.BlockSpec(shape, lambda b: (0,) * len(shape))
    kc_shape = jax.ShapeDtypeStruct((bsz, NSA_KVH, 2, rows, LANES), BF16)
    kc_spec = pl.BlockSpec((None, NSA_KVH, 2, rows, LANES), lambda b: (b, 0, 0, 0, 0))
    vc_shape = jax.ShapeDtypeStruct((bsz, NSA_KVH, NSA_DH, rows), BF16)
    vc_spec = pl.BlockSpec((None, NSA_KVH, NSA_DH, rows), lambda b: (b, 0, 0, 0))
    hid2 = NSA_KVH * CMP_HIDDEN
    kcd, vcd = pl.pallas_call(
        _nsa_cmp_kernel,
        out_shape=(kc_shape, vc_shape),
        grid=(bsz,),
        in_specs=[pl.BlockSpec((None, rows, wide), lambda b: (b, 0, 0)),
                  pl.BlockSpec((None, rows, wide), lambda b: (b, 0, 0)),
                  c1((2, wide)), c1((2, wide)),
                  c1((wide, hid2)), c1((wide, hid2)), c1((wide, hid2)), c1((wide, hid2)),
                  c1((1, hid2)), c1((1, hid2)), c1((hid2, LANES)), c1((LANES, hid2)),
                  c1((1, LANES)), c1((rows, LANES)), c1((rows, LANES)), c1((LANES, LANES))],
        out_specs=(kc_spec, vc_spec),
        compiler_params=_params(("parallel",)),
        name="nsa_compress",
    )(xk, xv, w["cmp_pe_k"], w["cmp_pe_v"], w["cmp_w1k_a"], w["cmp_w1k_b"], w["cmp_w1v_a"], w["cmp_w1v_b"],
      w["cmp_b1_k"], w["cmp_b1_v"], w["cmp_w2_k"], w["cmp_w2_v"].T, w["nsa_gkc"], cos_c, sin_c, gsum)

    TQ = NSA_TQ
    half_w = NSA_GROUP * NSA_DH
    c3 = lambda shape: pl.BlockSpec(shape, lambda b, h, i: (0,) * len(shape))
    assert NSA_TP == TQ
    keys = lambda r: pl.BlockSpec((None, None, 2, r, LANES), lambda b, h, i: (b, h, 0, 0, 0))
    vals = pl.BlockSpec((None, None, seq // TQ, NSA_DH, TQ), lambda b, h, i: (b, h, 0, 0, 0))
    stat = lambda: pltpu.VMEM((NSA_GROUP, 1, TQ), F32)
    accu = lambda: pltpu.VMEM((NSA_GROUP, NSA_DH, TQ), F32)
    return pl.pallas_call(
        functools.partial(_nsa_attn_kernel, n_cmp=n_cmp, n_sel=n_sel),
        out_shape=jax.ShapeDtypeStruct((bsz, seq, NSA_W), BF16),
        grid=(bsz, NSA_KVH, seq // TQ),
        in_specs=[pl.BlockSpec((None, half_w, TQ), lambda b, h, i: (b, h, i)),
                  keys(rows), pl.BlockSpec((None, None, NSA_DH, rows), lambda b, h, i: (b, h, 0, 0)),
                  keys(seq), vals, keys(seq), vals,
                  pl.BlockSpec((None, TQ, LANES), lambda b, h, i: (b, i, NSA_OFF_GATE // LANES)),
                  c3((LANES, rows)), c3((seq // TQ, TQ, LANES)),
                  pl.BlockSpec((None, LANES, N_BRANCH * half_w), lambda b, h, i: (h, 0, 0))],
        out_specs=pl.BlockSpec((None, TQ, half_w), lambda b, h, i: (b, i, h)),
        scratch_shapes=[accu(), stat(), stat(), accu(), stat(), stat(), accu()],
        compiler_params=_params(("parallel", "parallel", "arbitrary")),
        name="nsa_attention",
    )(qn, kcd, vcd, ksd, vsd, kwd, vwd, nsa, mt, esel, eg)


POST_TM = 256


def _rms(v, gain):
    return v * lax.rsqrt(jnp.mean(v * v, axis=-1, keepdims=True) + EPS) * gain


def _memkv_kernel(mem_ref, g_ref, w_ref, gk_ref, k_out, v_out):
    u = _rms(mem_ref[...], g_ref[...]).astype(BF16)
    kv = _dot(u, w_ref[...])
    for h in range(MEM_HEADS):
        cs = slice(h * MEM_DH, (h + 1) * MEM_DH)
        k_out[:, cs] = _rms(kv[:, cs], gk_ref[...]).astype(BF16)
    v_out[...] = kv[:, MEM_W:2 * MEM_W].astype(BF16)


def _memkv(mem, g_mem, w_mem_kv, g_k):
    bsz, m_len, _ = mem.shape
    c1 = lambda shape: pl.BlockSpec(shape, lambda b: (0,) * len(shape))
    shape = jax.ShapeDtypeStruct((bsz, m_len, MEM_W), BF16)
    spec = pl.BlockSpec((None, m_len, MEM_W), lambda b: (b, 0, 0))
    return pl.pallas_call(
        _memkv_kernel, out_shape=(shape, shape), grid=(bsz,),
        in_specs=[pl.BlockSpec((None, m_len, D_MODEL), lambda b: (b, 0, 0)),
                  c1((1, D_MODEL)), c1((D_MODEL, 2 * MEM_W)), c1((1, MEM_DH))],
        out_specs=(spec, spec),
        compiler_params=_params(("parallel",)),
        name="mem_kv",
    )(mem, g_mem, w_mem_kv, g_k)


def _post_kernel(x_ref, hml_ref, hnsa_ref, mk_ref, mv_ref, gmix_ref, wmq_ref, gmq_ref, wgate_ref,
                 wml_ref, wnsa_ref, wmem_ref, wout_ref, gffn_ref, wrh_ref, wrl_ref, br_ref,
                 x1_out, t_out, logit_out):
    x = x_ref[...]
    ub = _rms(x, gmix_ref[...]).astype(BF16)
    mq = _dot(ub, wmq_ref[...])
    heads = []
    for h in range(MEM_HEADS):
        cs = slice(h * MEM_DH, (h + 1) * MEM_DH)
        qn = (_rms(mq[:, cs], gmq_ref[...]) * (MEM_DH ** -0.5)).astype(BF16)
        s = _dot_nt(qn, mk_ref[:, cs])
        e = jnp.exp(s - jnp.max(s, axis=1, keepdims=True))
        p = e / jnp.sum(e, axis=1, keepdims=True)
        heads.append(_dot(p.astype(BF16), mv_ref[:, cs]))
    hmem = jnp.concatenate(heads, axis=1).astype(BF16)
    gates = _sigmoid(_dot(ub, wgate_ref[...]))
    merged = (gates[:, 0:D_MODEL] * _dot(hml_ref[...], wml_ref[...])
              + gates[:, D_MODEL:2 * D_MODEL] * _dot(hnsa_ref[...], wnsa_ref[...])
              + gates[:, 2 * D_MODEL:3 * D_MODEL] * _dot(hmem, wmem_ref[...]))
    x1 = x + _dot(merged.astype(BF16), wout_ref[...])
    x1_out[...] = x1
    t = _rms(x1, gffn_ref[...])
    t_out[...] = t
    t1, t2, _ = _split3(t)
    logit_out[...] = _dot(t1, wrh_ref[...]) + _dot(t1, wrl_ref[...]) + _dot(t2, wrh_ref[...]) + br_ref[...]


def _post(x2, hml, hnsa, mk, mv, w, seq):
    n = x2.shape[0]
    tm = POST_TM
    assert seq % tm == 0
    per = seq // tm
    m_len = mk.shape[1]
    row = lambda width: pl.BlockSpec((tm, width), lambda i: (i, 0))
    c1 = lambda shape: pl.BlockSpec(shape, lambda i: (0,) * len(shape))
    memspec = pl.BlockSpec((None, m_len, MEM_W), lambda i: (i // per, 0, 0))
    return pl.pallas_call(
        _post_kernel,
        out_shape=(jax.ShapeDtypeStruct((n, D_MODEL), F32), jax.ShapeDtypeStruct((n, D_MODEL), F32),
                   jax.ShapeDtypeStruct((n, LANES), F32)),
        grid=(n // tm,),
        in_specs=[row(D_MODEL), row(ML_W), row(NSA_W), memspec, memspec,
                  c1((1, D_MODEL)), c1((D_MODEL, MEM_W)), c1((1, MEM_DH)), c1((D_MODEL, N_BRANCH * D_MODEL)),
                  c1((ML_W, D_MODEL)), c1((NSA_W, D_MODEL)), c1((MEM_W, D_MODEL)), c1((D_MODEL, D_MODEL)),
                  c1((1, D_MODEL)), c1((D_MODEL, LANES)), c1((D_MODEL, LANES)), c1((1, LANES))],
        out_specs=(row(D_MODEL), row(D_MODEL), row(LANES)),
        compiler_params=_params(("parallel",)),
        name="post",
    )(x2, hml, hnsa, mk, mv, w["g_mix"], w["w_mq"], w["mem_gq"], w["w_gate"], w["w_br_ml"], w["w_br_nsa"],
      w["w_br_mem"], w["w_out"], w["g_ffn"], w["w_router_hi"], w["w_router_lo"], w["b_router"])


ROUTE_T = 512
MOE_BLK = 256
MOE_TD = 256


def _route_kernel(logit_ref, triu_ref, e_out, w_out, p_out, wtm_out, cnt_out, carry_ref):
    T = ROUTE_T

    @pl.when(pl.program_id(0) == 0)
    def _():
        carry_ref[...] = jnp.zeros(carry_ref.shape, F32)

    sc = logit_ref[...].T[0:N_EXPERTS, :]
    j_i = lax.broadcasted_iota(jnp.int32, (N_EXPERTS, T), 0)
    rank = jnp.zeros((N_EXPERTS, T), jnp.int32)
    for jp in range(N_EXPERTS):
        row = sc[jp:jp + 1, :]
        beats = (row > sc) | ((row == sc) & (j_i > jp))
        rank = rank + beats.astype(jnp.int32)
    sel = rank < TOP_K
    e = jnp.where(sel, jnp.exp(sc - jnp.max(sc, axis=0, keepdims=True)), 0.0)
    wgt = e / jnp.sum(e, axis=0, keepdims=True)
    self = sel.astype(F32)
    pos = _dot(self.astype(BF16), triu_ref[...]) + carry_ref[:, 0:1]
    carry_ref[...] = carry_ref[...] + jnp.sum(self, axis=1, keepdims=True)
    e_rows, w_rows, p_rows = [], [], []
    for r in range(TOP_K):
        hit = rank == r
        e_rows.append(jnp.sum(jnp.where(hit, j_i, 0), axis=0, keepdims=True))
        w_rows.append(jnp.sum(jnp.where(hit, wgt, 0.0), axis=0, keepdims=True))
        p_rows.append(jnp.sum(jnp.where(hit, pos, 0.0), axis=0, keepdims=True))
    e_out[...] = jnp.concatenate(e_rows, axis=0)
    w4 = jnp.concatenate(w_rows, axis=0)
    w_out[...] = w4
    p_out[...] = jnp.concatenate(p_rows, axis=0).astype(jnp.int32)
    wtm_out[...] = jnp.concatenate([w4, jnp.zeros((LANES - TOP_K, T), F32)], axis=0).T
    cnt_out[...] = carry_ref[...]


def _route(logits):
    n = logits.shape[0]
    T = ROUTE_T
    assert n % T == 0
    s = np.arange(T)
    triu = jnp.asarray(s[:, None] < s[None, :], BF16)
    row4 = lambda: pl.BlockSpec((TOP_K, T), lambda i: (0, i))
    return pl.pallas_call(
        _route_kernel,
        out_shape=(jax.ShapeDtypeStruct((TOP_K, n), jnp.int32), jax.ShapeDtypeStruct((TOP_K, n), F32),
                   jax.ShapeDtypeStruct((TOP_K, n), jnp.int32), jax.ShapeDtypeStruct((n, LANES), F32),
                   jax.ShapeDtypeStruct((N_EXPERTS, LANES), F32)),
        grid=(n // T,),
        in_specs=[pl.BlockSpec((T, LANES), lambda i: (i, 0)), pl.BlockSpec((T, T), lambda i: (0, 0))],
        out_specs=(row4(), row4(), row4(), pl.BlockSpec((T, LANES), lambda i: (i, 0)),
                   pl.BlockSpec((N_EXPERTS, LANES), lambda i: (0, 0))),
        scratch_shapes=[pltpu.VMEM((N_EXPERTS, LANES), F32)],
        compiler_params=_params(("arbitrary",)),
        name="moe_route",
    )(logits, triu)


def _dispatch_kernel(dest_ref, t_ref, xs_in, xs_out, sem):
    del xs_in

    def row_copy(j, d):
        return pltpu.make_async_copy(t_ref.at[pl.ds(j, 1)], xs_out.at[pl.ds(d, 1)], sem)

    def issue(j, carry):
        for r in range(TOP_K):
            row_copy(j, dest_ref[r, j]).start(priority=r % 2)
        return carry

    lax.fori_loop(0, MOE_TD, issue, 0)

    def drain(j, carry):
        for r in range(TOP_K):
            row_copy(j, dest_ref[r, j]).wait()
        return carry

    lax.fori_loop(0, MOE_TD, drain, 0)


def _dispatch(t, dest3, cap):
    n = t.shape[0]
    zeros = jnp.zeros((cap, D_MODEL), F32)
    return pl.pallas_call(
        _dispatch_kernel,
        out_shape=jax.ShapeDtypeStruct((cap, D_MODEL), F32),
        grid=(n // MOE_TD,),
        in_specs=[pl.BlockSpec((None, TOP_K, MOE_TD), lambda i: (i, 0, 0), memory_space=pltpu.SMEM),
                  pl.BlockSpec((MOE_TD, D_MODEL), lambda i: (i, 0)),
                  pl.BlockSpec(memory_space=pl.ANY)],
        out_specs=pl.BlockSpec(memory_space=pl.ANY),
        scratch_shapes=[pltpu.SemaphoreType.DMA],
        input_output_aliases={2: 0},
        compiler_params=_params(("arbitrary",)),
        name="moe_dispatch",
    )(dest3, t, zeros)


def _ffn_kernel(be_ref, nu_ref, xs_ref, wgu_ref, bgu_ref, wd_ref, bd_ref, ys_ref):
    del be_ref
    used = pl.program_id(0) < nu_ref[0]

    @pl.when(used)
    def _():
        hgu = _dot(xs_ref[...].astype(BF16), wgu_ref[...]) + bgu_ref[...]
        gate = jnp.minimum(hgu, SWIGLU_LIMIT)
        up1 = jnp.clip(hgu, -SWIGLU_LIMIT, SWIGLU_LIMIT) + 1.0
        act = gate * _sigmoid(SWIGLU_ALPHA * gate) * pltpu.roll(up1, 2 * D_FF - 1, axis=1)
        ys_ref[...] = _dot(act.astype(BF16), wd_ref[...]) + bd_ref[...]

    @pl.when(jnp.logical_not(used))
    def _():
        ys_ref[...] = jnp.zeros(ys_ref.shape, F32)


def _ffn(xs, blk_e, n_used, w):
    cap = xs.shape[0]
    espec = lambda r, c: pl.BlockSpec((None, r, c), lambda i, be, nu: (be[i], 0, 0))
    rows = pl.BlockSpec((MOE_BLK, D_MODEL), lambda i, be, nu: (i, 0))
    return pl.pallas_call(
        _ffn_kernel,
        out_shape=jax.ShapeDtypeStruct((cap, D_MODEL), F32),
        grid_spec=pltpu.PrefetchScalarGridSpec(
            num_scalar_prefetch=2, grid=(cap // MOE_BLK,),
            in_specs=[rows, espec(D_MODEL, 2 * D_FF), espec(1, 2 * D_FF), espec(2 * D_FF, D_MODEL),
                      espec(1, D_MODEL)],
            out_specs=rows),
        compiler_params=_params(("arbitrary",)),
        name="moe_ffn",
    )(blk_e, n_used, xs, w["w_gu"], w["b_gu"], w["w_down_x"], w["b_down"])


def _slots_kernel(ps_ref, e_ref, p_ref, d_out):
    e = e_ref[...]
    d = p_ref[...]
    for k in range(N_EXPERTS):
        d = d + jnp.where(e == k, ps_ref[k], 0)
    d_out[...] = d


def _slots(pad_start, e_idx, pos):
    n = e_idx.shape[1]
    blk = pl.BlockSpec((TOP_K, MOE_TD), lambda i, ps: (0, i))
    return pl.pallas_call(
        _slots_kernel,
        out_shape=jax.ShapeDtypeStruct((n // MOE_TD, TOP_K, MOE_TD), jnp.int32),
        grid_spec=pltpu.PrefetchScalarGridSpec(
            num_scalar_prefetch=1, grid=(n // MOE_TD,), in_specs=[blk, blk],
            out_specs=pl.BlockSpec((None, TOP_K, MOE_TD), lambda i, ps: (i, 0, 0))),
        compiler_params=_params(("parallel",)),
        name="moe_slots",
    )(pad_start, e_idx, pos)


def _combine_kernel(dest_ref, x1_ref, wtm_ref, ys_ref, out_ref, buf_ref, sem):
    def row_copy(r, j, d):
        return pltpu.make_async_copy(ys_ref.at[pl.ds(d, 1)], buf_ref.at[r, pl.ds(j, 1)], sem)

    def issue(j, carry):
        for r in range(TOP_K):
            row_copy(r, j, dest_ref[r, j]).start(priority=r % 2)
        return carry

    lax.fori_loop(0, MOE_TD, issue, 0)

    def drain(j, carry):
        for r in range(TOP_K):
            row_copy(r, j, dest_ref[r, j]).wait()
        return carry

    lax.fori_loop(0, MOE_TD, drain, 0)
    acc = x1_ref[...]
    wtm = wtm_ref[...]
    for r in range(TOP_K):
        acc = acc + wtm[:, r:r + 1] * buf_ref[r]
    out_ref[...] = acc


def _combine(x1, wtm, dest3, ys):
    n = x1.shape[0]
    row = lambda width: pl.BlockSpec((MOE_TD, width), lambda i: (i, 0))
    return pl.pallas_call(
        _combine_kernel,
        out_shape=jax.ShapeDtypeStruct((n, D_MODEL), F32),
        grid=(n // MOE_TD,),
        in_specs=[pl.BlockSpec((None, TOP_K, MOE_TD), lambda i: (i, 0, 0), memory_space=pltpu.SMEM),
                  row(D_MODEL), row(LANES), pl.BlockSpec(memory_space=pl.ANY)],
        out_specs=row(D_MODEL),
        scratch_shapes=[pltpu.VMEM((TOP_K, MOE_TD, D_MODEL), F32), pltpu.SemaphoreType.DMA],
        compiler_params=_params(("arbitrary",)),
        name="moe_combine",
    )(dest3, x1, wtm, ys)


def _moe(x1, t, logits, w):
    n = x1.shape[0]
    e_idx, _, pos, wtm, cnt = _route(logits)
    counts = cnt[:, 0].astype(jnp.int32)
    padded = (counts + MOE_BLK - 1) // MOE_BLK * MOE_BLK
    pad_end = jnp.cumsum(padded)
    pad_start = pad_end - padded
    cap = -(-(n * TOP_K + N_EXPERTS * (MOE_BLK - 1)) // MOE_BLK) * MOE_BLK
    dest3 = _slots(pad_start, e_idx, pos)
    starts = jnp.arange(cap // MOE_BLK, dtype=jnp.int32) * MOE_BLK
    blk_e = jnp.minimum(jnp.sum((pad_end[None, :] <= starts[:, None]).astype(jnp.int32), axis=1), N_EXPERTS - 1)
    n_used = (pad_end[-1:] // MOE_BLK).astype(jnp.int32)
    xs = _dispatch(t, dest3, cap)
    ys = _ffn(xs, blk_e, n_used, w)
    return _combine(x1, wtm, dest3, ys)


def _cmp_weights(pos, w1, b1, w2):
    eye = jnp.eye(NSA_KVH, dtype=F32)
    halves = []
    for part in (w1[:CMP_STRIDE], w1[CMP_STRIDE:]):
        big = jnp.einsum("ldf,hg->lhdgf", part, eye)
        halves.append(big.reshape(CMP_STRIDE * LANES, NSA_KVH * CMP_HIDDEN).astype(BF16))
    pe = jnp.stack([jnp.broadcast_to(part[:, None, :], (CMP_STRIDE, NSA_KVH, NSA_DH)).reshape(-1)
                    for part in (pos[:CMP_STRIDE], pos[CMP_STRIDE:])])
    w2big = jnp.einsum("fd,hg->hfgd", w2, eye).reshape(NSA_KVH * CMP_HIDDEN, LANES).astype(BF16)
    return halves[0], halves[1], pe, jnp.tile(b1, NSA_KVH).reshape(1, -1), w2big


def _prep_weights(p):
    w = {}
    offs = np.concatenate([[0], np.cumsum(IN_SPLITS)])
    col = lambda k: p["w_in"][:, offs[k]:offs[k + 1]]
    zpad = lambda n: jnp.zeros((D_MODEL, n), F32)
    w["w_ml"] = jnp.concatenate([col(0), col(1), col(2), col(3), col(4), col(5),
                                 zpad(LANES - 2 * ML_HEADS)], axis=1).astype(BF16)
    w["w_nsa"] = jnp.concatenate([col(k) for k in range(6, 14)] + [zpad(LANES - NSA_QH * 3)], axis=1).astype(BF16)
    w["g_mix"] = p["g_mix"].reshape(1, D_MODEL)
    w["conv_w"] = p["ml_conv_w"]
    w["conv_b"] = p["ml_conv_b"].reshape(1, 2 * ML_W)
    w["ml_gate_b"] = jnp.concatenate([p["ml_b_i"], p["ml_b_f"], jnp.zeros((LANES - 2 * ML_HEADS,), F32)]).reshape(1, LANES)
    w["ml_g_out"] = p["ml_g_out"]

    pair = lambda g: jnp.tile(g, LANES // NSA_DH).reshape(1, LANES)
    w["nsa_gq"] = pair(p["nsa_g_q"])
    w["nsa_gkc"], w["nsa_gks"], w["nsa_gkw"] = (pair(p["nsa_g_k"][k]) for k in range(3))
    (w["cmp_w1k_a"], w["cmp_w1k_b"], w["cmp_pe_k"], w["cmp_b1_k"], w["cmp_w2_k"]) = _cmp_weights(
        p["cmp_pos_k"], p["cmp_w1_k"], p["cmp_b1_k"], p["cmp_w2_k"])
    (w["cmp_w1v_a"], w["cmp_w1v_b"], w["cmp_pe_v"], w["cmp_b1_v"], w["cmp_w2_v"]) = _cmp_weights(
        p["cmp_pos_v"], p["cmp_w1_v"], p["cmp_b1_v"], p["cmp_w2_v"])

    w["g_mem"] = p["g_mem"].reshape(1, D_MODEL)
    w["w_mem_kv"] = p["w_mem_kv"].astype(BF16)
    w["mem_gq"] = p["mem_g_q"].reshape(1, MEM_DH)
    w["mem_gk"] = p["mem_g_k"].reshape(1, MEM_DH)
    w["w_mq"] = col(14).astype(BF16)
    w["w_gate"] = col(15).astype(BF16)
    for k in ("w_br_ml", "w_br_nsa", "w_br_mem", "w_out"):
        w[k] = p[k].astype(BF16)
    w["g_ffn"] = p["g_ffn"].reshape(1, D_MODEL)
    w_r = jnp.concatenate([p["w_router"], jnp.zeros((D_MODEL, LANES - N_EXPERTS), F32)], axis=1)
    w["w_router_hi"] = w_r.astype(BF16)
    w["w_router_lo"] = (w_r - w["w_router_hi"].astype(F32)).astype(BF16)
    w["b_router"] = jnp.concatenate([p["b_router"], jnp.zeros((LANES - N_EXPERTS,), F32)]).reshape(1, LANES)
    w["w_gu"] = p["w_gu"].astype(BF16)
    w["b_gu"] = p["b_gu"][:, None, :]
    wd = p["w_down"].astype(BF16)
    w["w_down_x"] = jnp.stack([wd, jnp.zeros_like(wd)], axis=2).reshape(N_EXPERTS, 2 * D_FF, D_MODEL)
    w["b_down"] = p["b_down"][:, None, :]
    return w


_PARAM_NAMES = ("g_mix", "w_in", "ml_conv_w", "ml_conv_b", "ml_b_i", "ml_b_f", "ml_g_out", "nsa_g_q", "nsa_g_k",
                "cmp_pos_k", "cmp_w1_k", "cmp_b1_k", "cmp_w2_k", "cmp_pos_v", "cmp_w1_v", "cmp_b1_v", "cmp_w2_v",
                "g_mem", "w_mem_kv", "mem_g_q", "mem_g_k", "w_br_ml", "w_br_nsa", "w_br_mem", "w_out", "g_ffn",
                "w_router", "b_router", "w_gu", "b_gu", "w_down", "b_down")


def _layer(x, mem, p):
    bsz, seq, d = x.shape
    w = _prep_weights(p)
    x2 = x.reshape(bsz * seq, d)
    ml, nsa = _proj(x2, w["g_mix"], w["w_ml"], w["w_nsa"])
    h_ml = _mlstm(ml.reshape(bsz, seq, ML_COLS), w["conv_w"], w["conv_b"], w["ml_gate_b"], w["ml_g_out"], bsz, seq)
    h_nsa = _nsa(nsa.reshape(bsz, seq, NSA_COLS), w, bsz, seq)
    mk, mv = _memkv(mem, w["g_mem"], w["w_mem_kv"], w["mem_gk"])
    x1, t, logits = _post(x2, h_ml.reshape(bsz * seq, ML_W), h_nsa.reshape(bsz * seq, NSA_W), mk, mv, w, seq)
    return _moe(x1, t, logits, w).reshape(bsz, seq, d)


def kernel(x, mem, g_mix, w_in, ml_conv_w, ml_conv_b, ml_b_i, ml_b_f, ml_g_out, nsa_g_q, nsa_g_k, cmp_pos_k,
           cmp_w1_k, cmp_b1_k, cmp_w2_k, cmp_pos_v, cmp_w1_v, cmp_b1_v, cmp_w2_v, g_mem, w_mem_kv, mem_g_q,
           mem_g_k, w_br_ml, w_br_nsa, w_br_mem, w_out, g_ffn, w_router, b_router, w_gu, b_gu, w_down, b_down):
    stacked = (g_mix, w_in, ml_conv_w, ml_conv_b, ml_b_i, ml_b_f, ml_g_out, nsa_g_q, nsa_g_k, cmp_pos_k,
               cmp_w1_k, cmp_b1_k, cmp_w2_k, cmp_pos_v, cmp_w1_v, cmp_b1_v, cmp_w2_v, g_mem, w_mem_kv, mem_g_q,
               mem_g_k, w_br_ml, w_br_nsa, w_br_mem, w_out, g_ffn, w_router, b_router, w_gu, b_gu, w_down, b_down)
    h = x
    for layer in range(g_mix.shape[0]):
        h = _layer(h, mem, {name: v[layer] for name, v in zip(_PARAM_NAMES, stacked)})
    return h
```

```python
import functools

import jax
import jax.numpy as jnp
import numpy as np
from jax import lax
from jax.experimental import pallas as pl
from jax.experimental.pallas import tpu as pltpu

F32 = jnp.float32
BF16 = jnp.bfloat16

D_MODEL = 1024
ML_HEADS = 4
ML_DQK = 128
ML_W = ML_HEADS * ML_DQK
ML_CHUNK = 64
ML_CONV = 4
NSA_QH = 8
NSA_KVH = 2
NSA_DH = 64
NSA_GROUP = NSA_QH // NSA_KVH
NSA_W = NSA_QH * NSA_DH
NSA_KV_W = NSA_KVH * NSA_DH
CMP_BLOCK = 32
CMP_STRIDE = 16
CMP_HIDDEN = 256
SEL_BLOCK = 64
SEL_TOPN = 8
SEL_FORCE_BONUS = 1e4
WINDOW = 512
MEM_HEADS = 4
MEM_DH = 128
MEM_W = MEM_HEADS * MEM_DH
N_EXPERTS = 32
TOP_K = 4
D_FF = D_MODEL
SWIGLU_ALPHA = 1.702
SWIGLU_LIMIT = 7.0
N_BRANCH = 3
ROPE_THETA = 10000.0
EPS = 1e-6
NEG = -1e30

LANES = 128
VMEM_LIMIT = 56 * 1024 * 1024

IN_SPLITS = (ML_W, ML_W, ML_W, ML_W, ML_HEADS, ML_HEADS,
             NSA_W, NSA_KV_W, NSA_KV_W, NSA_KV_W, NSA_KV_W, NSA_KV_W, NSA_KV_W, NSA_QH * 3,
             MEM_W, N_BRANCH * D_MODEL)

ML_COLS = 4 * ML_W + LANES
NSA_COLS = NSA_W + 6 * NSA_KV_W + LANES


def _dot(a, b):
    return jnp.dot(a, b, preferred_element_type=F32)


def _dot_nt(a, b):
    return lax.dot_general(a, b, (((1,), (1,)), ((), ())), preferred_element_type=F32)


def _dot_tn(a, b):
    return lax.dot_general(a, b, (((0,), (0,)), ((), ())), preferred_element_type=F32)


def _split3(v):
    a = v.astype(BF16)
    r = v - a.astype(F32)
    b = r.astype(BF16)
    c = (r - b.astype(F32)).astype(BF16)
    return a, b, c


def _sigmoid(v):
    return 1.0 / (1.0 + jnp.exp(-v))


def _log_sigmoid(v):
    return jnp.minimum(v, 0.0) - jnp.log1p(jnp.exp(-jnp.abs(v)))


def _params(sem):
    return pltpu.CompilerParams(dimension_semantics=sem, vmem_limit_bytes=VMEM_LIMIT)


PROJ_TM = 512


def _proj_kernel(x_ref, g_ref, wml_ref, wnsa_ref, oml_ref, onsa_ref):
    x = x_ref[...]
    u = x * lax.rsqrt(jnp.mean(x * x, axis=-1, keepdims=True) + EPS) * g_ref[...]
    ub = u.astype(BF16)
    oml_ref[...] = _dot(ub, wml_ref[...])
    onsa_ref[...] = _dot(ub, wnsa_ref[...])


def _proj(x2, g_mix, w_ml, w_nsa):
    n = x2.shape[0]
    tm = min(PROJ_TM, n)
    return pl.pallas_call(
        _proj_kernel,
        out_shape=(jax.ShapeDtypeStruct((n, ML_COLS), F32), jax.ShapeDtypeStruct((n, NSA_COLS), F32)),
        grid=(n // tm,),
        in_specs=[pl.BlockSpec((tm, D_MODEL), lambda i: (i, 0)),
                  pl.BlockSpec((1, D_MODEL), lambda i: (0, 0)),
                  pl.BlockSpec((D_MODEL, ML_COLS), lambda i: (0, 0)),
                  pl.BlockSpec((D_MODEL, NSA_COLS), lambda i: (0, 0))],
        out_specs=(pl.BlockSpec((tm, ML_COLS), lambda i: (i, 0)),
                   pl.BlockSpec((tm, NSA_COLS), lambda i: (i, 0))),
        compiler_params=_params(("parallel",)),
        name="in_proj",
    )(x2, g_mix, w_ml, w_nsa)


ML_T = 256
CONV_PAD = 8


def _mlstm_kernel(q_ref, k_ref, v_ref, o_ref, g_ref, cw_ref, cb_ref, gb_ref, gout_ref, tri_ref, blk_ref,
                  out_ref, buf_ref, c_ref, n_ref, m_ref):
    T, L = ML_T, ML_CHUNK

    @pl.when(pl.program_id(1) == 0)
    def _():
        buf_ref[0:CONV_PAD, :] = jnp.zeros((CONV_PAD, 2 * ML_W), F32)
        c_ref[...] = jnp.zeros(c_ref.shape, F32)
        n_ref[...] = jnp.zeros(n_ref.shape, F32)
        m_ref[...] = jnp.zeros(m_ref.shape, F32)

    buf_ref[CONV_PAD:CONV_PAD + T, 0:ML_W] = q_ref[...]
    buf_ref[CONV_PAD:CONV_PAD + T, ML_W:2 * ML_W] = k_ref[...]
    y = cb_ref[...]
    for j in range(ML_CONV):
        lo = CONV_PAD - (ML_CONV - 1) + j
        y = y + cw_ref[j:j + 1, :] * buf_ref[lo:lo + T, :]
    buf_ref[0:CONV_PAD, :] = buf_ref[T:T + CONV_PAD, :]
    qk = y * _sigmoid(y)
    qf = qk[:, 0:ML_W] * (ML_DQK ** -0.5)
    kf = qk[:, ML_W:2 * ML_W]

    gg = g_ref[...] + gb_ref[...]
    gt = gg.T
    tri = tri_ref[...]
    blk = blk_ref[...]
    lf1, lf2, lf3 = _split3(_log_sigmoid(gg))
    b_col = _dot(tri, lf1) + _dot(tri, lf2) + _dot(tri, lf3)
    bl_col = _dot(blk, lf1) + _dot(blk, lf2) + _dot(blk, lf3)
    lt1, lt2, lt3 = _split3(_log_sigmoid(gt[0:8, :]))
    b_row = _dot_nt(lt1, tri) + _dot_nt(lt2, tri) + _dot_nt(lt3, tri)
    bl_row = _dot_nt(lt1, blk) + _dot_nt(lt2, blk) + _dot_nt(lt3, blk)

    r_i = lax.broadcasted_iota(jnp.int32, (T, T), 0)
    c_i = lax.broadcasted_iota(jnp.int32, (T, T), 1)
    causal = (r_i // L == c_i // L) & (c_i <= r_i)
    row_chunk = lax.broadcasted_iota(jnp.int32, (T, 1), 0) // L
    lane_chunk = lax.broadcasted_iota(jnp.int32, (1, T), 1) // L
    n_chunks = T // L

    for h in range(ML_HEADS):
        cols = slice(h * ML_DQK, (h + 1) * ML_DQK)
        f_lane = ML_HEADS + h
        bc, blc, igc = b_col[:, f_lane:f_lane + 1], bl_col[:, f_lane:f_lane + 1], gg[:, h:h + 1]
        br, blr, igr = b_row[f_lane:f_lane + 1, :], bl_row[f_lane:f_lane + 1, :], gt[h:h + 1, :]
        g_row = blr - br + igr
        g_col = blc - bc + igc

        m_prev = [m_ref[h][:, 0:1]]
        a_old, a_new, m_cs = [], [], []
        for c in range(n_chunks):
            m_c = jnp.max(jnp.where(lane_chunk == c, g_row, NEG), axis=1, keepdims=True)
            bl = blr[:, c * L:c * L + 1]
            m_next = jnp.maximum(bl + m_prev[c], m_c)
            a_old.append(jnp.exp(bl + m_prev[c] - m_next))
            a_new.append(jnp.exp(m_c - m_next))
            m_cs.append(m_c)
            m_prev.append(m_next)
        mprev_col = jnp.zeros((T, 1), F32)
        mc_col = jnp.zeros((T, 1), F32)
        for c in range(n_chunks):
            mprev_col = jnp.where(row_chunk == c, m_prev[c], mprev_col)
            mc_col = jnp.where(row_chunk == c, m_cs[c], mc_col)

        q_c = qf[:, cols]
        k_c = kf[:, cols]
        q_b = q_c.astype(BF16)
        v_b = v_ref[:, cols].astype(BF16)
        d_log = jnp.where(causal, bc - br + igr, NEG)
        a_log = bc + mprev_col
        m_t = jnp.maximum(a_log, jnp.max(d_log, axis=1, keepdims=True))
        wd = jnp.exp(d_log - m_t) * _dot_nt(q_b, k_c.astype(BF16))
        inter = jnp.exp(a_log - m_t)
        intra = _dot(wd.astype(BF16), v_b)
        wsum = jnp.sum(wd, axis=1, keepdims=True)
        kw = k_c * jnp.exp(g_col - mc_col)
        kw_b = kw.astype(BF16)

        c_st = c_ref[h]
        n_st = n_ref[h]
        qc_parts, qn_parts = [], []
        for c in range(n_chunks):
            rows = slice(c * L, (c + 1) * L)
            qc_parts.append(_dot(q_b[rows, :], c_st.astype(BF16)))
            qn_parts.append(jnp.sum(q_c[rows, :] * n_st, axis=1, keepdims=True))
            c_st = a_old[c] * c_st + a_new[c] * _dot_tn(kw_b[rows, :], v_b[rows, :])
            n_st = a_old[c] * n_st + a_new[c] * jnp.sum(kw[rows, :], axis=0, keepdims=True)
        c_ref[h] = c_st
        n_ref[h] = n_st
        m_ref[h] = jnp.broadcast_to(m_prev[n_chunks], (1, LANES))

        num = inter * jnp.concatenate(qc_parts, axis=0) + intra
        den = inter * jnp.concatenate(qn_parts, axis=0) + wsum
        hh = num / jnp.maximum(jnp.abs(den), jnp.exp(-m_t))
        hn = hh * lax.rsqrt(jnp.mean(hh * hh, axis=-1, keepdims=True) + EPS) * gout_ref[h:h + 1, :]
        out_ref[:, cols] = (hn * _sigmoid(o_ref[:, cols])).astype(BF16)


def _mlstm(ml, conv_w, conv_b, gate_b, g_out, bsz, seq):
    T = min(ML_T, seq)
    assert T == ML_T and seq % T == 0
    t_i = np.arange(T)
    blk = t_i[:, None] // ML_CHUNK == t_i[None, :] // ML_CHUNK
    tri = jnp.asarray(blk & (t_i[None, :] <= t_i[:, None]), BF16)
    blk = jnp.asarray(blk, BF16)
    wblk = lambda j: pl.BlockSpec((None, T, ML_W), lambda b, i, j=j: (b, i, j))
    full = lambda shape: pl.BlockSpec(shape, lambda b, i: (0,) * len(shape))
    return pl.pallas_call(
        _mlstm_kernel,
        out_shape=jax.ShapeDtypeStruct((bsz, seq, ML_W), BF16),
        grid=(bsz, seq // T),
        in_specs=[wblk(0), wblk(1), wblk(2), wblk(3),
                  pl.BlockSpec((None, T, LANES), lambda b, i: (b, i, 4 * ML_W // LANES)),
                  full((ML_CONV, 2 * ML_W)), full((1, 2 * ML_W)), full((1, LANES)),
                  full((ML_HEADS, ML_DQK)), full((T, T)), full((T, T))],
        out_specs=pl.BlockSpec((None, T, ML_W), lambda b, i: (b, i, 0)),
        scratch_shapes=[pltpu.VMEM((CONV_PAD + T, 2 * ML_W), F32),
                        pltpu.VMEM((ML_HEADS, ML_DQK, ML_DQK), F32),
                        pltpu.VMEM((ML_HEADS, 1, ML_DQK), F32),
                        pltpu.VMEM((ML_HEADS, 1, LANES), F32)],
        compiler_params=_params(("parallel", "arbitrary")),
        name="mlstm",
    )(ml, ml, ml, ml, ml, conv_w, conv_b, gate_b, g_out, tri, blk)


NSA_TP = 256
NSA_TQ = 256
NSA_OFF_KC, NSA_OFF_VC, NSA_OFF_KS, NSA_OFF_VS, NSA_OFF_KW, NSA_OFF_VW, NSA_OFF_GATE = (
    NSA_W + k * NSA_KV_W for k in range(7))


def _rope_tables(pos):
    half = NSA_DH // 2
    inv_freq = jnp.power(ROPE_THETA, -jnp.arange(half, dtype=F32) / half)
    ang = pos.astype(F32)[:, None] * inv_freq[None, :]
    cos, sin = jnp.cos(ang), jnp.sin(ang)
    cos = jnp.concatenate([cos, cos], axis=1)
    sin = jnp.concatenate([-sin, sin], axis=1)
    return jnp.tile(cos, (1, 2)), jnp.tile(sin, (1, 2))


def _head_norm_rope(xs, gsum, gain, cos, sin):
    a, b, c = _split3(xs * xs)
    ss = _dot(a, gsum) + _dot(b, gsum) + _dot(c, gsum)
    xn = xs * lax.rsqrt(ss * (1.0 / NSA_DH) + EPS) * gain
    lane = lax.broadcasted_iota(jnp.int32, xs.shape, 1)
    first = (lane % NSA_DH) < NSA_DH // 2
    swapped = jnp.where(first, pltpu.roll(xn, LANES - NSA_DH // 2, axis=1), pltpu.roll(xn, NSA_DH // 2, axis=1))
    return xn * cos + swapped * sin


def _head_norm_rope_t(xt, gsum, gain, cos_t, sin_t):
    a, b, c = _split3(xt * xt)
    ss = _dot(gsum, a) + _dot(gsum, b) + _dot(gsum, c)
    xn = xt * lax.rsqrt(ss * (1.0 / NSA_DH) + EPS) * gain
    row = lax.broadcasted_iota(jnp.int32, xt.shape, 0)
    first = (row % NSA_DH) < NSA_DH // 2
    swapped = jnp.where(first, pltpu.roll(xn, LANES - NSA_DH // 2, axis=0), pltpu.roll(xn, NSA_DH // 2, axis=0))
    return xn * cos_t + swapped * sin_t


def _store_keys(out, slab):
    lane = lax.broadcasted_iota(jnp.int32, slab.shape, 1)
    low = lane < NSA_DH
    rolled = pltpu.roll(slab, NSA_DH, axis=1)
    zero = jnp.zeros_like(slab)
    out[0, 0] = jnp.where(low, slab, zero).astype(BF16)
    out[0, 1] = jnp.where(low, zero, rolled).astype(BF16)
    out[1, 0] = jnp.where(low, rolled, zero).astype(BF16)
    out[1, 1] = jnp.where(low, zero, slab).astype(BF16)


def _store_values_t(out, slab):
    st = slab.T
    out[0] = st[0:NSA_DH, :].astype(BF16)
    out[1] = st[NSA_DH:2 * NSA_DH, :].astype(BF16)


def _nsa_prep_kernel(x_ref, cos_ref, sin_ref, cost_ref, sint_ref, gq_ref, gks_ref, gkw_ref, gsum_ref,
                     q_out, ks_out, vs_out, kw_out, vw_out, kc_out, vc_out):
    cos, sin, gsum = cos_ref[...], sin_ref[...], gsum_ref[...]
    for c in range(NSA_W // LANES):
        xt = x_ref[:, c * LANES:(c + 1) * LANES].T
        qn = _head_norm_rope_t(xt, gsum, gq_ref[...], cost_ref[...], sint_ref[...]) * (NSA_DH ** -0.5)
        q_out[c * LANES:(c + 1) * LANES, :] = qn.astype(BF16)
    ks = _head_norm_rope(x_ref[:, NSA_OFF_KS:NSA_OFF_KS + LANES], gsum, gks_ref[...], cos, sin)
    kw = _head_norm_rope(x_ref[:, NSA_OFF_KW:NSA_OFF_KW + LANES], gsum, gkw_ref[...], cos, sin)
    _store_keys(ks_out, ks)
    _store_keys(kw_out, kw)
    _store_values_t(vs_out, x_ref[:, NSA_OFF_VS:NSA_OFF_VS + LANES])
    _store_values_t(vw_out, x_ref[:, NSA_OFF_VW:NSA_OFF_VW + LANES])
    kc_out[...] = x_ref[:, NSA_OFF_KC:NSA_OFF_KC + LANES]
    vc_out[...] = x_ref[:, NSA_OFF_VC:NSA_OFF_VC + LANES]


def _nsa_cmp_kernel(xk_ref, xv_ref, pek_ref, pev_ref, w1ka_ref, w1kb_ref, w1va_ref, w1vb_ref,
                    b1k_ref, b1v_ref, w2k_ref, w2v_ref, gk_ref, cos_ref, sin_ref, gsum_ref,
                    kc_out, vc_out):
    rows = xk_ref.shape[0]

    def hidden(x, pe, wa, wb, b1):
        first = _dot((x + pe[0:1, :]).astype(BF16), wa[...])
        second = _dot((x + pe[1:2, :]).astype(BF16), wb[...])
        hid = first + pltpu.roll(second, rows - 1, axis=0) + b1[...]
        return (hid * _sigmoid(hid)).astype(BF16)

    kc = _dot(hidden(xk_ref[...], pek_ref, w1ka_ref, w1kb_ref, b1k_ref), w2k_ref[...])
    kc = _head_norm_rope(kc, gsum_ref[...], gk_ref[...], cos_ref[...], sin_ref[...])
    _store_keys(kc_out, kc)
    vc_t = _dot_nt(w2v_ref[...], hidden(xv_ref[...], pev_ref, w1va_ref, w1vb_ref, b1v_ref))
    vc_out[0] = vc_t[0:NSA_DH, :].astype(BF16)
    vc_out[1] = vc_t[NSA_DH:2 * NSA_DH, :].astype(BF16)


NSA_HEAD_SLOT = (0, 2, 1, 3)


def _stacked_scores(k_lo, k_hi, q_pair, bias):
    s = jnp.concatenate([_dot(k_lo, q_pair), _dot(k_hi, q_pair)], axis=1)
    return s + jnp.concatenate([bias] * NSA_GROUP, axis=1)


def _online_update(s, v_t, m_ref, l_ref, acc_ref):
    m_old = m_ref[...]
    m_new = jnp.maximum(m_old, jnp.max(s, axis=0, keepdims=True))
    alpha = jnp.exp(m_old - m_new)
    e = jnp.exp(s - jnp.maximum(m_new, 0.1 * NEG))
    l_ref[...] = alpha * l_ref[...] + jnp.sum(e, axis=0, keepdims=True)
    acc_ref[...] = alpha * acc_ref[...] + _dot(v_t, e.astype(BF16))
    m_ref[...] = m_new


def _nsa_attn_kernel(q_ref, kc_ref, vc_ref, ks_ref, vs_ref, kw_ref, vw_ref, gate_ref, mt_ref, esel_ref, eg_ref,
                     out_ref, ms_s, ls_s, accs_s, mw_s, lw_s, accw_s, *, n_cmp, n_sel):
    TQ = NSA_TQ
    i = pl.program_id(2)
    t0 = i * TQ
    t_row = t0 + lax.broadcasted_iota(jnp.int32, (1, TQ), 1)
    q_pair = jnp.concatenate([q_ref[0:LANES, :], q_ref[LANES:2 * LANES, :]], axis=1)

    rows = kc_ref.shape[1]
    n_i = lax.broadcasted_iota(jnp.int32, (rows, 1), 0)
    cmask = (n_i < n_cmp) & (n_i * CMP_STRIDE + (CMP_BLOCK - 1) <= t_row)
    s = _stacked_scores(kc_ref[0], kc_ref[1], q_pair, jnp.where(cmask, 0.0, NEG))
    e = jnp.exp(s - jnp.maximum(jnp.max(s, axis=0, keepdims=True), 0.1 * NEG))
    p = e / jnp.maximum(jnp.sum(e, axis=0, keepdims=True), 1e-30)
    o_cmp_t = _dot(vc_ref[...], p.astype(BF16))
    psum = p[:, 0:TQ] + p[:, TQ:2 * TQ] + p[:, 2 * TQ:3 * TQ] + p[:, 3 * TQ:4 * TQ]
    p1, p2, p3 = _split3(psum)
    imp = _dot(mt_ref[...], p1) + _dot(mt_ref[...], p2) + _dot(mt_ref[...], p3)
    j_i = lax.broadcasted_iota(jnp.int32, (n_sel, TQ), 0)
    cur = (t0 + lax.broadcasted_iota(jnp.int32, (n_sel, TQ), 1)) // SEL_BLOCK
    forced = (j_i == 0) | (j_i == cur) | (j_i == cur - 1)
    score = jnp.where(j_i > cur, NEG, imp[0:n_sel, :] + SEL_FORCE_BONUS * forced.astype(F32))
    rank = jnp.zeros((n_sel, TQ), jnp.int32)
    for jp in range(n_sel):
        row = score[jp:jp + 1, :]
        beats = (row > score) | ((row == score) & (j_i > jp))
        rank = rank + beats.astype(jnp.int32)
    sel_t = (rank < min(SEL_TOPN, n_sel)).astype(F32)
    sel_b = jnp.concatenate([sel_t, jnp.zeros((LANES - n_sel, TQ), F32)], axis=0).astype(BF16)

    for m_ref, l_ref, acc_ref in ((ms_s, ls_s, accs_s), (mw_s, lw_s, accw_s)):
        m_ref[...] = jnp.full(m_ref.shape, NEG, F32)
        l_ref[...] = jnp.zeros(l_ref.shape, F32)
        acc_ref[...] = jnp.zeros(acc_ref.shape, F32)

    def key_tile(jt, carry):
        k0 = pl.multiple_of(jt * TQ, TQ)
        diff = t_row - (k0 + lax.broadcasted_iota(jnp.int32, (TQ, 1), 0))
        causal = diff >= 0
        smask = (_dot(esel_ref[jt], sel_b) > 0.5) & causal
        s = _stacked_scores(ks_ref[0, pl.ds(k0, TQ), :], ks_ref[1, pl.ds(k0, TQ), :], q_pair,
                            jnp.where(smask, 0.0, NEG))
        _online_update(s, vs_ref[jt], ms_s, ls_s, accs_s)

        @pl.when(jt * TQ + TQ > t0 - WINDOW + 1)
        def _():
            wmask = causal & (diff < WINDOW)
            sw = _stacked_scores(kw_ref[0, pl.ds(k0, TQ), :], kw_ref[1, pl.ds(k0, TQ), :], q_pair,
                                 jnp.where(wmask, 0.0, NEG))
            _online_update(sw, vw_ref[jt], mw_s, lw_s, accw_s)
        return carry

    lax.fori_loop(0, i + 1, key_tile, 0)

    g1, g2, g3 = _split3(_sigmoid(gate_ref[...]))
    gexp = _dot(g1, eg_ref[...]) + _dot(g2, eg_ref[...]) + _dot(g3, eg_ref[...])
    half_w = NSA_GROUP * NSA_DH
    o_sel_t = accs_s[...] / jnp.maximum(ls_s[...], 1e-30)
    o_win_t = accw_s[...] / jnp.maximum(lw_s[...], 1e-30)

    def token_major(o_t, c):
        ce, co = NSA_HEAD_SLOT[2 * c] * TQ, NSA_HEAD_SLOT[2 * c + 1] * TQ
        return jnp.concatenate([o_t[:, ce:ce + TQ], o_t[:, co:co + TQ]], axis=0).T

    for c in range(NSA_GROUP // 2):
        o_cmp = token_major(o_cmp_t, c)
        o_sel = token_major(o_sel_t, c)
        o_win = token_major(o_win_t, c)
        cs = slice(c * LANES, (c + 1) * LANES)
        o = (gexp[:, c * LANES:(c + 1) * LANES] * o_cmp
             + gexp[:, half_w + c * LANES:half_w + (c + 1) * LANES] * o_sel
             + gexp[:, 2 * half_w + c * LANES:2 * half_w + (c + 1) * LANES] * o_win)
        out_ref[:, cs] = o.astype(BF16)


def _nsa_constants(seq):
    n_cmp = (seq - CMP_BLOCK) // CMP_STRIDE + 1
    n_sel = seq // SEL_BLOCK
    rows = seq // CMP_STRIDE
    n = np.arange(rows)[None, :]
    j = np.arange(LANES)[:, None]
    mt = ((n * CMP_STRIDE < j * SEL_BLOCK + SEL_BLOCK) & (n * CMP_STRIDE + CMP_BLOCK > j * SEL_BLOCK)
          & (n < n_cmp) & (j < n_sel))
    s = np.arange(seq)
    esel = (s[:, None] // SEL_BLOCK == np.arange(LANES)[None, :])
    esel = esel.reshape(seq // NSA_TQ, NSA_TQ, LANES)
    eg = np.zeros((NSA_KVH, LANES, N_BRANCH * NSA_GROUP * NSA_DH), np.float32)
    for h in range(NSA_KVH):
        for g in range(NSA_GROUP):
            for br in range(N_BRANCH):
                lane = (h * NSA_GROUP + g) * N_BRANCH + br
                c0 = br * NSA_GROUP * NSA_DH + g * NSA_DH
                eg[h, lane, c0:c0 + NSA_DH] = 1.0
    gsum = np.kron(np.eye(LANES // NSA_DH), np.ones((NSA_DH, NSA_DH)))
    return (n_cmp, n_sel, jnp.asarray(mt, BF16), jnp.asarray(esel, BF16), jnp.asarray(eg, BF16),
            jnp.asarray(gsum, BF16))


def _nsa(nsa, w, bsz, seq):
    assert seq % NSA_TQ == 0 and seq % NSA_TP == 0
    n_cmp, n_sel, mt, esel, eg, gsum = _nsa_constants(seq)
    rows = seq // CMP_STRIDE
    cos, sin = _rope_tables(jnp.arange(seq))
    cos_c, sin_c = _rope_tables(jnp.arange(rows) * CMP_STRIDE + (CMP_BLOCK - 1))
    c2 = lambda shape: pl.BlockSpec(shape, lambda b, i: (0,) * len(shape))
    n_tiles = seq // NSA_TP
    k_shape = jax.ShapeDtypeStruct((bsz, NSA_KVH, 2, seq, LANES), BF16)
    k_spec = pl.BlockSpec((None, NSA_KVH, 2, NSA_TP, LANES), lambda b, i: (b, 0, 0, i, 0))
    v_shape = jax.ShapeDtypeStruct((bsz, NSA_KVH, n_tiles, NSA_DH, NSA_TP), BF16)
    v_spec = pl.BlockSpec((None, NSA_KVH, None, NSA_DH, NSA_TP), lambda b, i: (b, 0, i, 0, 0))
    raw_shape = jax.ShapeDtypeStruct((bsz, seq, LANES), F32)
    raw_spec = pl.BlockSpec((None, NSA_TP, LANES), lambda b, i: (b, i, 0))
    qn, ksd, vsd, kwd, vwd, kc_raw, vc_raw = pl.pallas_call(
        _nsa_prep_kernel,
        out_shape=(jax.ShapeDtypeStruct((bsz, NSA_W, seq), BF16), k_shape, v_shape, k_shape, v_shape,
                   raw_shape, raw_shape),
        grid=(bsz, n_tiles),
        in_specs=[pl.BlockSpec((None, NSA_TP, NSA_COLS), lambda b, i: (b, i, 0)),
                  pl.BlockSpec((NSA_TP, LANES), lambda b, i: (i, 0)),
                  pl.BlockSpec((NSA_TP, LANES), lambda b, i: (i, 0)),
                  pl.BlockSpec((LANES, NSA_TP), lambda b, i: (0, i)),
                  pl.BlockSpec((LANES, NSA_TP), lambda b, i: (0, i)),
                  c2((LANES, 1)), c2((1, LANES)), c2((1, LANES)), c2((LANES, LANES))],
        out_specs=(pl.BlockSpec((None, NSA_W, NSA_TP), lambda b, i: (b, 0, i)),
                   k_spec, v_spec, k_spec, v_spec, raw_spec, raw_spec),
        compiler_params=_params(("parallel", "parallel")),
        name="nsa_prep",
    )(nsa, cos, sin, cos.T, sin.T, w["nsa_gq"].reshape(LANES, 1), w["nsa_gks"], w["nsa_gkw"], gsum)

    wide = CMP_STRIDE * LANES
    xk = kc_raw.reshape(bsz, rows, wide)
    xv = vc_raw.reshape(bsz, rows, wide)
    c1 = lambda shape: pl.BlockSpec(shape, lambda b: (0,) * len(shape))
    kc_shape = jax.ShapeDtypeStruct((bsz, NSA_KVH, 2, rows, LANES), BF16)
    kc_spec = pl.BlockSpec((None, NSA_KVH, 2, rows, LANES), lambda b: (b, 0, 0, 0, 0))
    vc_shape = jax.ShapeDtypeStruct((bsz, NSA_KVH, NSA_DH, rows), BF16)
    vc_spec = pl.BlockSpec((None, NSA_KVH, NSA_DH, rows), lambda b: (b, 0, 0, 0))
    hid2 = NSA_KVH * CMP_HIDDEN
    kcd, vcd = pl.pallas_call(
        _nsa_cmp_kernel,
        out_shape=(kc_shape, vc_shape),
        grid=(bsz,),
        in_specs=[pl.BlockSpec((None, rows, wide), lambda b: (b, 0, 0)),
                  pl.BlockSpec((None, rows, wide), lambda b: (b, 0, 0)),
                  c1((2, wide)), c1((2, wide)),
                  c1((wide, hid2)), c1((wide, hid2)), c1((wide, hid2)), c1((wide, hid2)),
                  c1((1, hid2)), c1((1, hid2)), c1((hid2, LANES)), c1((LANES, hid2)),
                  c1((1, LANES)), c1((rows, LANES)), c1((rows, LANES)), c1((LANES, LANES))],
        out_specs=(kc_spec, vc_spec),
        compiler_params=_params(("parallel",)),
        name="nsa_compress",
    )(xk, xv, w["cmp_pe_k"], w["cmp_pe_v"], w["cmp_w1k_a"], w["cmp_w1k_b"], w["cmp_w1v_a"], w["cmp_w1v_b"],
      w["cmp_b1_k"], w["cmp_b1_v"], w["cmp_w2_k"], w["cmp_w2_v"].T, w["nsa_gkc"], cos_c, sin_c, gsum)

    TQ = NSA_TQ
    half_w = NSA_GROUP * NSA_DH
    c3 = lambda shape: pl.BlockSpec(shape, lambda b, h, i: (0,) * len(shape))
    assert NSA_TP == TQ
    keys = lambda r: pl.BlockSpec((None, None, 2, r, LANES), lambda b, h, i: (b, h, 0, 0, 0))
    vals = pl.BlockSpec((None, None, seq // TQ, NSA_DH, TQ), lambda b, h, i: (b, h, 0, 0, 0))
    stat = lambda: pltpu.VMEM((1, NSA_GROUP * TQ), F32)
    accu = lambda: pltpu.VMEM((NSA_DH, NSA_GROUP * TQ), F32)
    return pl.pallas_call(
        functools.partial(_nsa_attn_kernel, n_cmp=n_cmp, n_sel=n_sel),
        out_shape=jax.ShapeDtypeStruct((bsz, seq, NSA_W), BF16),
        grid=(bsz, NSA_KVH, seq // TQ),
        in_specs=[pl.BlockSpec((None, half_w, TQ), lambda b, h, i: (b, h, i)),
                  keys(rows), pl.BlockSpec((None, None, NSA_DH, rows), lambda b, h, i: (b, h, 0, 0)),
                  keys(seq), vals, keys(seq), vals,
                  pl.BlockSpec((None, TQ, LANES), lambda b, h, i: (b, i, NSA_OFF_GATE // LANES)),
                  c3((LANES, rows)), c3((seq // TQ, TQ, LANES)),
                  pl.BlockSpec((None, LANES, N_BRANCH * half_w), lambda b, h, i: (h, 0, 0))],
        out_specs=pl.BlockSpec((None, TQ, half_w), lambda b, h, i: (b, i, h)),
        scratch_shapes=[stat(), stat(), accu(), stat(), stat(), accu()],
        compiler_params=_params(("parallel", "parallel", "arbitrary")),
        name="nsa_attention",
    )(qn, kcd, vcd, ksd, vsd, kwd, vwd, nsa, mt, esel, eg)


POST_TM = 256


def _rms(v, gain):
    return v * lax.rsqrt(jnp.mean(v * v, axis=-1, keepdims=True) + EPS) * gain


def _memkv_kernel(mem_ref, g_ref, w_ref, gk_ref, k_out, v_out):
    u = _rms(mem_ref[...], g_ref[...]).astype(BF16)
    kv = _dot(u, w_ref[...])
    for h in range(MEM_HEADS):
        cs = slice(h * MEM_DH, (h + 1) * MEM_DH)
        k_out[:, cs] = _rms(kv[:, cs], gk_ref[...]).astype(BF16)
    v_out[...] = kv[:, MEM_W:2 * MEM_W].astype(BF16)


def _memkv(mem, g_mem, w_mem_kv, g_k):
    bsz, m_len, _ = mem.shape
    c1 = lambda shape: pl.BlockSpec(shape, lambda b: (0,) * len(shape))
    shape = jax.ShapeDtypeStruct((bsz, m_len, MEM_W), BF16)
    spec = pl.BlockSpec((None, m_len, MEM_W), lambda b: (b, 0, 0))
    return pl.pallas_call(
        _memkv_kernel, out_shape=(shape, shape), grid=(bsz,),
        in_specs=[pl.BlockSpec((None, m_len, D_MODEL), lambda b: (b, 0, 0)),
                  c1((1, D_MODEL)), c1((D_MODEL, 2 * MEM_W)), c1((1, MEM_DH))],
        out_specs=(spec, spec),
        compiler_params=_params(("parallel",)),
        name="mem_kv",
    )(mem, g_mem, w_mem_kv, g_k)


def _post_kernel(x_ref, hml_ref, hnsa_ref, mk_ref, mv_ref, gmix_ref, wmq_ref, gmq_ref, wgate_ref,
                 wml_ref, wnsa_ref, wmem_ref, wout_ref, gffn_ref, wrh_ref, wrl_ref, br_ref,
                 x1_out, t_out, logit_out):
    x = x_ref[...]
    ub = _rms(x, gmix_ref[...]).astype(BF16)
    mq = _dot(ub, wmq_ref[...])
    heads = []
    for h in range(MEM_HEADS):
        cs = slice(h * MEM_DH, (h + 1) * MEM_DH)
        qn = (_rms(mq[:, cs], gmq_ref[...]) * (MEM_DH ** -0.5)).astype(BF16)
        s = _dot_nt(qn, mk_ref[:, cs])
        e = jnp.exp(s - jnp.max(s, axis=1, keepdims=True))
        p = e / jnp.sum(e, axis=1, keepdims=True)
        heads.append(_dot(p.astype(BF16), mv_ref[:, cs]))
    hmem = jnp.concatenate(heads, axis=1).astype(BF16)
    gates = _sigmoid(_dot(ub, wgate_ref[...]))
    merged = (gates[:, 0:D_MODEL] * _dot(hml_ref[...], wml_ref[...])
              + gates[:, D_MODEL:2 * D_MODEL] * _dot(hnsa_ref[...], wnsa_ref[...])
              + gates[:, 2 * D_MODEL:3 * D_MODEL] * _dot(hmem, wmem_ref[...]))
    x1 = x + _dot(merged.astype(BF16), wout_ref[...])
    x1_out[...] = x1
    t = _rms(x1, gffn_ref[...])
    t_out[...] = t
    t1, t2, _ = _split3(t)
    logit_out[...] = _dot(t1, wrh_ref[...]) + _dot(t1, wrl_ref[...]) + _dot(t2, wrh_ref[...]) + br_ref[...]


def _post(x2, hml, hnsa, mk, mv, w, seq):
    n = x2.shape[0]
    tm = POST_TM
    assert seq % tm == 0
    per = seq // tm
    m_len = mk.shape[1]
    row = lambda width: pl.BlockSpec((tm, width), lambda i: (i, 0))
    c1 = lambda shape: pl.BlockSpec(shape, lambda i: (0,) * len(shape))
    memspec = pl.BlockSpec((None, m_len, MEM_W), lambda i: (i // per, 0, 0))
    return pl.pallas_call(
        _post_kernel,
        out_shape=(jax.ShapeDtypeStruct((n, D_MODEL), F32), jax.ShapeDtypeStruct((n, D_MODEL), F32),
                   jax.ShapeDtypeStruct((n, LANES), F32)),
        grid=(n // tm,),
        in_specs=[row(D_MODEL), row(ML_W), row(NSA_W), memspec, memspec,
                  c1((1, D_MODEL)), c1((D_MODEL, MEM_W)), c1((1, MEM_DH)), c1((D_MODEL, N_BRANCH * D_MODEL)),
                  c1((ML_W, D_MODEL)), c1((NSA_W, D_MODEL)), c1((MEM_W, D_MODEL)), c1((D_MODEL, D_MODEL)),
                  c1((1, D_MODEL)), c1((D_MODEL, LANES)), c1((D_MODEL, LANES)), c1((1, LANES))],
        out_specs=(row(D_MODEL), row(D_MODEL), row(LANES)),
        compiler_params=_params(("parallel",)),
        name="post",
    )(x2, hml, hnsa, mk, mv, w["g_mix"], w["w_mq"], w["mem_gq"], w["w_gate"], w["w_br_ml"], w["w_br_nsa"],
      w["w_br_mem"], w["w_out"], w["g_ffn"], w["w_router_hi"], w["w_router_lo"], w["b_router"])


ROUTE_T = 512
MOE_BLK = 256
MOE_TD = 256


def _route_kernel(logit_ref, triu_ref, e_out, w_out, p_out, wtm_out, cnt_out, carry_ref):
    T = ROUTE_T

    @pl.when(pl.program_id(0) == 0)
    def _():
        carry_ref[...] = jnp.zeros(carry_ref.shape, F32)

    sc = logit_ref[...].T[0:N_EXPERTS, :]
    j_i = lax.broadcasted_iota(jnp.int32, (N_EXPERTS, T), 0)
    rank = jnp.zeros((N_EXPERTS, T), jnp.int32)
    for jp in range(N_EXPERTS):
        row = sc[jp:jp + 1, :]
        beats = (row > sc) | ((row == sc) & (j_i > jp))
        rank = rank + beats.astype(jnp.int32)
    sel = rank < TOP_K
    e = jnp.where(sel, jnp.exp(sc - jnp.max(sc, axis=0, keepdims=True)), 0.0)
    wgt = e / jnp.sum(e, axis=0, keepdims=True)
    self = sel.astype(F32)
    pos = _dot(self.astype(BF16), triu_ref[...]) + carry_ref[:, 0:1]
    carry_ref[...] = carry_ref[...] + jnp.sum(self, axis=1, keepdims=True)
    e_rows, w_rows, p_rows = [], [], []
    for r in range(TOP_K):
        hit = rank == r
        e_rows.append(jnp.sum(jnp.where(hit, j_i, 0), axis=0, keepdims=True))
        w_rows.append(jnp.sum(jnp.where(hit, wgt, 0.0), axis=0, keepdims=True))
        p_rows.append(jnp.sum(jnp.where(hit, pos, 0.0), axis=0, keepdims=True))
    e_out[...] = jnp.concatenate(e_rows, axis=0)
    w4 = jnp.concatenate(w_rows, axis=0)
    w_out[...] = w4
    p_out[...] = jnp.concatenate(p_rows, axis=0).astype(jnp.int32)
    wtm_out[...] = jnp.concatenate([w4, jnp.zeros((LANES - TOP_K, T), F32)], axis=0).T
    cnt_out[...] = carry_ref[...]


def _route(logits):
    n = logits.shape[0]
    T = ROUTE_T
    assert n % T == 0
    s = np.arange(T)
    triu = jnp.asarray(s[:, None] < s[None, :], BF16)
    row4 = lambda: pl.BlockSpec((TOP_K, T), lambda i: (0, i))
    return pl.pallas_call(
        _route_kernel,
        out_shape=(jax.ShapeDtypeStruct((TOP_K, n), jnp.int32), jax.ShapeDtypeStruct((TOP_K, n), F32),
                   jax.ShapeDtypeStruct((TOP_K, n), jnp.int32), jax.ShapeDtypeStruct((n, LANES), F32),
                   jax.ShapeDtypeStruct((N_EXPERTS, LANES), F32)),
        grid=(n // T,),
        in_specs=[pl.BlockSpec((T, LANES), lambda i: (i, 0)), pl.BlockSpec((T, T), lambda i: (0, 0))],
        out_specs=(row4(), row4(), row4(), pl.BlockSpec((T, LANES), lambda i: (i, 0)),
                   pl.BlockSpec((N_EXPERTS, LANES), lambda i: (0, 0))),
        scratch_shapes=[pltpu.VMEM((N_EXPERTS, LANES), F32)],
        compiler_params=_params(("arbitrary",)),
        name="moe_route",
    )(logits, triu)


def _dispatch_kernel(dest_ref, t_ref, xs_in, xs_out, sem):
    del xs_in

    def row_copy(j, d):
        return pltpu.make_async_copy(t_ref.at[pl.ds(j, 1)], xs_out.at[pl.ds(d, 1)], sem)

    def issue(j, carry):
        for r in range(TOP_K):
            row_copy(j, dest_ref[r, j]).start(priority=r % 2)
        return carry

    lax.fori_loop(0, MOE_TD, issue, 0)

    def drain(j, carry):
        for r in range(TOP_K):
            row_copy(j, dest_ref[r, j]).wait()
        return carry

    lax.fori_loop(0, MOE_TD, drain, 0)


def _dispatch(t, dest3, cap):
    n = t.shape[0]
    zeros = jnp.zeros((cap, D_MODEL), F32)
    return pl.pallas_call(
        _dispatch_kernel,
        out_shape=jax.ShapeDtypeStruct((cap, D_MODEL), F32),
        grid=(n // MOE_TD,),
        in_specs=[pl.BlockSpec((None, TOP_K, MOE_TD), lambda i: (i, 0, 0), memory_space=pltpu.SMEM),
                  pl.BlockSpec((MOE_TD, D_MODEL), lambda i: (i, 0)),
                  pl.BlockSpec(memory_space=pl.ANY)],
        out_specs=pl.BlockSpec(memory_space=pl.ANY),
        scratch_shapes=[pltpu.SemaphoreType.DMA],
        input_output_aliases={2: 0},
        compiler_params=_params(("arbitrary",)),
        name="moe_dispatch",
    )(dest3, t, zeros)


def _ffn_kernel(be_ref, nu_ref, xs_ref, wgu_ref, bgu_ref, wd_ref, bd_ref, ys_ref, wgu_s, wd_s):
    i = pl.program_id(0)
    used = i < nu_ref[0]
    new_expert = (i == 0) | (be_ref[i] != be_ref[jnp.maximum(i - 1, 0)])

    @pl.when(used & new_expert)
    def _():
        wgu_s[...] = wgu_ref[...].astype(BF16)
        bits = lax.shift_right_logical(pltpu.bitcast(wd_ref[...].astype(BF16).astype(F32), jnp.uint32),
                                       jnp.uint32(16))
        wd_s[...] = bits | lax.shift_left(bits, jnp.uint32(16))

    @pl.when(used)
    def _():
        hgu = _dot(xs_ref[...].astype(BF16), wgu_s[...]) + bgu_ref[...]
        gate = jnp.minimum(hgu, SWIGLU_LIMIT)
        up1 = jnp.clip(hgu, -SWIGLU_LIMIT, SWIGLU_LIMIT) + 1.0
        even = lax.broadcasted_iota(jnp.int32, (1, 2 * D_FF), 1) % 2 == 0
        act = jnp.where(even, gate * _sigmoid(SWIGLU_ALPHA * gate) * pltpu.roll(up1, 2 * D_FF - 1, axis=1), 0.0)
        ys_ref[...] = _dot(act.astype(BF16), pltpu.bitcast(wd_s[...], BF16)) + bd_ref[...]

    @pl.when(jnp.logical_not(used))
    def _():
        ys_ref[...] = jnp.zeros(ys_ref.shape, F32)


def _ffn(xs, blk_e, n_used, w):
    cap = xs.shape[0]
    espec = lambda r, c: pl.BlockSpec((None, r, c), lambda i, be, nu: (be[i], 0, 0))
    rows = pl.BlockSpec((MOE_BLK, D_MODEL), lambda i, be, nu: (i, 0))
    return pl.pallas_call(
        _ffn_kernel,
        out_shape=jax.ShapeDtypeStruct((cap, D_MODEL), F32),
        grid_spec=pltpu.PrefetchScalarGridSpec(
            num_scalar_prefetch=2, grid=(cap // MOE_BLK,),
            in_specs=[rows, espec(D_MODEL, 2 * D_FF), espec(1, 2 * D_FF), espec(D_FF, D_MODEL),
                      espec(1, D_MODEL)],
            out_specs=rows,
            scratch_shapes=[pltpu.VMEM((D_MODEL, 2 * D_FF), BF16), pltpu.VMEM((D_FF, D_MODEL), jnp.uint32)]),
        compiler_params=_params(("arbitrary",)),
        name="moe_ffn",
    )(blk_e, n_used, xs, w["w_gu"], w["b_gu"], w["w_down"], w["b_down"])


def _slots_kernel(ps_ref, e_ref, p_ref, d_out):
    e = e_ref[...]
    d = p_ref[...]
    for k in range(N_EXPERTS):
        d = d + jnp.where(e == k, ps_ref[k], 0)
    d_out[...] = d


def _slots(pad_start, e_idx, pos):
    n = e_idx.shape[1]
    blk = pl.BlockSpec((TOP_K, MOE_TD), lambda i, ps: (0, i))
    return pl.pallas_call(
        _slots_kernel,
        out_shape=jax.ShapeDtypeStruct((n // MOE_TD, TOP_K, MOE_TD), jnp.int32),
        grid_spec=pltpu.PrefetchScalarGridSpec(
            num_scalar_prefetch=1, grid=(n // MOE_TD,), in_specs=[blk, blk],
            out_specs=pl.BlockSpec((None, TOP_K, MOE_TD), lambda i, ps: (i, 0, 0))),
        compiler_params=_params(("parallel",)),
        name="moe_slots",
    )(pad_start, e_idx, pos)


def _combine_kernel(dest_ref, x1_ref, wtm_ref, ys_ref, out_ref, buf_ref, sem):
    def row_copy(r, j, d):
        return pltpu.make_async_copy(ys_ref.at[pl.ds(d, 1)], buf_ref.at[r, pl.ds(j, 1)], sem)

    def issue(j, carry):
        for r in range(TOP_K):
            row_copy(r, j, dest_ref[r, j]).start(priority=r % 2)
        return carry

    lax.fori_loop(0, MOE_TD, issue, 0)

    def drain(j, carry):
        for r in range(TOP_K):
            row_copy(r, j, dest_ref[r, j]).wait()
        return carry

    lax.fori_loop(0, MOE_TD, drain, 0)
    acc = x1_ref[...]
    wtm = wtm_ref[...]
    for r in range(TOP_K):
        acc = acc + wtm[:, r:r + 1] * buf_ref[r]
    out_ref[...] = acc


def _combine(x1, wtm, dest3, ys):
    n = x1.shape[0]
    row = lambda width: pl.BlockSpec((MOE_TD, width), lambda i: (i, 0))
    return pl.pallas_call(
        _combine_kernel,
        out_shape=jax.ShapeDtypeStruct((n, D_MODEL), F32),
        grid=(n // MOE_TD,),
        in_specs=[pl.BlockSpec((None, TOP_K, MOE_TD), lambda i: (i, 0, 0), memory_space=pltpu.SMEM),
                  row(D_MODEL), row(LANES), pl.BlockSpec(memory_space=pl.ANY)],
        out_specs=row(D_MODEL),
        scratch_shapes=[pltpu.VMEM((TOP_K, MOE_TD, D_MODEL), F32), pltpu.SemaphoreType.DMA],
        compiler_params=_params(("arbitrary",)),
        name="moe_combine",
    )(dest3, x1, wtm, ys)


def _moe(x1, t, logits, w):
    n = x1.shape[0]
    e_idx, _, pos, wtm, cnt = _route(logits)
    counts = cnt[:, 0].astype(jnp.int32)
    padded = (counts + MOE_BLK - 1) // MOE_BLK * MOE_BLK
    pad_end = jnp.cumsum(padded)
    pad_start = pad_end - padded
    cap = -(-(n * TOP_K + N_EXPERTS * (MOE_BLK - 1)) // MOE_BLK) * MOE_BLK
    dest3 = _slots(pad_start, e_idx, pos)
    starts = jnp.arange(cap // MOE_BLK, dtype=jnp.int32) * MOE_BLK
    blk_e = jnp.minimum(jnp.sum((pad_end[None, :] <= starts[:, None]).astype(jnp.int32), axis=1), N_EXPERTS - 1)
    n_used = (pad_end[-1:] // MOE_BLK).astype(jnp.int32)
    xs = _dispatch(t, dest3, cap)
    ys = _ffn(xs, blk_e, n_used, w)
    return _combine(x1, wtm, dest3, ys)


def _cmp_weights(pos, w1, b1, w2):
    eye = jnp.eye(NSA_KVH, dtype=F32)
    halves = []
    for part in (w1[:CMP_STRIDE], w1[CMP_STRIDE:]):
        big = jnp.einsum("ldf,hg->lhdgf", part, eye)
        halves.append(big.reshape(CMP_STRIDE * LANES, NSA_KVH * CMP_HIDDEN).astype(BF16))
    pe = jnp.stack([jnp.broadcast_to(part[:, None, :], (CMP_STRIDE, NSA_KVH, NSA_DH)).reshape(-1)
                    for part in (pos[:CMP_STRIDE], pos[CMP_STRIDE:])])
    w2big = jnp.einsum("fd,hg->hfgd", w2, eye).reshape(NSA_KVH * CMP_HIDDEN, LANES).astype(BF16)
    return halves[0], halves[1], pe, jnp.tile(b1, NSA_KVH).reshape(1, -1), w2big


def _prep_weights(p):
    w = {}
    offs = np.concatenate([[0], np.cumsum(IN_SPLITS)])
    col = lambda k: p["w_in"][:, offs[k]:offs[k + 1]]
    zpad = lambda n: jnp.zeros((D_MODEL, n), F32)
    w["w_ml"] = jnp.concatenate([col(0), col(1), col(2), col(3), col(4), col(5),
                                 zpad(LANES - 2 * ML_HEADS)], axis=1).astype(BF16)
    w["w_nsa"] = jnp.concatenate([col(k) for k in range(6, 14)] + [zpad(LANES - NSA_QH * 3)], axis=1).astype(BF16)
    w["g_mix"] = p["g_mix"].reshape(1, D_MODEL)
    w["conv_w"] = p["ml_conv_w"]
    w["conv_b"] = p["ml_conv_b"].reshape(1, 2 * ML_W)
    w["ml_gate_b"] = jnp.concatenate([p["ml_b_i"], p["ml_b_f"], jnp.zeros((LANES - 2 * ML_HEADS,), F32)]).reshape(1, LANES)
    w["ml_g_out"] = p["ml_g_out"]

    pair = lambda g: jnp.tile(g, LANES // NSA_DH).reshape(1, LANES)
    w["nsa_gq"] = pair(p["nsa_g_q"])
    w["nsa_gkc"], w["nsa_gks"], w["nsa_gkw"] = (pair(p["nsa_g_k"][k]) for k in range(3))
    (w["cmp_w1k_a"], w["cmp_w1k_b"], w["cmp_pe_k"], w["cmp_b1_k"], w["cmp_w2_k"]) = _cmp_weights(
        p["cmp_pos_k"], p["cmp_w1_k"], p["cmp_b1_k"], p["cmp_w2_k"])
    (w["cmp_w1v_a"], w["cmp_w1v_b"], w["cmp_pe_v"], w["cmp_b1_v"], w["cmp_w2_v"]) = _cmp_weights(
        p["cmp_pos_v"], p["cmp_w1_v"], p["cmp_b1_v"], p["cmp_w2_v"])

    w["g_mem"] = p["g_mem"].reshape(1, D_MODEL)
    w["w_mem_kv"] = p["w_mem_kv"].astype(BF16)
    w["mem_gq"] = p["mem_g_q"].reshape(1, MEM_DH)
    w["mem_gk"] = p["mem_g_k"].reshape(1, MEM_DH)
    w["w_mq"] = col(14).astype(BF16)
    w["w_gate"] = col(15).astype(BF16)
    for k in ("w_br_ml", "w_br_nsa", "w_br_mem", "w_out"):
        w[k] = p[k].astype(BF16)
    w["g_ffn"] = p["g_ffn"].reshape(1, D_MODEL)
    w_r = jnp.concatenate([p["w_router"], jnp.zeros((D_MODEL, LANES - N_EXPERTS), F32)], axis=1)
    w["w_router_hi"] = w_r.astype(BF16)
    w["w_router_lo"] = (w_r - w["w_router_hi"].astype(F32)).astype(BF16)
    w["b_router"] = jnp.concatenate([p["b_router"], jnp.zeros((LANES - N_EXPERTS,), F32)]).reshape(1, LANES)
    w["w_gu"] = p["w_gu"]
    w["b_gu"] = p["b_gu"][:, None, :]
    w["w_down"] = p["w_down"]
    w["b_down"] = p["b_down"][:, None, :]
    return w


_PARAM_NAMES = ("g_mix", "w_in", "ml_conv_w", "ml_conv_b", "ml_b_i", "ml_b_f", "ml_g_out", "nsa_g_q", "nsa_g_k",
                "cmp_pos_k", "cmp_w1_k", "cmp_b1_k", "cmp_w2_k", "cmp_pos_v", "cmp_w1_v", "cmp_b1_v", "cmp_w2_v",
                "g_mem", "w_mem_kv", "mem_g_q", "mem_g_k", "w_br_ml", "w_br_nsa", "w_br_mem", "w_out", "g_ffn",
                "w_router", "b_router", "w_gu", "b_gu", "w_down", "b_down")


def _layer(x, mem, p):
    bsz, seq, d = x.shape
    w = _prep_weights(p)
    x2 = x.reshape(bsz * seq, d)
    ml, nsa = _proj(x2, w["g_mix"], w["w_ml"], w["w_nsa"])
    h_ml = _mlstm(ml.reshape(bsz, seq, ML_COLS), w["conv_w"], w["conv_b"], w["ml_gate_b"], w["ml_g_out"], bsz, seq)
    h_nsa = _nsa(nsa.reshape(bsz, seq, NSA_COLS), w, bsz, seq)
    mk, mv = _memkv(mem, w["g_mem"], w["w_mem_kv"], w["mem_gk"])
    x1, t, logits = _post(x2, h_ml.reshape(bsz * seq, ML_W), h_nsa.reshape(bsz * seq, NSA_W), mk, mv, w, seq)
    return _moe(x1, t, logits, w).reshape(bsz, seq, d)


def kernel(x, mem, g_mix, w_in, ml_conv_w, ml_conv_b, ml_b_i, ml_b_f, ml_g_out, nsa_g_q, nsa_g_k, cmp_pos_k,
           cmp_w1_k, cmp_b1_k, cmp_w2_k, cmp_pos_v, cmp_w1_v, cmp_b1_v, cmp_w2_v, g_mem, w_mem_kv, mem_g_q,
           mem_g_k, w_br_ml, w_br_nsa, w_br_mem, w_out, g_ffn, w_router, b_router, w_gu, b_gu, w_down, b_down):
    stacked = (g_mix, w_in, ml_conv_w, ml_conv_b, ml_b_i, ml_b_f, ml_g_out, nsa_g_q, nsa_g_k, cmp_pos_k,
               cmp_w1_k, cmp_b1_k, cmp_w2_k, cmp_pos_v, cmp_w1_v, cmp_b1_v, cmp_w2_v, g_mem, w_mem_kv, mem_g_q,
               mem_g_k, w_br_ml, w_br_nsa, w_br_mem, w_out, g_ffn, w_router, b_router, w_gu, b_gu, w_down, b_down)
    h = x
    for layer in range(g_mix.shape[0]):
        h = _layer(h, mem, {name: v[layer] for name, v in zip(_PARAM_NAMES, stacked)})
    return h
```

```python
import functools

import jax
import jax.numpy as jnp
import numpy as np
from jax import lax
from jax.experimental import pallas as pl
from jax.experimental.pallas import tpu as pltpu

F32 = jnp.float32
BF16 = jnp.bfloat16

D_MODEL = 1024
ML_HEADS = 4
ML_DQK = 128
ML_W = ML_HEADS * ML_DQK
ML_CHUNK = 64
ML_CONV = 4
NSA_QH = 8
NSA_KVH = 2
NSA_DH = 64
NSA_GROUP = NSA_QH // NSA_KVH
NSA_W = NSA_QH * NSA_DH
NSA_KV_W = NSA_KVH * NSA_DH
CMP_BLOCK = 32
CMP_STRIDE = 16
CMP_HIDDEN = 256
SEL_BLOCK = 64
SEL_TOPN = 8
SEL_FORCE_BONUS = 1e4
WINDOW = 512
MEM_HEADS = 4
MEM_DH = 128
MEM_W = MEM_HEADS * MEM_DH
N_EXPERTS = 32
TOP_K = 4
D_FF = D_MODEL
SWIGLU_ALPHA = 1.702
SWIGLU_LIMIT = 7.0
N_BRANCH = 3
ROPE_THETA = 10000.0
EPS = 1e-6
NEG = -1e30

LANES = 128
VMEM_LIMIT = 56 * 1024 * 1024

IN_SPLITS = (ML_W, ML_W, ML_W, ML_W, ML_HEADS, ML_HEADS,
             NSA_W, NSA_KV_W, NSA_KV_W, NSA_KV_W, NSA_KV_W, NSA_KV_W, NSA_KV_W, NSA_QH * 3,
             MEM_W, N_BRANCH * D_MODEL)

ML_COLS = 4 * ML_W + LANES
NSA_COLS = NSA_W + 6 * NSA_KV_W + LANES


def _dot(a, b):
    return jnp.dot(a, b, preferred_element_type=F32)


def _dot_nt(a, b):
    return lax.dot_general(a, b, (((1,), (1,)), ((), ())), preferred_element_type=F32)


def _dot_tn(a, b):
    return lax.dot_general(a, b, (((0,), (0,)), ((), ())), preferred_element_type=F32)


def _split3(v):
    a = v.astype(BF16)
    r = v - a.astype(F32)
    b = r.astype(BF16)
    c = (r - b.astype(F32)).astype(BF16)
    return a, b, c


def _sigmoid(v):
    return 1.0 / (1.0 + jnp.exp(-v))


def _log_sigmoid(v):
    return jnp.minimum(v, 0.0) - jnp.log1p(jnp.exp(-jnp.abs(v)))


def _params(sem):
    return pltpu.CompilerParams(dimension_semantics=sem, vmem_limit_bytes=VMEM_LIMIT)


PROJ_TM = 512


def _proj_kernel(x_ref, g_ref, wml_ref, wnsa_ref, oml_ref, onsa_ref):
    x = x_ref[...]
    u = x * lax.rsqrt(jnp.mean(x * x, axis=-1, keepdims=True) + EPS) * g_ref[...]
    ub = u.astype(BF16)
    oml_ref[...] = _dot(ub, wml_ref[...])
    onsa_ref[...] = _dot(ub, wnsa_ref[...])


def _proj(x2, g_mix, w_ml, w_nsa):
    n = x2.shape[0]
    tm = min(PROJ_TM, n)
    return pl.pallas_call(
        _proj_kernel,
        out_shape=(jax.ShapeDtypeStruct((n, ML_COLS), F32), jax.ShapeDtypeStruct((n, NSA_COLS), F32)),
        grid=(n // tm,),
        in_specs=[pl.BlockSpec((tm, D_MODEL), lambda i: (i, 0)),
                  pl.BlockSpec((1, D_MODEL), lambda i: (0, 0)),
                  pl.BlockSpec((D_MODEL, ML_COLS), lambda i: (0, 0)),
                  pl.BlockSpec((D_MODEL, NSA_COLS), lambda i: (0, 0))],
        out_specs=(pl.BlockSpec((tm, ML_COLS), lambda i: (i, 0)),
                   pl.BlockSpec((tm, NSA_COLS), lambda i: (i, 0))),
        compiler_params=_params(("parallel",)),
        name="in_proj",
    )(x2, g_mix, w_ml, w_nsa)


ML_T = 256
CONV_PAD = 8


def _mlstm_kernel(q_ref, k_ref, v_ref, o_ref, g_ref, cw_ref, cb_ref, gb_ref, gout_ref, tri_ref, blk_ref,
                  out_ref, buf_ref, c_ref, n_ref, m_ref):
    T, L = ML_T, ML_CHUNK

    @pl.when(pl.program_id(1) == 0)
    def _():
        buf_ref[0:CONV_PAD, :] = jnp.zeros((CONV_PAD, 2 * ML_W), F32)
        c_ref[...] = jnp.zeros(c_ref.shape, F32)
        n_ref[...] = jnp.zeros(n_ref.shape, F32)
        m_ref[...] = jnp.zeros(m_ref.shape, F32)

    buf_ref[CONV_PAD:CONV_PAD + T, 0:ML_W] = q_ref[...]
    buf_ref[CONV_PAD:CONV_PAD + T, ML_W:2 * ML_W] = k_ref[...]
    y = cb_ref[...]
    for j in range(ML_CONV):
        lo = CONV_PAD - (ML_CONV - 1) + j
        y = y + cw_ref[j:j + 1, :] * buf_ref[lo:lo + T, :]
    buf_ref[0:CONV_PAD, :] = buf_ref[T:T + CONV_PAD, :]
    qk = y * _sigmoid(y)
    qf = qk[:, 0:ML_W] * (ML_DQK ** -0.5)
    kf = qk[:, ML_W:2 * ML_W]

    gg = g_ref[...] + gb_ref[...]
    gt = gg.T
    tri = tri_ref[...]
    blk = blk_ref[...]
    lf1, lf2, lf3 = _split3(_log_sigmoid(gg))
    b_col = _dot(tri, lf1) + _dot(tri, lf2) + _dot(tri, lf3)
    bl_col = _dot(blk, lf1) + _dot(blk, lf2) + _dot(blk, lf3)
    lt1, lt2, lt3 = _split3(_log_sigmoid(gt[0:8, :]))
    b_row = _dot_nt(lt1, tri) + _dot_nt(lt2, tri) + _dot_nt(lt3, tri)
    bl_row = _dot_nt(lt1, blk) + _dot_nt(lt2, blk) + _dot_nt(lt3, blk)

    r_i = lax.broadcasted_iota(jnp.int32, (T, T), 0)
    c_i = lax.broadcasted_iota(jnp.int32, (T, T), 1)
    causal = (r_i // L == c_i // L) & (c_i <= r_i)
    row_chunk = lax.broadcasted_iota(jnp.int32, (T, 1), 0) // L
    lane_chunk = lax.broadcasted_iota(jnp.int32, (1, T), 1) // L
    n_chunks = T // L

    for h in range(ML_HEADS):
        cols = slice(h * ML_DQK, (h + 1) * ML_DQK)
        f_lane = ML_HEADS + h
        bc, blc, igc = b_col[:, f_lane:f_lane + 1], bl_col[:, f_lane:f_lane + 1], gg[:, h:h + 1]
        br, blr, igr = b_row[f_lane:f_lane + 1, :], bl_row[f_lane:f_lane + 1, :], gt[h:h + 1, :]
        g_row = blr - br + igr
        g_col = blc - bc + igc

        m_prev = [m_ref[h][:, 0:1]]
        a_old, a_new, m_cs = [], [], []
        for c in range(n_chunks):
            m_c = jnp.max(jnp.where(lane_chunk == c, g_row, NEG), axis=1, keepdims=True)
            bl = blr[:, c * L:c * L + 1]
            m_next = jnp.maximum(bl + m_prev[c], m_c)
            a_old.append(jnp.exp(bl + m_prev[c] - m_next))
            a_new.append(jnp.exp(m_c - m_next))
            m_cs.append(m_c)
            m_prev.append(m_next)
        mprev_col = jnp.zeros((T, 1), F32)
        mc_col = jnp.zeros((T, 1), F32)
        for c in range(n_chunks):
            mprev_col = jnp.where(row_chunk == c, m_prev[c], mprev_col)
            mc_col = jnp.where(row_chunk == c, m_cs[c], mc_col)

        q_c = qf[:, cols]
        k_c = kf[:, cols]
        q_b = q_c.astype(BF16)
        v_b = v_ref[:, cols].astype(BF16)
        d_log = jnp.where(causal, bc - br + igr, NEG)
        a_log = bc + mprev_col
        m_t = jnp.maximum(a_log, jnp.max(d_log, axis=1, keepdims=True))
        wd = jnp.exp(d_log - m_t) * _dot_nt(q_b, k_c.astype(BF16))
        inter = jnp.exp(a_log - m_t)
        intra = _dot(wd.astype(BF16), v_b)
        wsum = jnp.sum(wd, axis=1, keepdims=True)
        kw = k_c * jnp.exp(g_col - mc_col)
        kw_b = kw.astype(BF16)

        c_st = c_ref[h]
        n_st = n_ref[h]
        qc_parts, qn_parts = [], []
        for c in range(n_chunks):
            rows = slice(c * L, (c + 1) * L)
            qc_parts.append(_dot(q_b[rows, :], c_st.astype(BF16)))
            qn_parts.append(jnp.sum(q_c[rows, :] * n_st, axis=1, keepdims=True))
            c_st = a_old[c] * c_st + a_new[c] * _dot_tn(kw_b[rows, :], v_b[rows, :])
            n_st = a_old[c] * n_st + a_new[c] * jnp.sum(kw[rows, :], axis=0, keepdims=True)
        c_ref[h] = c_st
        n_ref[h] = n_st
        m_ref[h] = jnp.broadcast_to(m_prev[n_chunks], (1, LANES))

        num = inter * jnp.concatenate(qc_parts, axis=0) + intra
        den = inter * jnp.concatenate(qn_parts, axis=0) + wsum
        hh = num / jnp.maximum(jnp.abs(den), jnp.exp(-m_t))
        hn = hh * lax.rsqrt(jnp.mean(hh * hh, axis=-1, keepdims=True) + EPS) * gout_ref[h:h + 1, :]
        out_ref[:, cols] = (hn * _sigmoid(o_ref[:, cols])).astype(BF16)


def _mlstm(ml, conv_w, conv_b, gate_b, g_out, bsz, seq):
    T = min(ML_T, seq)
    assert T == ML_T and seq % T == 0
    t_i = np.arange(T)
    blk = t_i[:, None] // ML_CHUNK == t_i[None, :] // ML_CHUNK
    tri = jnp.asarray(blk & (t_i[None, :] <= t_i[:, None]), BF16)
    blk = jnp.asarray(blk, BF16)
    wblk = lambda j: pl.BlockSpec((None, T, ML_W), lambda b, i, j=j: (b, i, j))
    full = lambda shape: pl.BlockSpec(shape, lambda b, i: (0,) * len(shape))
    return pl.pallas_call(
        _mlstm_kernel,
        out_shape=jax.ShapeDtypeStruct((bsz, seq, ML_W), BF16),
        grid=(bsz, seq // T),
        in_specs=[wblk(0), wblk(1), wblk(2), wblk(3),
                  pl.BlockSpec((None, T, LANES), lambda b, i: (b, i, 4 * ML_W // LANES)),
                  full((ML_CONV, 2 * ML_W)), full((1, 2 * ML_W)), full((1, LANES)),
                  full((ML_HEADS, ML_DQK)), full((T, T)), full((T, T))],
        out_specs=pl.BlockSpec((None, T, ML_W), lambda b, i: (b, i, 0)),
        scratch_shapes=[pltpu.VMEM((CONV_PAD + T, 2 * ML_W), F32),
                        pltpu.VMEM((ML_HEADS, ML_DQK, ML_DQK), F32),
                        pltpu.VMEM((ML_HEADS, 1, ML_DQK), F32),
                        pltpu.VMEM((ML_HEADS, 1, LANES), F32)],
        compiler_params=_params(("parallel", "arbitrary")),
        name="mlstm",
    )(ml, ml, ml, ml, ml, conv_w, conv_b, gate_b, g_out, tri, blk)


NSA_TP = 256
NSA_TQ = 256
NSA_OFF_KC, NSA_OFF_VC, NSA_OFF_KS, NSA_OFF_VS, NSA_OFF_KW, NSA_OFF_VW, NSA_OFF_GATE = (
    NSA_W + k * NSA_KV_W for k in range(7))


def _rope_tables(pos):
    half = NSA_DH // 2
    inv_freq = jnp.power(ROPE_THETA, -jnp.arange(half, dtype=F32) / half)
    ang = pos.astype(F32)[:, None] * inv_freq[None, :]
    cos, sin = jnp.cos(ang), jnp.sin(ang)
    cos = jnp.concatenate([cos, cos], axis=1)
    sin = jnp.concatenate([-sin, sin], axis=1)
    return jnp.tile(cos, (1, 2)), jnp.tile(sin, (1, 2))


def _head_norm_rope(xs, gsum, gain, cos, sin):
    a, b, c = _split3(xs * xs)
    ss = _dot(a, gsum) + _dot(b, gsum) + _dot(c, gsum)
    xn = xs * lax.rsqrt(ss * (1.0 / NSA_DH) + EPS) * gain
    lane = lax.broadcasted_iota(jnp.int32, xs.shape, 1)
    first = (lane % NSA_DH) < NSA_DH // 2
    swapped = jnp.where(first, pltpu.roll(xn, LANES - NSA_DH // 2, axis=1), pltpu.roll(xn, NSA_DH // 2, axis=1))
    return xn * cos + swapped * sin


def _head_norm_rope_t(xt, gsum, gain, cos_t, sin_t):
    a, b, c = _split3(xt * xt)
    ss = _dot(gsum, a) + _dot(gsum, b) + _dot(gsum, c)
    xn = xt * lax.rsqrt(ss * (1.0 / NSA_DH) + EPS) * gain
    row = lax.broadcasted_iota(jnp.int32, xt.shape, 0)
    first = (row % NSA_DH) < NSA_DH // 2
    swapped = jnp.where(first, pltpu.roll(xn, LANES - NSA_DH // 2, axis=0), pltpu.roll(xn, NSA_DH // 2, axis=0))
    return xn * cos_t + swapped * sin_t


def _store_keys(out, slab):
    lane = lax.broadcasted_iota(jnp.int32, slab.shape, 1)
    low = lane < NSA_DH
    rolled = pltpu.roll(slab, NSA_DH, axis=1)
    zero = jnp.zeros_like(slab)
    out[0, 0] = jnp.where(low, slab, zero).astype(BF16)
    out[0, 1] = jnp.where(low, zero, rolled).astype(BF16)
    out[1, 0] = jnp.where(low, rolled, zero).astype(BF16)
    out[1, 1] = jnp.where(low, zero, slab).astype(BF16)


def _store_values_t(out, slab):
    st = slab.T
    out[0] = st[0:NSA_DH, :].astype(BF16)
    out[1] = st[NSA_DH:2 * NSA_DH, :].astype(BF16)


def _nsa_prep_kernel(x_ref, cos_ref, sin_ref, cost_ref, sint_ref, gq_ref, gks_ref, gkw_ref, gsum_ref,
                     q_out, ks_out, vs_out, kw_out, vw_out, kc_out, vc_out):
    cos, sin, gsum = cos_ref[...], sin_ref[...], gsum_ref[...]
    for c in range(NSA_W // LANES):
        xt = x_ref[:, c * LANES:(c + 1) * LANES].T
        qn = _head_norm_rope_t(xt, gsum, gq_ref[...], cost_ref[...], sint_ref[...]) * (NSA_DH ** -0.5)
        q_out[c * LANES:(c + 1) * LANES, :] = qn.astype(BF16)
    ks = _head_norm_rope(x_ref[:, NSA_OFF_KS:NSA_OFF_KS + LANES], gsum, gks_ref[...], cos, sin)
    kw = _head_norm_rope(x_ref[:, NSA_OFF_KW:NSA_OFF_KW + LANES], gsum, gkw_ref[...], cos, sin)
    _store_keys(ks_out, ks)
    _store_keys(kw_out, kw)
    _store_values_t(vs_out, x_ref[:, NSA_OFF_VS:NSA_OFF_VS + LANES])
    _store_values_t(vw_out, x_ref[:, NSA_OFF_VW:NSA_OFF_VW + LANES])
    kc_out[...] = x_ref[:, NSA_OFF_KC:NSA_OFF_KC + LANES]
    vc_out[...] = x_ref[:, NSA_OFF_VC:NSA_OFF_VC + LANES]


def _nsa_cmp_kernel(xk_ref, xv_ref, pek_ref, pev_ref, w1ka_ref, w1kb_ref, w1va_ref, w1vb_ref,
                    b1k_ref, b1v_ref, w2k_ref, w2v_ref, gk_ref, cos_ref, sin_ref, gsum_ref,
                    kc_out, vc_out):
    rows = xk_ref.shape[0]

    def hidden(x, pe, wa, wb, b1):
        first = _dot((x + pe[0:1, :]).astype(BF16), wa[...])
        second = _dot((x + pe[1:2, :]).astype(BF16), wb[...])
        hid = first + pltpu.roll(second, rows - 1, axis=0) + b1[...]
        return (hid * _sigmoid(hid)).astype(BF16)

    kc = _dot(hidden(xk_ref[...], pek_ref, w1ka_ref, w1kb_ref, b1k_ref), w2k_ref[...])
    kc = _head_norm_rope(kc, gsum_ref[...], gk_ref[...], cos_ref[...], sin_ref[...])
    _store_keys(kc_out, kc)
    vc_t = _dot_nt(w2v_ref[...], hidden(xv_ref[...], pev_ref, w1va_ref, w1vb_ref, b1v_ref))
    vc_out[0] = vc_t[0:NSA_DH, :].astype(BF16)
    vc_out[1] = vc_t[NSA_DH:2 * NSA_DH, :].astype(BF16)


NSA_HEAD_SLOT = (0, 2, 1, 3)


def _stacked_scores(k_lo, k_hi, q_pair, bias):
    s = jnp.concatenate([_dot(k_lo, q_pair), _dot(k_hi, q_pair)], axis=1)
    return s + jnp.concatenate([bias] * NSA_GROUP, axis=1)


def _online_update(s, v_t, m_ref, l_ref, acc_ref):
    m_old = m_ref[...]
    m_new = jnp.maximum(m_old, jnp.max(s, axis=0, keepdims=True))
    alpha = jnp.exp(m_old - m_new)
    e = jnp.exp(s - jnp.maximum(m_new, 0.1 * NEG))
    l_ref[...] = alpha * l_ref[...] + jnp.sum(e, axis=0, keepdims=True)
    acc_ref[...] = alpha * acc_ref[...] + _dot(v_t, e.astype(BF16))
    m_ref[...] = m_new


def _nsa_attn_kernel(q_ref, kc_ref, vc_ref, ks_ref, vs_ref, kw_ref, vw_ref, gate_ref, mt_ref, esel_ref, eg_ref,
                     out_ref, ms_s, ls_s, accs_s, mw_s, lw_s, accw_s, *, n_cmp, n_sel):
    TQ = NSA_TQ
    i = pl.program_id(2)
    t0 = i * TQ
    t_row = t0 + lax.broadcasted_iota(jnp.int32, (1, TQ), 1)
    q_pair = jnp.concatenate([q_ref[0:LANES, :], q_ref[LANES:2 * LANES, :]], axis=1)

    rows = kc_ref.shape[1]
    n_i = lax.broadcasted_iota(jnp.int32, (rows, 1), 0)
    cmask = (n_i < n_cmp) & (n_i * CMP_STRIDE + (CMP_BLOCK - 1) <= t_row)
    s = _stacked_scores(kc_ref[0], kc_ref[1], q_pair, jnp.where(cmask, 0.0, NEG))
    e = jnp.exp(s - jnp.maximum(jnp.max(s, axis=0, keepdims=True), 0.1 * NEG))
    p = e / jnp.maximum(jnp.sum(e, axis=0, keepdims=True), 1e-30)
    o_cmp_t = _dot(vc_ref[...], p.astype(BF16))
    psum = p[:, 0:TQ] + p[:, TQ:2 * TQ] + p[:, 2 * TQ:3 * TQ] + p[:, 3 * TQ:4 * TQ]
    p1, p2, p3 = _split3(psum)
    imp = _dot(mt_ref[...], p1) + _dot(mt_ref[...], p2) + _dot(mt_ref[...], p3)
    j_i = lax.broadcasted_iota(jnp.int32, (n_sel, TQ), 0)
    cur = (t0 + lax.broadcasted_iota(jnp.int32, (n_sel, TQ), 1)) // SEL_BLOCK
    forced = (j_i == 0) | (j_i == cur) | (j_i == cur - 1)
    score = jnp.where(j_i > cur, NEG, imp[0:n_sel, :] + SEL_FORCE_BONUS * forced.astype(F32))
    rank = jnp.zeros((n_sel, TQ), jnp.int32)
    for jp in range(n_sel):
        row = score[jp:jp + 1, :]
        beats = (row > score) | ((row == score) & (j_i > jp))
        rank = rank + beats.astype(jnp.int32)
    sel_t = (rank < min(SEL_TOPN, n_sel)).astype(F32)
    sel_b = jnp.concatenate([sel_t, jnp.zeros((LANES - n_sel, TQ), F32)], axis=0).astype(BF16)

    for m_ref, l_ref, acc_ref in ((ms_s, ls_s, accs_s), (mw_s, lw_s, accw_s)):
        m_ref[...] = jnp.full(m_ref.shape, NEG, F32)
        l_ref[...] = jnp.zeros(l_ref.shape, F32)
        acc_ref[...] = jnp.zeros(acc_ref.shape, F32)

    def key_tile(jt, carry):
        k0 = pl.multiple_of(jt * TQ, TQ)
        diff = t_row - (k0 + lax.broadcasted_iota(jnp.int32, (TQ, 1), 0))
        causal = diff >= 0
        smask = (_dot(esel_ref[jt], sel_b) > 0.5) & causal
        s = _stacked_scores(ks_ref[0, pl.ds(k0, TQ), :], ks_ref[1, pl.ds(k0, TQ), :], q_pair,
                            jnp.where(smask, 0.0, NEG))
        _online_update(s, vs_ref[jt], ms_s, ls_s, accs_s)

        @pl.when(jt * TQ + TQ > t0 - WINDOW + 1)
        def _():
            wmask = causal & (diff < WINDOW)
            sw = _stacked_scores(kw_ref[0, pl.ds(k0, TQ), :], kw_ref[1, pl.ds(k0, TQ), :], q_pair,
                                 jnp.where(wmask, 0.0, NEG))
            _online_update(sw, vw_ref[jt], mw_s, lw_s, accw_s)
        return carry

    lax.fori_loop(0, i + 1, key_tile, 0)

    g1, g2, g3 = _split3(_sigmoid(gate_ref[...]))
    gexp = _dot(g1, eg_ref[...]) + _dot(g2, eg_ref[...]) + _dot(g3, eg_ref[...])
    half_w = NSA_GROUP * NSA_DH
    o_sel_t = accs_s[...] / jnp.maximum(ls_s[...], 1e-30)
    o_win_t = accw_s[...] / jnp.maximum(lw_s[...], 1e-30)

    def token_major(o_t, c):
        ce, co = NSA_HEAD_SLOT[2 * c] * TQ, NSA_HEAD_SLOT[2 * c + 1] * TQ
        return jnp.concatenate([o_t[:, ce:ce + TQ], o_t[:, co:co + TQ]], axis=0).T

    for c in range(NSA_GROUP // 2):
        o_cmp = token_major(o_cmp_t, c)
        o_sel = token_major(o_sel_t, c)
        o_win = token_major(o_win_t, c)
        cs = slice(c * LANES, (c + 1) * LANES)
        o = (gexp[:, c * LANES:(c + 1) * LANES] * o_cmp
             + gexp[:, half_w + c * LANES:half_w + (c + 1) * LANES] * o_sel
             + gexp[:, 2 * half_w + c * LANES:2 * half_w + (c + 1) * LANES] * o_win)
        out_ref[:, cs] = o.astype(BF16)


def _nsa_constants(seq):
    n_cmp = (seq - CMP_BLOCK) // CMP_STRIDE + 1
    n_sel = seq // SEL_BLOCK
    rows = seq // CMP_STRIDE
    n = np.arange(rows)[None, :]
    j = np.arange(LANES)[:, None]
    mt = ((n * CMP_STRIDE < j * SEL_BLOCK + SEL_BLOCK) & (n * CMP_STRIDE + CMP_BLOCK > j * SEL_BLOCK)
          & (n < n_cmp) & (j < n_sel))
    s = np.arange(seq)
    esel = (s[:, None] // SEL_BLOCK == np.arange(LANES)[None, :])
    esel = esel.reshape(seq // NSA_TQ, NSA_TQ, LANES)
    eg = np.zeros((NSA_KVH, LANES, N_BRANCH * NSA_GROUP * NSA_DH), np.float32)
    for h in range(NSA_KVH):
        for g in range(NSA_GROUP):
            for br in range(N_BRANCH):
                lane = (h * NSA_GROUP + g) * N_BRANCH + br
                c0 = br * NSA_GROUP * NSA_DH + g * NSA_DH
                eg[h, lane, c0:c0 + NSA_DH] = 1.0
    gsum = np.kron(np.eye(LANES // NSA_DH), np.ones((NSA_DH, NSA_DH)))
    return (n_cmp, n_sel, jnp.asarray(mt, BF16), jnp.asarray(esel, BF16), jnp.asarray(eg, BF16),
            jnp.asarray(gsum, BF16))


def _nsa(nsa, w, bsz, seq):
    assert seq % NSA_TQ == 0 and seq % NSA_TP == 0
    n_cmp, n_sel, mt, esel, eg, gsum = _nsa_constants(seq)
    rows = seq // CMP_STRIDE
    cos, sin = _rope_tables(jnp.arange(seq))
    cos_c, sin_c = _rope_tables(jnp.arange(rows) * CMP_STRIDE + (CMP_BLOCK - 1))
    c2 = lambda shape: pl.BlockSpec(shape, lambda b, i: (0,) * len(shape))
    n_tiles = seq // NSA_TP
    k_shape = jax.ShapeDtypeStruct((bsz, NSA_KVH, 2, seq, LANES), BF16)
    k_spec = pl.BlockSpec((None, NSA_KVH, 2, NSA_TP, LANES), lambda b, i: (b, 0, 0, i, 0))
    v_shape = jax.ShapeDtypeStruct((bsz, NSA_KVH, n_tiles, NSA_DH, NSA_TP), BF16)
    v_spec = pl.BlockSpec((None, NSA_KVH, None, NSA_DH, NSA_TP), lambda b, i: (b, 0, i, 0, 0))
    raw_shape = jax.ShapeDtypeStruct((bsz, seq, LANES), F32)
    raw_spec = pl.BlockSpec((None, NSA_TP, LANES), lambda b, i: (b, i, 0))
    qn, ksd, vsd, kwd, vwd, kc_raw, vc_raw = pl.pallas_call(
        _nsa_prep_kernel,
        out_shape=(jax.ShapeDtypeStruct((bsz, NSA_W, seq), BF16), k_shape, v_shape, k_shape, v_shape,
                   raw_shape, raw_shape),
        grid=(bsz, n_tiles),
        in_specs=[pl.BlockSpec((None, NSA_TP, NSA_COLS), lambda b, i: (b, i, 0)),
                  pl.BlockSpec((NSA_TP, LANES), lambda b, i: (i, 0)),
                  pl.BlockSpec((NSA_TP, LANES), lambda b, i: (i, 0)),
                  pl.BlockSpec((LANES, NSA_TP), lambda b, i: (0, i)),
                  pl.BlockSpec((LANES, NSA_TP), lambda b, i: (0, i)),
                  c2((LANES, 1)), c2((1, LANES)), c2((1, LANES)), c2((LANES, LANES))],
        out_specs=(pl.BlockSpec((None, NSA_W, NSA_TP), lambda b, i: (b, 0, i)),
                   k_spec, v_spec, k_spec, v_spec, raw_spec, raw_spec),
        compiler_params=_params(("parallel", "parallel")),
        name="nsa_prep",
    )(nsa, cos, sin, cos.T, sin.T, w["nsa_gq"].reshape(LANES, 1), w["nsa_gks"], w["nsa_gkw"], gsum)

    wide = CMP_STRIDE * LANES
    xk = kc_raw.reshape(bsz, rows, wide)
    xv = vc_raw.reshape(bsz, rows, wide)
    c1 = lambda shape: pl.BlockSpec(shape, lambda b: (0,) * len(shape))
    kc_shape = jax.ShapeDtypeStruct((bsz, NSA_KVH, 2, rows, LANES), BF16)
    kc_spec = pl.BlockSpec((None, NSA_KVH, 2, rows, LANES), lambda b: (b, 0, 0, 0, 0))
    vc_shape = jax.ShapeDtypeStruct((bsz, NSA_KVH, NSA_DH, rows), BF16)
    vc_spec = pl.BlockSpec((None, NSA_KVH, NSA_DH, rows), lambda b: (b, 0, 0, 0))
    hid2 = NSA_KVH * CMP_HIDDEN
    kcd, vcd = pl.pallas_call(
        _nsa_cmp_kernel,
        out_shape=(kc_shape, vc_shape),
        grid=(bsz,),
        in_specs=[pl.BlockSpec((None, rows, wide), lambda b: (b, 0, 0)),
                  pl.BlockSpec((None, rows, wide), lambda b: (b, 0, 0)),
                  c1((2, wide)), c1((2, wide)),
                  c1((wide, hid2)), c1((wide, hid2)), c1((wide, hid2)), c1((wide, hid2)),
                  c1((1, hid2)), c1((1, hid2)), c1((hid2, LANES)), c1((LANES, hid2)),
                  c1((1, LANES)), c1((rows, LANES)), c1((rows, LANES)), c1((LANES, LANES))],
        out_specs=(kc_spec, vc_spec),
        compiler_params=_params(("parallel",)),
        name="nsa_compress",
    )(xk, xv, w["cmp_pe_k"], w["cmp_pe_v"], w["cmp_w1k_a"], w["cmp_w1k_b"], w["cmp_w1v_a"], w["cmp_w1v_b"],
      w["cmp_b1_k"], w["cmp_b1_v"], w["cmp_w2_k"], w["cmp_w2_v"].T, w["nsa_gkc"], cos_c, sin_c, gsum)

    TQ = NSA_TQ
    half_w = NSA_GROUP * NSA_DH
    c3 = lambda shape: pl.BlockSpec(shape, lambda b, h, i: (0,) * len(shape))
    assert NSA_TP == TQ
    keys = lambda r: pl.BlockSpec((None, None, 2, r, LANES), lambda b, h, i: (b, h, 0, 0, 0))
    vals = pl.BlockSpec((None, None, seq // TQ, NSA_DH, TQ), lambda b, h, i: (b, h, 0, 0, 0))
    stat = lambda: pltpu.VMEM((1, NSA_GROUP * TQ), F32)
    accu = lambda: pltpu.VMEM((NSA_DH, NSA_GROUP * TQ), F32)
    return pl.pallas_call(
        functools.partial(_nsa_attn_kernel, n_cmp=n_cmp, n_sel=n_sel),
        out_shape=jax.ShapeDtypeStruct((bsz, seq, NSA_W), BF16),
        grid=(bsz, NSA_KVH, seq // TQ),
        in_specs=[pl.BlockSpec((None, half_w, TQ), lambda b, h, i: (b, h, i)),
                  keys(rows), pl.BlockSpec((None, None, NSA_DH, rows), lambda b, h, i: (b, h, 0, 0)),
                  keys(seq), vals, keys(seq), vals,
                  pl.BlockSpec((None, TQ, LANES), lambda b, h, i: (b, i, NSA_OFF_GATE // LANES)),
                  c3((LANES, rows)), c3((seq // TQ, TQ, LANES)),
                  pl.BlockSpec((None, LANES, N_BRANCH * half_w), lambda b, h, i: (h, 0, 0))],
        out_specs=pl.BlockSpec((None, TQ, half_w), lambda b, h, i: (b, i, h)),
        scratch_shapes=[stat(), stat(), accu(), stat(), stat(), accu()],
        compiler_params=_params(("parallel", "parallel", "arbitrary")),
        name="nsa_attention",
    )(qn, kcd, vcd, ksd, vsd, kwd, vwd, nsa, mt, esel, eg)


POST_TM = 256


def _rms(v, gain):
    return v * lax.rsqrt(jnp.mean(v * v, axis=-1, keepdims=True) + EPS) * gain


def _memkv_kernel(mem_ref, g_ref, w_ref, gk_ref, k_out, v_out):
    u = _rms(mem_ref[...], g_ref[...]).astype(BF16)
    kv = _dot(u, w_ref[...])
    for h in range(MEM_HEADS):
        cs = slice(h * MEM_DH, (h + 1) * MEM_DH)
        k_out[:, cs] = _rms(kv[:, cs], gk_ref[...]).astype(BF16)
    v_out[...] = kv[:, MEM_W:2 * MEM_W].astype(BF16)


def _memkv(mem, g_mem, w_mem_kv, g_k):
    bsz, m_len, _ = mem.shape
    c1 = lambda shape: pl.BlockSpec(shape, lambda b: (0,) * len(shape))
    shape = jax.ShapeDtypeStruct((bsz, m_len, MEM_W), BF16)
    spec = pl.BlockSpec((None, m_len, MEM_W), lambda b: (b, 0, 0))
    return pl.pallas_call(
        _memkv_kernel, out_shape=(shape, shape), grid=(bsz,),
        in_specs=[pl.BlockSpec((None, m_len, D_MODEL), lambda b: (b, 0, 0)),
                  c1((1, D_MODEL)), c1((D_MODEL, 2 * MEM_W)), c1((1, MEM_DH))],
        out_specs=(spec, spec),
        compiler_params=_params(("parallel",)),
        name="mem_kv",
    )(mem, g_mem, w_mem_kv, g_k)


def _post_kernel(x_ref, hml_ref, hnsa_ref, mk_ref, mv_ref, gmix_ref, wmq_ref, gmq_ref, wgate_ref,
                 wml_ref, wnsa_ref, wmem_ref, wout_ref, gffn_ref, wrh_ref, wrl_ref, br_ref,
                 x1_out, t_out, logit_out):
    x = x_ref[...]
    ub = _rms(x, gmix_ref[...]).astype(BF16)
    mq = _dot(ub, wmq_ref[...])
    heads = []
    for h in range(MEM_HEADS):
        cs = slice(h * MEM_DH, (h + 1) * MEM_DH)
        qn = (_rms(mq[:, cs], gmq_ref[...]) * (MEM_DH ** -0.5)).astype(BF16)
        s = _dot_nt(qn, mk_ref[:, cs])
        e = jnp.exp(s - jnp.max(s, axis=1, keepdims=True))
        p = e / jnp.sum(e, axis=1, keepdims=True)
        heads.append(_dot(p.astype(BF16), mv_ref[:, cs]))
    hmem = jnp.concatenate(heads, axis=1).astype(BF16)
    gates = _sigmoid(_dot(ub, wgate_ref[...]))
    merged = (gates[:, 0:D_MODEL] * _dot(hml_ref[...], wml_ref[...])
              + gates[:, D_MODEL:2 * D_MODEL] * _dot(hnsa_ref[...], wnsa_ref[...])
              + gates[:, 2 * D_MODEL:3 * D_MODEL] * _dot(hmem, wmem_ref[...]))
    x1 = x + _dot(merged.astype(BF16), wout_ref[...])
    x1_out[...] = x1
    t = _rms(x1, gffn_ref[...])
    t_out[...] = t
    t1, t2, _ = _split3(t)
    logit_out[...] = _dot(t1, wrh_ref[...]) + _dot(t1, wrl_ref[...]) + _dot(t2, wrh_ref[...]) + br_ref[...]


def _post(x2, hml, hnsa, mk, mv, w, seq):
    n = x2.shape[0]
    tm = POST_TM
    assert seq % tm == 0
    per = seq // tm
    m_len = mk.shape[1]
    row = lambda width: pl.BlockSpec((tm, width), lambda i: (i, 0))
    c1 = lambda shape: pl.BlockSpec(shape, lambda i: (0,) * len(shape))
    memspec = pl.BlockSpec((None, m_len, MEM_W), lambda i: (i // per, 0, 0))
    return pl.pallas_call(
        _post_kernel,
        out_shape=(jax.ShapeDtypeStruct((n, D_MODEL), F32), jax.ShapeDtypeStruct((n, D_MODEL), F32),
                   jax.ShapeDtypeStruct((n, LANES), F32)),
        grid=(n // tm,),
        in_specs=[row(D_MODEL), row(ML_W), row(NSA_W), memspec, memspec,
                  c1((1, D_MODEL)), c1((D_MODEL, MEM_W)), c1((1, MEM_DH)), c1((D_MODEL, N_BRANCH * D_MODEL)),
                  c1((ML_W, D_MODEL)), c1((NSA_W, D_MODEL)), c1((MEM_W, D_MODEL)), c1((D_MODEL, D_MODEL)),
                  c1((1, D_MODEL)), c1((D_MODEL, LANES)), c1((D_MODEL, LANES)), c1((1, LANES))],
        out_specs=(row(D_MODEL), row(D_MODEL), row(LANES)),
        compiler_params=_params(("parallel",)),
        name="post",
    )(x2, hml, hnsa, mk, mv, w["g_mix"], w["w_mq"], w["mem_gq"], w["w_gate"], w["w_br_ml"], w["w_br_nsa"],
      w["w_br_mem"], w["w_out"], w["g_ffn"], w["w_router_hi"], w["w_router_lo"], w["b_router"])


ROUTE_T = 512
MOE_BLK = 256
MOE_TD = 256


def _route_kernel(logit_ref, e_out, wtm_out, cnt_out):
    T = ROUTE_T

    @pl.when(pl.program_id(0) == 0)
    def _():
        cnt_out[...] = jnp.zeros(cnt_out.shape, F32)

    sc = logit_ref[...].T[0:N_EXPERTS, :]
    j_i = lax.broadcasted_iota(jnp.int32, (N_EXPERTS, T), 0)
    rank = jnp.zeros((N_EXPERTS, T), jnp.int32)
    for jp in range(N_EXPERTS):
        row = sc[jp:jp + 1, :]
        beats = (row > sc) | ((row == sc) & (j_i > jp))
        rank = rank + beats.astype(jnp.int32)
    sel = rank < TOP_K
    e = jnp.where(sel, jnp.exp(sc - jnp.max(sc, axis=0, keepdims=True)), 0.0)
    wgt = e / jnp.sum(e, axis=0, keepdims=True)
    cnt_out[...] = cnt_out[...] + jnp.sum(sel.astype(F32), axis=1, keepdims=True)
    e_rows, w_rows = [], []
    for r in range(TOP_K):
        hit = rank == r
        e_rows.append(jnp.sum(jnp.where(hit, j_i, 0), axis=0, keepdims=True))
        w_rows.append(jnp.sum(jnp.where(hit, wgt, 0.0), axis=0, keepdims=True))
    e_out[...] = jnp.concatenate(e_rows, axis=0)
    w4 = jnp.concatenate(w_rows, axis=0)
    wtm_out[...] = jnp.concatenate([w4, jnp.zeros((LANES - TOP_K, T), F32)], axis=0).T


def _route(logits):
    n = logits.shape[0]
    T = ROUTE_T
    assert n % T == 0
    return pl.pallas_call(
        _route_kernel,
        out_shape=(jax.ShapeDtypeStruct((TOP_K, n), jnp.int32), jax.ShapeDtypeStruct((n, LANES), F32),
                   jax.ShapeDtypeStruct((N_EXPERTS, LANES), F32)),
        grid=(n // T,),
        in_specs=[pl.BlockSpec((T, LANES), lambda i: (i, 0))],
        out_specs=(pl.BlockSpec((TOP_K, T), lambda i: (0, i)), pl.BlockSpec((T, LANES), lambda i: (i, 0)),
                   pl.BlockSpec((N_EXPERTS, LANES), lambda i: (0, 0))),
        compiler_params=_params(("arbitrary",)),
        name="moe_route",
    )(logits)


def _ffn_fused_kernel(be_ref, base_ref, nv_ref, order_ref, t_hbm, wgu_ref, bgu_ref, wd_ref, bd_ref, yk_hbm,
                      xbuf, ybuf, gsem, ssem, wgu_s, wd_s, *, n_tok):
    i = pl.program_id(0)
    nb = pl.num_programs(0)
    slot = i % 2
    other = 1 - slot
    last_id = n_tok * TOP_K - 1
    dump0 = n_tok * TOP_K

    def start_gather(blk, buf):
        b0 = base_ref[blk]
        for j in range(MOE_BLK):
            tok = lax.shift_right_logical(order_ref[jnp.minimum(b0 + j, last_id)], 2)
            pltpu.make_async_copy(t_hbm.at[pl.ds(tok, 1)], xbuf.at[buf, pl.ds(j, 1)], gsem.at[buf]).start()

    def start_scatter(blk, nv, buf):
        b0 = base_ref[blk]
        for j in range(MOE_BLK):
            dst = jnp.where(j < nv, order_ref[jnp.minimum(b0 + j, last_id)], dump0 + buf * MOE_BLK + j)
            pltpu.make_async_copy(ybuf.at[buf, pl.ds(j, 1)], yk_hbm.at[pl.ds(dst, 1)], ssem.at[buf]).start()

    def wait_gather(buf):
        pltpu.make_async_copy(t_hbm.at[pl.ds(0, MOE_BLK)], xbuf.at[buf], gsem.at[buf]).wait()

    def wait_scatter(buf):
        pltpu.make_async_copy(ybuf.at[buf], yk_hbm.at[pl.ds(0, MOE_BLK)], ssem.at[buf]).wait()

    @pl.when(i == 0)
    def _():
        xbuf[...] = jnp.zeros(xbuf.shape, F32)
        ybuf[...] = jnp.zeros(ybuf.shape, F32)
        start_gather(0, 0)
        start_scatter(0, 0, 0)

    @pl.when((i == 0) | (be_ref[i] != be_ref[jnp.maximum(i - 1, 0)]))
    def _():
        wgu_s[...] = wgu_ref[...].astype(BF16)
        bits = lax.shift_right_logical(pltpu.bitcast(wd_ref[...].astype(BF16).astype(F32), jnp.uint32),
                                       jnp.uint32(16))
        wd_s[...] = bits | lax.shift_left(bits, jnp.uint32(16))

    for par in range(2):
        @pl.when(slot == par)
        def _():
            wait_gather(par)
            start_gather(jnp.minimum(i + 1, nb - 1), 1 - par)
            prev = jnp.maximum(i - 1, 0)
            start_scatter(prev, jnp.where(i == 0, 0, nv_ref[prev]), 1 - par)
            hgu = _dot(xbuf[par].astype(BF16), wgu_s[...]) + bgu_ref[...]
            gate = jnp.minimum(hgu, SWIGLU_LIMIT)
            up1 = jnp.clip(hgu, -SWIGLU_LIMIT, SWIGLU_LIMIT) + 1.0
            even = lax.broadcasted_iota(jnp.int32, (1, 2 * D_FF), 1) % 2 == 0
            act = jnp.where(even, gate * _sigmoid(SWIGLU_ALPHA * gate) * pltpu.roll(up1, 2 * D_FF - 1, axis=1), 0.0)
            y = _dot(act.astype(BF16), pltpu.bitcast(wd_s[...], BF16)) + bd_ref[...]
            wait_scatter(par)
            ybuf[par] = y

            @pl.when(i == nb - 1)
            def _():
                start_scatter(i, nv_ref[i], par)
                wait_gather(1 - par)
                wait_scatter(1 - par)
                wait_scatter(par)


def _ffn_fused(t, blk_e, base, n_valid, order, w):
    n = t.shape[0]
    nb = blk_e.shape[0]
    espec = lambda r, c: pl.BlockSpec((None, r, c), lambda i, be, *_: (be[i], 0, 0))
    return pl.pallas_call(
        functools.partial(_ffn_fused_kernel, n_tok=n),
        out_shape=jax.ShapeDtypeStruct((n * TOP_K + 2 * MOE_BLK, D_MODEL), F32),
        grid_spec=pltpu.PrefetchScalarGridSpec(
            num_scalar_prefetch=4, grid=(nb,),
            in_specs=[pl.BlockSpec(memory_space=pl.ANY), espec(D_MODEL, 2 * D_FF), espec(1, 2 * D_FF),
                      espec(D_FF, D_MODEL), espec(1, D_MODEL)],
            out_specs=pl.BlockSpec(memory_space=pl.ANY),
            scratch_shapes=[pltpu.VMEM((2, MOE_BLK, D_MODEL), F32), pltpu.VMEM((2, MOE_BLK, D_MODEL), F32),
                            pltpu.SemaphoreType.DMA((2,)), pltpu.SemaphoreType.DMA((2,)),
                            pltpu.VMEM((D_MODEL, 2 * D_FF), BF16), pltpu.VMEM((D_FF, D_MODEL), jnp.uint32)]),
        compiler_params=_params(("arbitrary",)),
        name="moe_ffn",
    )(blk_e, base, n_valid, order, t, w["w_gu"], w["b_gu"], w["w_down"], w["b_down"])


def _moe_sum_kernel(x1_ref, wtm_ref, y_ref, out_ref):
    wtm = wtm_ref[...]
    acc = x1_ref[...]
    for r in range(TOP_K):
        acc = acc + wtm[:, r:r + 1] * y_ref[:, r * D_MODEL:(r + 1) * D_MODEL]
    out_ref[...] = acc


def _moe_sum(x1, wtm, yk):
    n = x1.shape[0]
    y4 = yk.reshape(-1, TOP_K * D_MODEL)
    row = lambda width: pl.BlockSpec((MOE_TD, width), lambda i: (i, 0))
    return pl.pallas_call(
        _moe_sum_kernel,
        out_shape=jax.ShapeDtypeStruct((n, D_MODEL), F32),
        grid=(n // MOE_TD,),
        in_specs=[row(D_MODEL), row(LANES), row(TOP_K * D_MODEL)],
        out_specs=row(D_MODEL),
        compiler_params=_params(("parallel",)),
        name="moe_sum",
    )(x1, wtm, y4)


def _moe(x1, t, logits, w):
    n = x1.shape[0]
    e_idx, wtm, cnt = _route(logits)
    counts = cnt[:, 0].astype(jnp.int32)
    padded = (counts + MOE_BLK - 1) // MOE_BLK * MOE_BLK
    pad_end = jnp.cumsum(padded)
    pad_start = pad_end - padded
    start = jnp.cumsum(counts) - counts
    nb = -(-(n * TOP_K + N_EXPERTS * (MOE_BLK - 1)) // MOE_BLK)
    first = jnp.arange(nb, dtype=jnp.int32) * MOE_BLK
    blk_e = jnp.minimum(jnp.sum((pad_end[None, :] <= first[:, None]).astype(jnp.int32), axis=1), N_EXPERTS - 1)
    within = first - pad_start[blk_e]
    n_valid = jnp.clip(counts[blk_e] - within, 0, MOE_BLK).astype(jnp.int32)
    base = (start[blk_e] + within).astype(jnp.int32)
    order = jnp.argsort(e_idx.T.reshape(-1), stable=True).astype(jnp.int32)
    yk = _ffn_fused(t, blk_e, base, n_valid, order, w)
    return _moe_sum(x1, wtm, yk)


def _cmp_weights(pos, w1, b1, w2):
    eye = jnp.eye(NSA_KVH, dtype=F32)
    halves = []
    for part in (w1[:CMP_STRIDE], w1[CMP_STRIDE:]):
        big = jnp.einsum("ldf,hg->lhdgf", part, eye)
        halves.append(big.reshape(CMP_STRIDE * LANES, NSA_KVH * CMP_HIDDEN).astype(BF16))
    pe = jnp.stack([jnp.broadcast_to(part[:, None, :], (CMP_STRIDE, NSA_KVH, NSA_DH)).reshape(-1)
                    for part in (pos[:CMP_STRIDE], pos[CMP_STRIDE:])])
    w2big = jnp.einsum("fd,hg->hfgd", w2, eye).reshape(NSA_KVH * CMP_HIDDEN, LANES).astype(BF16)
    return halves[0], halves[1], pe, jnp.tile(b1, NSA_KVH).reshape(1, -1), w2big


def _prep_weights(p):
    w = {}
    offs = np.concatenate([[0], np.cumsum(IN_SPLITS)])
    col = lambda k: p["w_in"][:, offs[k]:offs[k + 1]]
    zpad = lambda n: jnp.zeros((D_MODEL, n), F32)
    w["w_ml"] = jnp.concatenate([col(0), col(1), col(2), col(3), col(4), col(5),
                                 zpad(LANES - 2 * ML_HEADS)], axis=1).astype(BF16)
    w["w_nsa"] = jnp.concatenate([col(k) for k in range(6, 14)] + [zpad(LANES - NSA_QH * 3)], axis=1).astype(BF16)
    w["g_mix"] = p["g_mix"].reshape(1, D_MODEL)
    w["conv_w"] = p["ml_conv_w"]
    w["conv_b"] = p["ml_conv_b"].reshape(1, 2 * ML_W)
    w["ml_gate_b"] = jnp.concatenate([p["ml_b_i"], p["ml_b_f"], jnp.zeros((LANES - 2 * ML_HEADS,), F32)]).reshape(1, LANES)
    w["ml_g_out"] = p["ml_g_out"]

    pair = lambda g: jnp.tile(g, LANES // NSA_DH).reshape(1, LANES)
    w["nsa_gq"] = pair(p["nsa_g_q"])
    w["nsa_gkc"], w["nsa_gks"], w["nsa_gkw"] = (pair(p["nsa_g_k"][k]) for k in range(3))
    (w["cmp_w1k_a"], w["cmp_w1k_b"], w["cmp_pe_k"], w["cmp_b1_k"], w["cmp_w2_k"]) = _cmp_weights(
        p["cmp_pos_k"], p["cmp_w1_k"], p["cmp_b1_k"], p["cmp_w2_k"])
    (w["cmp_w1v_a"], w["cmp_w1v_b"], w["cmp_pe_v"], w["cmp_b1_v"], w["cmp_w2_v"]) = _cmp_weights(
        p["cmp_pos_v"], p["cmp_w1_v"], p["cmp_b1_v"], p["cmp_w2_v"])

    w["g_mem"] = p["g_mem"].reshape(1, D_MODEL)
    w["w_mem_kv"] = p["w_mem_kv"].astype(BF16)
    w["mem_gq"] = p["mem_g_q"].reshape(1, MEM_DH)
    w["mem_gk"] = p["mem_g_k"].reshape(1, MEM_DH)
    w["w_mq"] = col(14).astype(BF16)
    w["w_gate"] = col(15).astype(BF16)
    for k in ("w_br_ml", "w_br_nsa", "w_br_mem", "w_out"):
        w[k] = p[k].astype(BF16)
    w["g_ffn"] = p["g_ffn"].reshape(1, D_MODEL)
    w_r = jnp.concatenate([p["w_router"], jnp.zeros((D_MODEL, LANES - N_EXPERTS), F32)], axis=1)
    w["w_router_hi"] = w_r.astype(BF16)
    w["w_router_lo"] = (w_r - w["w_router_hi"].astype(F32)).astype(BF16)
    w["b_router"] = jnp.concatenate([p["b_router"], jnp.zeros((LANES - N_EXPERTS,), F32)]).reshape(1, LANES)
    w["w_gu"] = p["w_gu"]
    w["b_gu"] = p["b_gu"][:, None, :]
    w["w_down"] = p["w_down"]
    w["b_down"] = p["b_down"][:, None, :]
    return w


_PARAM_NAMES = ("g_mix", "w_in", "ml_conv_w", "ml_conv_b", "ml_b_i", "ml_b_f", "ml_g_out", "nsa_g_q", "nsa_g_k",
                "cmp_pos_k", "cmp_w1_k", "cmp_b1_k", "cmp_w2_k", "cmp_pos_v", "cmp_w1_v", "cmp_b1_v", "cmp_w2_v",
                "g_mem", "w_mem_kv", "mem_g_q", "mem_g_k", "w_br_ml", "w_br_nsa", "w_br_mem", "w_out", "g_ffn",
                "w_router", "b_router", "w_gu", "b_gu", "w_down", "b_down")


def _layer(x, mem, p):
    bsz, seq, d = x.shape
    w = _prep_weights(p)
    x2 = x.reshape(bsz * seq, d)
    ml, nsa = _proj(x2, w["g_mix"], w["w_ml"], w["w_nsa"])
    h_ml = _mlstm(ml.reshape(bsz, seq, ML_COLS), w["conv_w"], w["conv_b"], w["ml_gate_b"], w["ml_g_out"], bsz, seq)
    h_nsa = _nsa(nsa.reshape(bsz, seq, NSA_COLS), w, bsz, seq)
    mk, mv = _memkv(mem, w["g_mem"], w["w_mem_kv"], w["mem_gk"])
    x1, t, logits = _post(x2, h_ml.reshape(bsz * seq, ML_W), h_nsa.reshape(bsz * seq, NSA_W), mk, mv, w, seq)
    return _moe(x1, t, logits, w).reshape(bsz, seq, d)


def kernel(x, mem, g_mix, w_in, ml_conv_w, ml_conv_b, ml_b_i, ml_b_f, ml_g_out, nsa_g_q, nsa_g_k, cmp_pos_k,
           cmp_w1_k, cmp_b1_k, cmp_w2_k, cmp_pos_v, cmp_w1_v, cmp_b1_v, cmp_w2_v, g_mem, w_mem_kv, mem_g_q,
           mem_g_k, w_br_ml, w_br_nsa, w_br_mem, w_out, g_ffn, w_router, b_router, w_gu, b_gu, w_down, b_down):
    stacked = (g_mix, w_in, ml_conv_w, ml_conv_b, ml_b_i, ml_b_f, ml_g_out, nsa_g_q, nsa_g_k, cmp_pos_k,
               cmp_w1_k, cmp_b1_k, cmp_w2_k, cmp_pos_v, cmp_w1_v, cmp_b1_v, cmp_w2_v, g_mem, w_mem_kv, mem_g_q,
               mem_g_k, w_br_ml, w_br_nsa, w_br_mem, w_out, g_ffn, w_router, b_router, w_gu, b_gu, w_down, b_down)
    h = x
    for layer in range(g_mix.shape[0]):
        h = _layer(h, mem, {name: v[layer] for name, v in zip(_PARAM_NAMES, stacked)})
    return h
```

```python
import functools

import jax
import jax.numpy as jnp
import numpy as np
from jax import lax
from jax.experimental import pallas as pl
from jax.experimental.pallas import tpu as pltpu

F32 = jnp.float32
BF16 = jnp.bfloat16

D_MODEL = 1024
ML_HEADS = 4
ML_DQK = 128
ML_W = ML_HEADS * ML_DQK
ML_CHUNK = 64
ML_CONV = 4
NSA_QH = 8
NSA_KVH = 2
NSA_DH = 64
NSA_GROUP = NSA_QH // NSA_KVH
NSA_W = NSA_QH * NSA_DH
NSA_KV_W = NSA_KVH * NSA_DH
CMP_BLOCK = 32
CMP_STRIDE = 16
CMP_HIDDEN = 256
SEL_BLOCK = 64
SEL_TOPN = 8
SEL_FORCE_BONUS = 1e4
WINDOW = 512
MEM_HEADS = 4
MEM_DH = 128
MEM_W = MEM_HEADS * MEM_DH
N_EXPERTS = 32
TOP_K = 4
D_FF = D_MODEL
SWIGLU_ALPHA = 1.702
SWIGLU_LIMIT = 7.0
N_BRANCH = 3
ROPE_THETA = 10000.0
EPS = 1e-6
NEG = -1e30

LANES = 128
VMEM_LIMIT = 56 * 1024 * 1024

IN_SPLITS = (ML_W, ML_W, ML_W, ML_W, ML_HEADS, ML_HEADS,
             NSA_W, NSA_KV_W, NSA_KV_W, NSA_KV_W, NSA_KV_W, NSA_KV_W, NSA_KV_W, NSA_QH * 3,
             MEM_W, N_BRANCH * D_MODEL)

ML_COLS = 4 * ML_W + LANES
NSA_COLS = NSA_W + 6 * NSA_KV_W + LANES


def _dot(a, b):
    return jnp.dot(a, b, preferred_element_type=F32)


def _dot_nt(a, b):
    return lax.dot_general(a, b, (((1,), (1,)), ((), ())), preferred_element_type=F32)


def _dot_tn(a, b):
    return lax.dot_general(a, b, (((0,), (0,)), ((), ())), preferred_element_type=F32)


def _split3(v):
    a = v.astype(BF16)
    r = v - a.astype(F32)
    b = r.astype(BF16)
    c = (r - b.astype(F32)).astype(BF16)
    return a, b, c


def _sigmoid(v):
    return 1.0 / (1.0 + jnp.exp(-v))


def _log_sigmoid(v):
    return jnp.minimum(v, 0.0) - jnp.log1p(jnp.exp(-jnp.abs(v)))


def _params(sem):
    return pltpu.CompilerParams(dimension_semantics=sem, vmem_limit_bytes=VMEM_LIMIT)


PROJ_TM = 512


def _proj_kernel(x_ref, g_ref, wml_ref, wnsa_ref, oml_ref, onsa_ref):
    x = x_ref[...]
    u = x * lax.rsqrt(jnp.mean(x * x, axis=-1, keepdims=True) + EPS) * g_ref[...]
    ub = u.astype(BF16)
    oml_ref[...] = _dot(ub, wml_ref[...])
    onsa_ref[...] = _dot(ub, wnsa_ref[...])


def _proj(x2, g_mix, w_ml, w_nsa):
    n = x2.shape[0]
    tm = min(PROJ_TM, n)
    return pl.pallas_call(
        _proj_kernel,
        out_shape=(jax.ShapeDtypeStruct((n, ML_COLS), F32), jax.ShapeDtypeStruct((n, NSA_COLS), F32)),
        grid=(n // tm,),
        in_specs=[pl.BlockSpec((tm, D_MODEL), lambda i: (i, 0)),
                  pl.BlockSpec((1, D_MODEL), lambda i: (0, 0)),
                  pl.BlockSpec((D_MODEL, ML_COLS), lambda i: (0, 0)),
                  pl.BlockSpec((D_MODEL, NSA_COLS), lambda i: (0, 0))],
        out_specs=(pl.BlockSpec((tm, ML_COLS), lambda i: (i, 0)),
                   pl.BlockSpec((tm, NSA_COLS), lambda i: (i, 0))),
        compiler_params=_params(("parallel",)),
        name="in_proj",
    )(x2, g_mix, w_ml, w_nsa)


ML_T = 256
CONV_PAD = 8


def _mlstm_kernel(q_ref, k_ref, v_ref, o_ref, g_ref, cw_ref, cb_ref, gb_ref, gout_ref, tri_ref, blk_ref,
                  out_ref, buf_ref, c_ref, n_ref, m_ref):
    T, L = ML_T, ML_CHUNK

    @pl.when(pl.program_id(1) == 0)
    def _():
        buf_ref[0:CONV_PAD, :] = jnp.zeros((CONV_PAD, 2 * ML_W), F32)
        c_ref[...] = jnp.zeros(c_ref.shape, F32)
        n_ref[...] = jnp.zeros(n_ref.shape, F32)
        m_ref[...] = jnp.zeros(m_ref.shape, F32)

    buf_ref[CONV_PAD:CONV_PAD + T, 0:ML_W] = q_ref[...]
    buf_ref[CONV_PAD:CONV_PAD + T, ML_W:2 * ML_W] = k_ref[...]
    y = cb_ref[...]
    for j in range(ML_CONV):
        lo = CONV_PAD - (ML_CONV - 1) + j
        y = y + cw_ref[j:j + 1, :] * buf_ref[lo:lo + T, :]
    buf_ref[0:CONV_PAD, :] = buf_ref[T:T + CONV_PAD, :]
    qk = y * _sigmoid(y)
    qf = qk[:, 0:ML_W] * (ML_DQK ** -0.5)
    kf = qk[:, ML_W:2 * ML_W]

    gg = g_ref[...] + gb_ref[...]
    gt = gg.T
    tri = tri_ref[...]
    blk = blk_ref[...]
    lf1, lf2, lf3 = _split3(_log_sigmoid(gg))
    b_col = _dot(tri, lf1) + _dot(tri, lf2) + _dot(tri, lf3)
    bl_col = _dot(blk, lf1) + _dot(blk, lf2) + _dot(blk, lf3)
    lt1, lt2, lt3 = _split3(_log_sigmoid(gt[0:8, :]))
    b_row = _dot_nt(lt1, tri) + _dot_nt(lt2, tri) + _dot_nt(lt3, tri)
    bl_row = _dot_nt(lt1, blk) + _dot_nt(lt2, blk) + _dot_nt(lt3, blk)

    r_i = lax.broadcasted_iota(jnp.int32, (T, T), 0)
    c_i = lax.broadcasted_iota(jnp.int32, (T, T), 1)
    causal = (r_i // L == c_i // L) & (c_i <= r_i)
    row_chunk = lax.broadcasted_iota(jnp.int32, (T, 1), 0) // L
    lane_chunk = lax.broadcasted_iota(jnp.int32, (1, T), 1) // L
    n_chunks = T // L

    for h in range(ML_HEADS):
        cols = slice(h * ML_DQK, (h + 1) * ML_DQK)
        f_lane = ML_HEADS + h
        bc, blc, igc = b_col[:, f_lane:f_lane + 1], bl_col[:, f_lane:f_lane + 1], gg[:, h:h + 1]
        br, blr, igr = b_row[f_lane:f_lane + 1, :], bl_row[f_lane:f_lane + 1, :], gt[h:h + 1, :]
        g_row = blr - br + igr
        g_col = blc - bc + igc

        m_prev = [m_ref[h][:, 0:1]]
        a_old, a_new, m_cs = [], [], []
        for c in range(n_chunks):
            m_c = jnp.max(jnp.where(lane_chunk == c, g_row, NEG), axis=1, keepdims=True)
            bl = blr[:, c * L:c * L + 1]
            m_next = jnp.maximum(bl + m_prev[c], m_c)
            a_old.append(jnp.exp(bl + m_prev[c] - m_next))
            a_new.append(jnp.exp(m_c - m_next))
            m_cs.append(m_c)
            m_prev.append(m_next)
        mprev_col = jnp.zeros((T, 1), F32)
        mc_col = jnp.zeros((T, 1), F32)
        for c in range(n_chunks):
            mprev_col = jnp.where(row_chunk == c, m_prev[c], mprev_col)
            mc_col = jnp.where(row_chunk == c, m_cs[c], mc_col)

        q_c = qf[:, cols]
        k_c = kf[:, cols]
        q_b = q_c.astype(BF16)
        v_b = v_ref[:, cols].astype(BF16)
        d_log = jnp.where(causal, bc - br + igr, NEG)
        a_log = bc + mprev_col
        m_t = jnp.maximum(a_log, jnp.max(d_log, axis=1, keepdims=True))
        wd = jnp.exp(d_log - m_t) * _dot_nt(q_b, k_c.astype(BF16))
        inter = jnp.exp(a_log - m_t)
        intra = _dot(wd.astype(BF16), v_b)
        wsum = jnp.sum(wd, axis=1, keepdims=True)
        kw = k_c * jnp.exp(g_col - mc_col)
        kw_b = kw.astype(BF16)

        c_st = c_ref[h]
        n_st = n_ref[h]
        qc_parts, qn_parts = [], []
        for c in range(n_chunks):
            rows = slice(c * L, (c + 1) * L)
            qc_parts.append(_dot(q_b[rows, :], c_st.astype(BF16)))
            qn_parts.append(jnp.sum(q_c[rows, :] * n_st, axis=1, keepdims=True))
            c_st = a_old[c] * c_st + a_new[c] * _dot_tn(kw_b[rows, :], v_b[rows, :])
            n_st = a_old[c] * n_st + a_new[c] * jnp.sum(kw[rows, :], axis=0, keepdims=True)
        c_ref[h] = c_st
        n_ref[h] = n_st
        m_ref[h] = jnp.broadcast_to(m_prev[n_chunks], (1, LANES))

        num = inter * jnp.concatenate(qc_parts, axis=0) + intra
        den = inter * jnp.concatenate(qn_parts, axis=0) + wsum
        hh = num / jnp.maximum(jnp.abs(den), jnp.exp(-m_t))
        hn = hh * lax.rsqrt(jnp.mean(hh * hh, axis=-1, keepdims=True) + EPS) * gout_ref[h:h + 1, :]
        out_ref[:, cols] = (hn * _sigmoid(o_ref[:, cols])).astype(BF16)


def _mlstm(ml, conv_w, conv_b, gate_b, g_out, bsz, seq):
    T = min(ML_T, seq)
    assert T == ML_T and seq % T == 0
    t_i = np.arange(T)
    blk = t_i[:, None] // ML_CHUNK == t_i[None, :] // ML_CHUNK
    tri = jnp.asarray(blk & (t_i[None, :] <= t_i[:, None]), BF16)
    blk = jnp.asarray(blk, BF16)
    wblk = lambda j: pl.BlockSpec((None, T, ML_W), lambda b, i, j=j: (b, i, j))
    full = lambda shape: pl.BlockSpec(shape, lambda b, i: (0,) * len(shape))
    return pl.pallas_call(
        _mlstm_kernel,
        out_shape=jax.ShapeDtypeStruct((bsz, seq, ML_W), BF16),
        grid=(bsz, seq // T),
        in_specs=[wblk(0), wblk(1), wblk(2), wblk(3),
                  pl.BlockSpec((None, T, LANES), lambda b, i: (b, i, 4 * ML_W // LANES)),
                  full((ML_CONV, 2 * ML_W)), full((1, 2 * ML_W)), full((1, LANES)),
                  full((ML_HEADS, ML_DQK)), full((T, T)), full((T, T))],
        out_specs=pl.BlockSpec((None, T, ML_W), lambda b, i: (b, i, 0)),
        scratch_shapes=[pltpu.VMEM((CONV_PAD + T, 2 * ML_W), F32),
                        pltpu.VMEM((ML_HEADS, ML_DQK, ML_DQK), F32),
                        pltpu.VMEM((ML_HEADS, 1, ML_DQK), F32),
                        pltpu.VMEM((ML_HEADS, 1, LANES), F32)],
        compiler_params=_params(("parallel", "arbitrary")),
        name="mlstm",
    )(ml, ml, ml, ml, ml, conv_w, conv_b, gate_b, g_out, tri, blk)


NSA_TP = 256
NSA_TQ = 256
NSA_OFF_KC, NSA_OFF_VC, NSA_OFF_KS, NSA_OFF_VS, NSA_OFF_KW, NSA_OFF_VW, NSA_OFF_GATE = (
    NSA_W + k * NSA_KV_W for k in range(7))


def _rope_tables(pos):
    half = NSA_DH // 2
    inv_freq = jnp.power(ROPE_THETA, -jnp.arange(half, dtype=F32) / half)
    ang = pos.astype(F32)[:, None] * inv_freq[None, :]
    cos, sin = jnp.cos(ang), jnp.sin(ang)
    cos = jnp.concatenate([cos, cos], axis=1)
    sin = jnp.concatenate([-sin, sin], axis=1)
    return jnp.tile(cos, (1, 2)), jnp.tile(sin, (1, 2))


def _head_norm_rope(xs, gsum, gain, cos, sin):
    a, b, c = _split3(xs * xs)
    ss = _dot(a, gsum) + _dot(b, gsum) + _dot(c, gsum)
    xn = xs * lax.rsqrt(ss * (1.0 / NSA_DH) + EPS) * gain
    lane = lax.broadcasted_iota(jnp.int32, xs.shape, 1)
    first = (lane % NSA_DH) < NSA_DH // 2
    swapped = jnp.where(first, pltpu.roll(xn, LANES - NSA_DH // 2, axis=1), pltpu.roll(xn, NSA_DH // 2, axis=1))
    return xn * cos + swapped * sin


def _head_norm_rope_t(xt, gsum, gain, cos_t, sin_t):
    a, b, c = _split3(xt * xt)
    ss = _dot(gsum, a) + _dot(gsum, b) + _dot(gsum, c)
    xn = xt * lax.rsqrt(ss * (1.0 / NSA_DH) + EPS) * gain
    row = lax.broadcasted_iota(jnp.int32, xt.shape, 0)
    first = (row % NSA_DH) < NSA_DH // 2
    swapped = jnp.where(first, pltpu.roll(xn, LANES - NSA_DH // 2, axis=0), pltpu.roll(xn, NSA_DH // 2, axis=0))
    return xn * cos_t + swapped * sin_t


def _store_keys(out, slab):
    lane = lax.broadcasted_iota(jnp.int32, slab.shape, 1)
    low = lane < NSA_DH
    rolled = pltpu.roll(slab, NSA_DH, axis=1)
    zero = jnp.zeros_like(slab)
    out[0, 0] = jnp.where(low, slab, zero).astype(BF16)
    out[0, 1] = jnp.where(low, zero, rolled).astype(BF16)
    out[1, 0] = jnp.where(low, rolled, zero).astype(BF16)
    out[1, 1] = jnp.where(low, zero, slab).astype(BF16)


def _store_values_t(out, slab):
    st = slab.T
    out[0] = st[0:NSA_DH, :].astype(BF16)
    out[1] = st[NSA_DH:2 * NSA_DH, :].astype(BF16)


def _nsa_prep_kernel(x_ref, cos_ref, sin_ref, cost_ref, sint_ref, gq_ref, gks_ref, gkw_ref, gsum_ref,
                     q_out, ks_out, vs_out, kw_out, vw_out, kc_out, vc_out):
    cos, sin, gsum = cos_ref[...], sin_ref[...], gsum_ref[...]
    for c in range(NSA_W // LANES):
        xt = x_ref[:, c * LANES:(c + 1) * LANES].T
        qn = _head_norm_rope_t(xt, gsum, gq_ref[...], cost_ref[...], sint_ref[...]) * (NSA_DH ** -0.5)
        q_out[c * LANES:(c + 1) * LANES, :] = qn.astype(BF16)
    ks = _head_norm_rope(x_ref[:, NSA_OFF_KS:NSA_OFF_KS + LANES], gsum, gks_ref[...], cos, sin)
    kw = _head_norm_rope(x_ref[:, NSA_OFF_KW:NSA_OFF_KW + LANES], gsum, gkw_ref[...], cos, sin)
    _store_keys(ks_out, ks)
    _store_keys(kw_out, kw)
    _store_values_t(vs_out, x_ref[:, NSA_OFF_VS:NSA_OFF_VS + LANES])
    _store_values_t(vw_out, x_ref[:, NSA_OFF_VW:NSA_OFF_VW + LANES])
    kc_out[...] = x_ref[:, NSA_OFF_KC:NSA_OFF_KC + LANES]
    vc_out[...] = x_ref[:, NSA_OFF_VC:NSA_OFF_VC + LANES]


def _nsa_cmp_kernel(xk_ref, xv_ref, pek_ref, pev_ref, w1ka_ref, w1kb_ref, w1va_ref, w1vb_ref,
                    b1k_ref, b1v_ref, w2k_ref, w2v_ref, gk_ref, cos_ref, sin_ref, gsum_ref,
                    kc_out, vc_out):
    rows = xk_ref.shape[0]

    def hidden(x, pe, wa, wb, b1):
        first = _dot((x + pe[0:1, :]).astype(BF16), wa[...])
        second = _dot((x + pe[1:2, :]).astype(BF16), wb[...])
        hid = first + pltpu.roll(second, rows - 1, axis=0) + b1[...]
        return (hid * _sigmoid(hid)).astype(BF16)

    kc = _dot(hidden(xk_ref[...], pek_ref, w1ka_ref, w1kb_ref, b1k_ref), w2k_ref[...])
    kc = _head_norm_rope(kc, gsum_ref[...], gk_ref[...], cos_ref[...], sin_ref[...])
    _store_keys(kc_out, kc)
    vc_t = _dot_nt(w2v_ref[...], hidden(xv_ref[...], pev_ref, w1va_ref, w1vb_ref, b1v_ref))
    vc_out[0] = vc_t[0:NSA_DH, :].astype(BF16)
    vc_out[1] = vc_t[NSA_DH:2 * NSA_DH, :].astype(BF16)


NSA_HEAD_SLOT = (0, 2, 1, 3)


def _stacked_scores(k_lo, k_hi, q_pair, bias):
    s = jnp.concatenate([_dot(k_lo, q_pair), _dot(k_hi, q_pair)], axis=1)
    return s + jnp.concatenate([bias] * NSA_GROUP, axis=1)


def _online_update(s, v_t, m_ref, l_ref, acc_ref):
    m_old = m_ref[...]
    m_new = jnp.maximum(m_old, jnp.max(s, axis=0, keepdims=True))
    alpha = jnp.exp(m_old - m_new)
    e = jnp.exp(s - jnp.maximum(m_new, 0.1 * NEG))
    l_ref[...] = alpha * l_ref[...] + jnp.sum(e, axis=0, keepdims=True)
    acc_ref[...] = alpha * acc_ref[...] + _dot(v_t, e.astype(BF16))
    m_ref[...] = m_new


def _nsa_attn_kernel(q_ref, kc_ref, vc_ref, ks_ref, vs_ref, kw_ref, vw_ref, gate_ref, mt_ref, esel_ref, eg_ref,
                     out_ref, ms_s, ls_s, accs_s, mw_s, lw_s, accw_s, *, n_cmp, n_sel):
    TQ = NSA_TQ
    i = pl.program_id(2)
    t0 = i * TQ
    t_row = t0 + lax.broadcasted_iota(jnp.int32, (1, TQ), 1)
    q_pair = jnp.concatenate([q_ref[0:LANES, :], q_ref[LANES:2 * LANES, :]], axis=1)

    rows = kc_ref.shape[1]
    n_i = lax.broadcasted_iota(jnp.int32, (rows, 1), 0)
    cmask = (n_i < n_cmp) & (n_i * CMP_STRIDE + (CMP_BLOCK - 1) <= t_row)
    s = _stacked_scores(kc_ref[0], kc_ref[1], q_pair, jnp.where(cmask, 0.0, NEG))
    e = jnp.exp(s - jnp.maximum(jnp.max(s, axis=0, keepdims=True), 0.1 * NEG))
    p = e / jnp.maximum(jnp.sum(e, axis=0, keepdims=True), 1e-30)
    o_cmp_t = _dot(vc_ref[...], p.astype(BF16))
    psum = p[:, 0:TQ] + p[:, TQ:2 * TQ] + p[:, 2 * TQ:3 * TQ] + p[:, 3 * TQ:4 * TQ]
    p1, p2, p3 = _split3(psum)
    imp = _dot(mt_ref[...], p1) + _dot(mt_ref[...], p2) + _dot(mt_ref[...], p3)
    j_i = lax.broadcasted_iota(jnp.int32, (n_sel, TQ), 0)
    cur = (t0 + lax.broadcasted_iota(jnp.int32, (n_sel, TQ), 1)) // SEL_BLOCK
    forced = (j_i == 0) | (j_i == cur) | (j_i == cur - 1)
    score = jnp.where(j_i > cur, NEG, imp[0:n_sel, :] + SEL_FORCE_BONUS * forced.astype(F32))
    rank = jnp.zeros((n_sel, TQ), jnp.int32)
    for jp in range(n_sel):
        row = score[jp:jp + 1, :]
        beats = (row > score) | ((row == score) & (j_i > jp))
        rank = rank + beats.astype(jnp.int32)
    sel_t = (rank < min(SEL_TOPN, n_sel)).astype(F32)
    sel_b = jnp.concatenate([sel_t, jnp.zeros((LANES - n_sel, TQ), F32)], axis=0).astype(BF16)

    for m_ref, l_ref, acc_ref in ((ms_s, ls_s, accs_s), (mw_s, lw_s, accw_s)):
        m_ref[...] = jnp.full(m_ref.shape, NEG, F32)
        l_ref[...] = jnp.zeros(l_ref.shape, F32)
        acc_ref[...] = jnp.zeros(acc_ref.shape, F32)

    def key_tile(jt, carry):
        k0 = pl.multiple_of(jt * TQ, TQ)
        diff = t_row - (k0 + lax.broadcasted_iota(jnp.int32, (TQ, 1), 0))
        causal = diff >= 0
        smask = (_dot(esel_ref[jt], sel_b) > 0.5) & causal
        s = _stacked_scores(ks_ref[0, pl.ds(k0, TQ), :], ks_ref[1, pl.ds(k0, TQ), :], q_pair,
                            jnp.where(smask, 0.0, NEG))
        _online_update(s, vs_ref[jt], ms_s, ls_s, accs_s)

        @pl.when(jt * TQ + TQ > t0 - WINDOW + 1)
        def _():
            wmask = causal & (diff < WINDOW)
            sw = _stacked_scores(kw_ref[0, pl.ds(k0, TQ), :], kw_ref[1, pl.ds(k0, TQ), :], q_pair,
                                 jnp.where(wmask, 0.0, NEG))
            _online_update(sw, vw_ref[jt], mw_s, lw_s, accw_s)
        return carry

    lax.fori_loop(0, i + 1, key_tile, 0)

    g1, g2, g3 = _split3(_sigmoid(gate_ref[...]))
    gexp = _dot(g1, eg_ref[...]) + _dot(g2, eg_ref[...]) + _dot(g3, eg_ref[...])
    half_w = NSA_GROUP * NSA_DH
    o_sel_t = accs_s[...] / jnp.maximum(ls_s[...], 1e-30)
    o_win_t = accw_s[...] / jnp.maximum(lw_s[...], 1e-30)

    def token_major(o_t, c):
        ce, co = NSA_HEAD_SLOT[2 * c] * TQ, NSA_HEAD_SLOT[2 * c + 1] * TQ
        return jnp.concatenate([o_t[:, ce:ce + TQ], o_t[:, co:co + TQ]], axis=0).T

    for c in range(NSA_GROUP // 2):
        o_cmp = token_major(o_cmp_t, c)
        o_sel = token_major(o_sel_t, c)
        o_win = token_major(o_win_t, c)
        cs = slice(c * LANES, (c + 1) * LANES)
        o = (gexp[:, c * LANES:(c + 1) * LANES] * o_cmp
             + gexp[:, half_w + c * LANES:half_w + (c + 1) * LANES] * o_sel
             + gexp[:, 2 * half_w + c * LANES:2 * half_w + (c + 1) * LANES] * o_win)
        out_ref[:, cs] = o.astype(BF16)


def _nsa_constants(seq):
    n_cmp = (seq - CMP_BLOCK) // CMP_STRIDE + 1
    n_sel = seq // SEL_BLOCK
    rows = seq // CMP_STRIDE
    n = np.arange(rows)[None, :]
    j = np.arange(LANES)[:, None]
    mt = ((n * CMP_STRIDE < j * SEL_BLOCK + SEL_BLOCK) & (n * CMP_STRIDE + CMP_BLOCK > j * SEL_BLOCK)
          & (n < n_cmp) & (j < n_sel))
    s = np.arange(seq)
    esel = (s[:, None] // SEL_BLOCK == np.arange(LANES)[None, :])
    esel = esel.reshape(seq // NSA_TQ, NSA_TQ, LANES)
    eg = np.zeros((NSA_KVH, LANES, N_BRANCH * NSA_GROUP * NSA_DH), np.float32)
    for h in range(NSA_KVH):
        for g in range(NSA_GROUP):
            for br in range(N_BRANCH):
                lane = (h * NSA_GROUP + g) * N_BRANCH + br
                c0 = br * NSA_GROUP * NSA_DH + g * NSA_DH
                eg[h, lane, c0:c0 + NSA_DH] = 1.0
    gsum = np.kron(np.eye(LANES // NSA_DH), np.ones((NSA_DH, NSA_DH)))
    return (n_cmp, n_sel, jnp.asarray(mt, BF16), jnp.asarray(esel, BF16), jnp.asarray(eg, BF16),
            jnp.asarray(gsum, BF16))


def _nsa(nsa, w, bsz, seq):
    assert seq % NSA_TQ == 0 and seq % NSA_TP == 0
    n_cmp, n_sel, mt, esel, eg, gsum = _nsa_constants(seq)
    rows = seq // CMP_STRIDE
    cos, sin = _rope_tables(jnp.arange(seq))
    cos_c, sin_c = _rope_tables(jnp.arange(rows) * CMP_STRIDE + (CMP_BLOCK - 1))
    c2 = lambda shape: pl.BlockSpec(shape, lambda b, i: (0,) * len(shape))
    n_tiles = seq // NSA_TP
    k_shape = jax.ShapeDtypeStruct((bsz, NSA_KVH, 2, seq, LANES), BF16)
    k_spec = pl.BlockSpec((None, NSA_KVH, 2, NSA_TP, LANES), lambda b, i: (b, 0, 0, i, 0))
    v_shape = jax.ShapeDtypeStruct((bsz, NSA_KVH, n_tiles, NSA_DH, NSA_TP), BF16)
    v_spec = pl.BlockSpec((None, NSA_KVH, None, NSA_DH, NSA_TP), lambda b, i: (b, 0, i, 0, 0))
    raw_shape = jax.ShapeDtypeStruct((bsz, seq, LANES), F32)
    raw_spec = pl.BlockSpec((None, NSA_TP, LANES), lambda b, i: (b, i, 0))
    qn, ksd, vsd, kwd, vwd, kc_raw, vc_raw = pl.pallas_call(
        _nsa_prep_kernel,
        out_shape=(jax.ShapeDtypeStruct((bsz, NSA_W, seq), BF16), k_shape, v_shape, k_shape, v_shape,
                   raw_shape, raw_shape),
        grid=(bsz, n_tiles),
        in_specs=[pl.BlockSpec((None, NSA_TP, NSA_COLS), lambda b, i: (b, i, 0)),
                  pl.BlockSpec((NSA_TP, LANES), lambda b, i: (i, 0)),
                  pl.BlockSpec((NSA_TP, LANES), lambda b, i: (i, 0)),
                  pl.BlockSpec((LANES, NSA_TP), lambda b, i: (0, i)),
                  pl.BlockSpec((LANES, NSA_TP), lambda b, i: (0, i)),
                  c2((LANES, 1)), c2((1, LANES)), c2((1, LANES)), c2((LANES, LANES))],
        out_specs=(pl.BlockSpec((None, NSA_W, NSA_TP), lambda b, i: (b, 0, i)),
                   k_spec, v_spec, k_spec, v_spec, raw_spec, raw_spec),
        compiler_params=_params(("parallel", "parallel")),
        name="nsa_prep",
    )(nsa, cos, sin, cos.T, sin.T, w["nsa_gq"].reshape(LANES, 1), w["nsa_gks"], w["nsa_gkw"], gsum)

    wide = CMP_STRIDE * LANES
    xk = kc_raw.reshape(bsz, rows, wide)
    xv = vc_raw.reshape(bsz, rows, wide)
    c1 = lambda shape: pl.BlockSpec(shape, lambda b: (0,) * len(shape))
    kc_shape = jax.ShapeDtypeStruct((bsz, NSA_KVH, 2, rows, LANES), BF16)
    kc_spec = pl.BlockSpec((None, NSA_KVH, 2, rows, LANES), lambda b: (b, 0, 0, 0, 0))
    vc_shape = jax.ShapeDtypeStruct((bsz, NSA_KVH, NSA_DH, rows), BF16)
    vc_spec = pl.BlockSpec((None, NSA_KVH, NSA_DH, rows), lambda b: (b, 0, 0, 0))
    hid2 = NSA_KVH * CMP_HIDDEN
    kcd, vcd = pl.pallas_call(
        _nsa_cmp_kernel,
        out_shape=(kc_shape, vc_shape),
        grid=(bsz,),
        in_specs=[pl.BlockSpec((None, rows, wide), lambda b: (b, 0, 0)),
                  pl.BlockSpec((None, rows, wide), lambda b: (b, 0, 0)),
                  c1((2, wide)), c1((2, wide)),
                  c1((wide, hid2)), c1((wide, hid2)), c1((wide, hid2)), c1((wide, hid2)),
                  c1((1, hid2)), c1((1, hid2)), c1((hid2, LANES)), c1((LANES, hid2)),
                  c1((1, LANES)), c1((rows, LANES)), c1((rows, LANES)), c1((LANES, LANES))],
        out_specs=(kc_spec, vc_spec),
        compiler_params=_params(("parallel",)),
        name="nsa_compress",
    )(xk, xv, w["cmp_pe_k"], w["cmp_pe_v"], w["cmp_w1k_a"], w["cmp_w1k_b"], w["cmp_w1v_a"], w["cmp_w1v_b"],
      w["cmp_b1_k"], w["cmp_b1_v"], w["cmp_w2_k"], w["cmp_w2_v"].T, w["nsa_gkc"], cos_c, sin_c, gsum)

    TQ = NSA_TQ
    half_w = NSA_GROUP * NSA_DH
    c3 = lambda shape: pl.BlockSpec(shape, lambda b, h, i: (0,) * len(shape))
    assert NSA_TP == TQ
    keys = lambda r: pl.BlockSpec((None, None, 2, r, LANES), lambda b, h, i: (b, h, 0, 0, 0))
    vals = pl.BlockSpec((None, None, seq // TQ, NSA_DH, TQ), lambda b, h, i: (b, h, 0, 0, 0))
    stat = lambda: pltpu.VMEM((1, NSA_GROUP * TQ), F32)
    accu = lambda: pltpu.VMEM((NSA_DH, NSA_GROUP * TQ), F32)
    return pl.pallas_call(
        functools.partial(_nsa_attn_kernel, n_cmp=n_cmp, n_sel=n_sel),
        out_shape=jax.ShapeDtypeStruct((bsz, seq, NSA_W), BF16),
        grid=(bsz, NSA_KVH, seq // TQ),
        in_specs=[pl.BlockSpec((None, half_w, TQ), lambda b, h, i: (b, h, i)),
                  keys(rows), pl.BlockSpec((None, None, NSA_DH, rows), lambda b, h, i: (b, h, 0, 0)),
                  keys(seq), vals, keys(seq), vals,
                  pl.BlockSpec((None, TQ, LANES), lambda b, h, i: (b, i, NSA_OFF_GATE // LANES)),
                  c3((LANES, rows)), c3((seq // TQ, TQ, LANES)),
                  pl.BlockSpec((None, LANES, N_BRANCH * half_w), lambda b, h, i: (h, 0, 0))],
        out_specs=pl.BlockSpec((None, TQ, half_w), lambda b, h, i: (b, i, h)),
        scratch_shapes=[stat(), stat(), accu(), stat(), stat(), accu()],
        compiler_params=_params(("parallel", "parallel", "arbitrary")),
        name="nsa_attention",
    )(qn, kcd, vcd, ksd, vsd, kwd, vwd, nsa, mt, esel, eg)


POST_TM = 256


def _rms(v, gain):
    return v * lax.rsqrt(jnp.mean(v * v, axis=-1, keepdims=True) + EPS) * gain


def _memkv_kernel(mem_ref, g_ref, w_ref, gk_ref, k_out, v_out):
    u = _rms(mem_ref[...], g_ref[...]).astype(BF16)
    kv = _dot(u, w_ref[...])
    for h in range(MEM_HEADS):
        cs = slice(h * MEM_DH, (h + 1) * MEM_DH)
        k_out[:, cs] = _rms(kv[:, cs], gk_ref[...]).astype(BF16)
    v_out[...] = kv[:, MEM_W:2 * MEM_W].astype(BF16)


def _memkv(mem, g_mem, w_mem_kv, g_k):
    bsz, m_len, _ = mem.shape
    c1 = lambda shape: pl.BlockSpec(shape, lambda b: (0,) * len(shape))
    shape = jax.ShapeDtypeStruct((bsz, m_len, MEM_W), BF16)
    spec = pl.BlockSpec((None, m_len, MEM_W), lambda b: (b, 0, 0))
    return pl.pallas_call(
        _memkv_kernel, out_shape=(shape, shape), grid=(bsz,),
        in_specs=[pl.BlockSpec((None, m_len, D_MODEL), lambda b: (b, 0, 0)),
                  c1((1, D_MODEL)), c1((D_MODEL, 2 * MEM_W)), c1((1, MEM_DH))],
        out_specs=(spec, spec),
        compiler_params=_params(("parallel",)),
        name="mem_kv",
    )(mem, g_mem, w_mem_kv, g_k)


def _post_kernel(x_ref, hml_ref, hnsa_ref, mk_ref, mv_ref, gmix_ref, wmq_ref, gmq_ref, wgate_ref,
                 wml_ref, wnsa_ref, wmem_ref, wout_ref, gffn_ref, wrh_ref, wrl_ref, br_ref,
                 x1_out, t_out, logit_out):
    x = x_ref[...]
    ub = _rms(x, gmix_ref[...]).astype(BF16)
    mq = _dot(ub, wmq_ref[...])
    heads = []
    for h in range(MEM_HEADS):
        cs = slice(h * MEM_DH, (h + 1) * MEM_DH)
        qn = (_rms(mq[:, cs], gmq_ref[...]) * (MEM_DH ** -0.5)).astype(BF16)
        s = _dot_nt(qn, mk_ref[:, cs])
        e = jnp.exp(s - jnp.max(s, axis=1, keepdims=True))
        p = e / jnp.sum(e, axis=1, keepdims=True)
        heads.append(_dot(p.astype(BF16), mv_ref[:, cs]))
    hmem = jnp.concatenate(heads, axis=1).astype(BF16)
    gates = _sigmoid(_dot(ub, wgate_ref[...]))
    merged = (gates[:, 0:D_MODEL] * _dot(hml_ref[...], wml_ref[...])
              + gates[:, D_MODEL:2 * D_MODEL] * _dot(hnsa_ref[...], wnsa_ref[...])
              + gates[:, 2 * D_MODEL:3 * D_MODEL] * _dot(hmem, wmem_ref[...]))
    x1 = x + _dot(merged.astype(BF16), wout_ref[...])
    x1_out[...] = x1
    t = _rms(x1, gffn_ref[...])
    t_out[...] = t
    t1, t2, _ = _split3(t)
    logit_out[...] = _dot(t1, wrh_ref[...]) + _dot(t1, wrl_ref[...]) + _dot(t2, wrh_ref[...]) + br_ref[...]


def _post(x2, hml, hnsa, mk, mv, w, seq):
    n = x2.shape[0]
    tm = POST_TM
    assert seq % tm == 0
    per = seq // tm
    m_len = mk.shape[1]
    row = lambda width: pl.BlockSpec((tm, width), lambda i: (i, 0))
    c1 = lambda shape: pl.BlockSpec(shape, lambda i: (0,) * len(shape))
    memspec = pl.BlockSpec((None, m_len, MEM_W), lambda i: (i // per, 0, 0))
    return pl.pallas_call(
        _post_kernel,
        out_shape=(jax.ShapeDtypeStruct((n, D_MODEL), F32), jax.ShapeDtypeStruct((n, D_MODEL), F32),
                   jax.ShapeDtypeStruct((n, LANES), F32)),
        grid=(n // tm,),
        in_specs=[row(D_MODEL), row(ML_W), row(NSA_W), memspec, memspec,
                  c1((1, D_MODEL)), c1((D_MODEL, MEM_W)), c1((1, MEM_DH)), c1((D_MODEL, N_BRANCH * D_MODEL)),
                  c1((ML_W, D_MODEL)), c1((NSA_W, D_MODEL)), c1((MEM_W, D_MODEL)), c1((D_MODEL, D_MODEL)),
                  c1((1, D_MODEL)), c1((D_MODEL, LANES)), c1((D_MODEL, LANES)), c1((1, LANES))],
        out_specs=(row(D_MODEL), row(D_MODEL), row(LANES)),
        compiler_params=_params(("parallel",)),
        name="post",
    )(x2, hml, hnsa, mk, mv, w["g_mix"], w["w_mq"], w["mem_gq"], w["w_gate"], w["w_br_ml"], w["w_br_nsa"],
      w["w_br_mem"], w["w_out"], w["g_ffn"], w["w_router_hi"], w["w_router_lo"], w["b_router"])


ROUTE_T = 512
MOE_BLK = 256
MOE_TD = 256
MOE_DEPTH = 3


def _route_kernel(logit_ref, e_out, wtm_out, cnt_out):
    T = ROUTE_T

    @pl.when(pl.program_id(0) == 0)
    def _():
        cnt_out[...] = jnp.zeros(cnt_out.shape, F32)

    sc = logit_ref[...].T[0:N_EXPERTS, :]
    j_i = lax.broadcasted_iota(jnp.int32, (N_EXPERTS, T), 0)
    rank = jnp.zeros((N_EXPERTS, T), jnp.int32)
    for jp in range(N_EXPERTS):
        row = sc[jp:jp + 1, :]
        beats = (row > sc) | ((row == sc) & (j_i > jp))
        rank = rank + beats.astype(jnp.int32)
    sel = rank < TOP_K
    e = jnp.where(sel, jnp.exp(sc - jnp.max(sc, axis=0, keepdims=True)), 0.0)
    wgt = e / jnp.sum(e, axis=0, keepdims=True)
    cnt_out[...] = cnt_out[...] + jnp.sum(sel.astype(F32), axis=1, keepdims=True)
    e_rows, w_rows = [], []
    for r in range(TOP_K):
        hit = rank == r
        e_rows.append(jnp.sum(jnp.where(hit, j_i, 0), axis=0, keepdims=True))
        w_rows.append(jnp.sum(jnp.where(hit, wgt, 0.0), axis=0, keepdims=True))
    e_out[...] = jnp.concatenate(e_rows, axis=0)
    w4 = jnp.concatenate(w_rows, axis=0)
    wtm_out[...] = jnp.concatenate([w4, jnp.zeros((LANES - TOP_K, T), F32)], axis=0).T


def _route(logits):
    n = logits.shape[0]
    T = ROUTE_T
    assert n % T == 0
    return pl.pallas_call(
        _route_kernel,
        out_shape=(jax.ShapeDtypeStruct((TOP_K, n), jnp.int32), jax.ShapeDtypeStruct((n, LANES), F32),
                   jax.ShapeDtypeStruct((N_EXPERTS, LANES), F32)),
        grid=(n // T,),
        in_specs=[pl.BlockSpec((T, LANES), lambda i: (i, 0))],
        out_specs=(pl.BlockSpec((TOP_K, T), lambda i: (0, i)), pl.BlockSpec((T, LANES), lambda i: (i, 0)),
                   pl.BlockSpec((N_EXPERTS, LANES), lambda i: (0, 0))),
        compiler_params=_params(("arbitrary",)),
        name="moe_route",
    )(logits)


def _ffn_fused_kernel(be_ref, base_ref, nv_ref, order_ref, t_hbm, wgu_ref, bgu_ref, wd_ref, bd_ref, yk_hbm,
                      xbuf, ybuf, gsem, ssem, wgu_s, wd_s, *, n_tok):
    i = pl.program_id(0)
    nb = pl.num_programs(0)
    slot = i % MOE_DEPTH
    last_id = n_tok * TOP_K - 1
    dump0 = n_tok * TOP_K

    def start_gather(blk, buf):
        b0 = base_ref[blk]
        for j in range(MOE_BLK):
            tok = order_ref[jnp.minimum(b0 + j, last_id)] & (n_tok - 1)
            pltpu.make_async_copy(t_hbm.at[pl.ds(tok, 1)], xbuf.at[buf, pl.ds(j, 1)], gsem.at[buf]).start()

    def start_scatter(blk, nv, buf):
        b0 = base_ref[blk]
        for j in range(MOE_BLK):
            dst = jnp.where(j < nv, order_ref[jnp.minimum(b0 + j, last_id)], dump0 + buf * MOE_BLK + j)
            pltpu.make_async_copy(ybuf.at[buf, pl.ds(j, 1)], yk_hbm.at[pl.ds(dst, 1)], ssem.at[buf]).start()

    def wait_gather(buf):
        pltpu.make_async_copy(t_hbm.at[pl.ds(0, MOE_BLK)], xbuf.at[buf], gsem.at[buf]).wait()

    def wait_scatter(buf):
        pltpu.make_async_copy(ybuf.at[buf], yk_hbm.at[pl.ds(0, MOE_BLK)], ssem.at[buf]).wait()

    @pl.when(i == 0)
    def _():
        xbuf[...] = jnp.zeros(xbuf.shape, F32)
        ybuf[...] = jnp.zeros(ybuf.shape, F32)
        for q in range(MOE_DEPTH - 1):
            start_gather(jnp.minimum(q, nb - 1), q)
            start_scatter(0, 0, q)

    @pl.when((i == 0) | (be_ref[i] != be_ref[jnp.maximum(i - 1, 0)]))
    def _():
        wgu_s[...] = wgu_ref[...].astype(BF16)
        bits = lax.shift_right_logical(pltpu.bitcast(wd_ref[...].astype(BF16).astype(F32), jnp.uint32),
                                       jnp.uint32(16))
        wd_s[...] = bits | lax.shift_left(bits, jnp.uint32(16))

    for par in range(MOE_DEPTH):
        ahead = (par + MOE_DEPTH - 1) % MOE_DEPTH

        @pl.when(slot == par)
        def _(par=par, ahead=ahead):
            wait_gather(par)
            start_gather(jnp.minimum(i + MOE_DEPTH - 1, nb - 1), ahead)
            prev = jnp.maximum(i - 1, 0)
            start_scatter(prev, jnp.where(i == 0, 0, nv_ref[prev]), ahead)
            hgu = _dot(xbuf[par].astype(BF16), wgu_s[...]) + bgu_ref[...]
            gate = jnp.minimum(hgu, SWIGLU_LIMIT)
            up1 = jnp.clip(hgu, -SWIGLU_LIMIT, SWIGLU_LIMIT) + 1.0
            even = lax.broadcasted_iota(jnp.int32, (1, 2 * D_FF), 1) % 2 == 0
            act = jnp.where(even, gate * _sigmoid(SWIGLU_ALPHA * gate) * pltpu.roll(up1, 2 * D_FF - 1, axis=1), 0.0)
            y = _dot(act.astype(BF16), pltpu.bitcast(wd_s[...], BF16)) + bd_ref[...]
            wait_scatter(par)
            ybuf[par] = y

            @pl.when(i == nb - 1)
            def _():
                start_scatter(i, nv_ref[i], par)
                for q in range(MOE_DEPTH):
                    if q != par:
                        wait_gather(q)
                    wait_scatter(q)


def _ffn_fused(t, blk_e, base, n_valid, order, w):
    n = t.shape[0]
    nb = blk_e.shape[0]
    espec = lambda r, c: pl.BlockSpec((None, r, c), lambda i, be, *_: (be[i], 0, 0))
    return pl.pallas_call(
        functools.partial(_ffn_fused_kernel, n_tok=n),
        out_shape=jax.ShapeDtypeStruct((n * TOP_K + MOE_DEPTH * MOE_BLK, D_MODEL), F32),
        grid_spec=pltpu.PrefetchScalarGridSpec(
            num_scalar_prefetch=4, grid=(nb,),
            in_specs=[pl.BlockSpec(memory_space=pl.ANY), espec(D_MODEL, 2 * D_FF), espec(1, 2 * D_FF),
                      espec(D_FF, D_MODEL), espec(1, D_MODEL)],
            out_specs=pl.BlockSpec(memory_space=pl.ANY),
            scratch_shapes=[pltpu.VMEM((MOE_DEPTH, MOE_BLK, D_MODEL), F32),
                            pltpu.VMEM((MOE_DEPTH, MOE_BLK, D_MODEL), F32),
                            pltpu.SemaphoreType.DMA((MOE_DEPTH,)), pltpu.SemaphoreType.DMA((MOE_DEPTH,)),
                            pltpu.VMEM((D_MODEL, 2 * D_FF), BF16), pltpu.VMEM((D_FF, D_MODEL), jnp.uint32)]),
        compiler_params=_params(("arbitrary",)),
        name="moe_ffn",
    )(blk_e, base, n_valid, order, t, w["w_gu"], w["b_gu"], w["w_down"], w["b_down"])


def _moe_sum_kernel(x1_ref, wtm_ref, y0_ref, y1_ref, y2_ref, y3_ref, out_ref):
    wtm = wtm_ref[...]
    acc = x1_ref[...]
    for r, y_ref in enumerate((y0_ref, y1_ref, y2_ref, y3_ref)):
        acc = acc + wtm[:, r:r + 1] * y_ref[...]
    out_ref[...] = acc


def _moe_sum(x1, wtm, yk):
    n = x1.shape[0]
    per = n // MOE_TD
    row = lambda width: pl.BlockSpec((MOE_TD, width), lambda i: (i, 0))
    ysp = lambda r: pl.BlockSpec((MOE_TD, D_MODEL), lambda i, r=r: (r * per + i, 0))
    return pl.pallas_call(
        _moe_sum_kernel,
        out_shape=jax.ShapeDtypeStruct((n, D_MODEL), F32),
        grid=(per,),
        in_specs=[row(D_MODEL), row(LANES), ysp(0), ysp(1), ysp(2), ysp(3)],
        out_specs=row(D_MODEL),
        compiler_params=_params(("parallel",)),
        name="moe_sum",
    )(x1, wtm, yk, yk, yk, yk)


def _moe(x1, t, logits, w):
    n = x1.shape[0]
    e_idx, wtm, cnt = _route(logits)
    counts = cnt[:, 0].astype(jnp.int32)
    padded = (counts + MOE_BLK - 1) // MOE_BLK * MOE_BLK
    pad_end = jnp.cumsum(padded)
    pad_start = pad_end - padded
    start = jnp.cumsum(counts) - counts
    nb = -(-(n * TOP_K + N_EXPERTS * (MOE_BLK - 1)) // MOE_BLK)
    first = jnp.arange(nb, dtype=jnp.int32) * MOE_BLK
    blk_e = jnp.minimum(jnp.sum((pad_end[None, :] <= first[:, None]).astype(jnp.int32), axis=1), N_EXPERTS - 1)
    within = first - pad_start[blk_e]
    n_valid = jnp.clip(counts[blk_e] - within, 0, MOE_BLK).astype(jnp.int32)
    base = (start[blk_e] + within).astype(jnp.int32)
    assert n & (n - 1) == 0
    flat = jnp.argsort(e_idx.T.reshape(-1), stable=True).astype(jnp.int32)
    order = (flat % TOP_K) * n + flat // TOP_K
    yk = _ffn_fused(t, blk_e, base, n_valid, order, w)
    return _moe_sum(x1, wtm, yk)


def _cmp_weights(pos, w1, b1, w2):
    eye = jnp.eye(NSA_KVH, dtype=F32)
    halves = []
    for part in (w1[:CMP_STRIDE], w1[CMP_STRIDE:]):
        big = jnp.einsum("ldf,hg->lhdgf", part, eye)
        halves.append(big.reshape(CMP_STRIDE * LANES, NSA_KVH * CMP_HIDDEN).astype(BF16))
    pe = jnp.stack([jnp.broadcast_to(part[:, None, :], (CMP_STRIDE, NSA_KVH, NSA_DH)).reshape(-1)
                    for part in (pos[:CMP_STRIDE], pos[CMP_STRIDE:])])
    w2big = jnp.einsum("fd,hg->hfgd", w2, eye).reshape(NSA_KVH * CMP_HIDDEN, LANES).astype(BF16)
    return halves[0], halves[1], pe, jnp.tile(b1, NSA_KVH).reshape(1, -1), w2big


def _prep_weights(p):
    w = {}
    offs = np.concatenate([[0], np.cumsum(IN_SPLITS)])
    col = lambda k: p["w_in"][:, offs[k]:offs[k + 1]]
    zpad = lambda n: jnp.zeros((D_MODEL, n), F32)
    w["w_ml"] = jnp.concatenate([col(0), col(1), col(2), col(3), col(4), col(5),
                                 zpad(LANES - 2 * ML_HEADS)], axis=1).astype(BF16)
    w["w_nsa"] = jnp.concatenate([col(k) for k in range(6, 14)] + [zpad(LANES - NSA_QH * 3)], axis=1).astype(BF16)
    w["g_mix"] = p["g_mix"].reshape(1, D_MODEL)
    w["conv_w"] = p["ml_conv_w"]
    w["conv_b"] = p["ml_conv_b"].reshape(1, 2 * ML_W)
    w["ml_gate_b"] = jnp.concatenate([p["ml_b_i"], p["ml_b_f"], jnp.zeros((LANES - 2 * ML_HEADS,), F32)]).reshape(1, LANES)
    w["ml_g_out"] = p["ml_g_out"]

    pair = lambda g: jnp.tile(g, LANES // NSA_DH).reshape(1, LANES)
    w["nsa_gq"] = pair(p["nsa_g_q"])
    w["nsa_gkc"], w["nsa_gks"], w["nsa_gkw"] = (pair(p["nsa_g_k"][k]) for k in range(3))
    (w["cmp_w1k_a"], w["cmp_w1k_b"], w["cmp_pe_k"], w["cmp_b1_k"], w["cmp_w2_k"]) = _cmp_weights(
        p["cmp_pos_k"], p["cmp_w1_k"], p["cmp_b1_k"], p["cmp_w2_k"])
    (w["cmp_w1v_a"], w["cmp_w1v_b"], w["cmp_pe_v"], w["cmp_b1_v"], w["cmp_w2_v"]) = _cmp_weights(
        p["cmp_pos_v"], p["cmp_w1_v"], p["cmp_b1_v"], p["cmp_w2_v"])

    w["g_mem"] = p["g_mem"].reshape(1, D_MODEL)
    w["w_mem_kv"] = p["w_mem_kv"].astype(BF16)
    w["mem_gq"] = p["mem_g_q"].reshape(1, MEM_DH)
    w["mem_gk"] = p["mem_g_k"].reshape(1, MEM_DH)
    w["w_mq"] = col(14).astype(BF16)
    w["w_gate"] = col(15).astype(BF16)
    for k in ("w_br_ml", "w_br_nsa", "w_br_mem", "w_out"):
        w[k] = p[k].astype(BF16)
    w["g_ffn"] = p["g_ffn"].reshape(1, D_MODEL)
    w_r = jnp.concatenate([p["w_router"], jnp.zeros((D_MODEL, LANES - N_EXPERTS), F32)], axis=1)
    w["w_router_hi"] = w_r.astype(BF16)
    w["w_router_lo"] = (w_r - w["w_router_hi"].astype(F32)).astype(BF16)
    w["b_router"] = jnp.concatenate([p["b_router"], jnp.zeros((LANES - N_EXPERTS,), F32)]).reshape(1, LANES)
    w["w_gu"] = p["w_gu"]
    w["b_gu"] = p["b_gu"][:, None, :]
    w["w_down"] = p["w_down"]
    w["b_down"] = p["b_down"][:, None, :]
    return w


_PARAM_NAMES = ("g_mix", "w_in", "ml_conv_w", "ml_conv_b", "ml_b_i", "ml_b_f", "ml_g_out", "nsa_g_q", "nsa_g_k",
                "cmp_pos_k", "cmp_w1_k", "cmp_b1_k", "cmp_w2_k", "cmp_pos_v", "cmp_w1_v", "cmp_b1_v", "cmp_w2_v",
                "g_mem", "w_mem_kv", "mem_g_q", "mem_g_k", "w_br_ml", "w_br_nsa", "w_br_mem", "w_out", "g_ffn",
                "w_router", "b_router", "w_gu", "b_gu", "w_down", "b_down")


def _layer(x, mem, p):
    bsz, seq, d = x.shape
    w = _prep_weights(p)
    x2 = x.reshape(bsz * seq, d)
    ml, nsa = _proj(x2, w["g_mix"], w["w_ml"], w["w_nsa"])
    h_ml = _mlstm(ml.reshape(bsz, seq, ML_COLS), w["conv_w"], w["conv_b"], w["ml_gate_b"], w["ml_g_out"], bsz, seq)
    h_nsa = _nsa(nsa.reshape(bsz, seq, NSA_COLS), w, bsz, seq)
    mk, mv = _memkv(mem, w["g_mem"], w["w_mem_kv"], w["mem_gk"])
    x1, t, logits = _post(x2, h_ml.reshape(bsz * seq, ML_W), h_nsa.reshape(bsz * seq, NSA_W), mk, mv, w, seq)
    return _moe(x1, t, logits, w).reshape(bsz, seq, d)


def kernel(x, mem, g_mix, w_in, ml_conv_w, ml_conv_b, ml_b_i, ml_b_f, ml_g_out, nsa_g_q, nsa_g_k, cmp_pos_k,
           cmp_w1_k, cmp_b1_k, cmp_w2_k, cmp_pos_v, cmp_w1_v, cmp_b1_v, cmp_w2_v, g_mem, w_mem_kv, mem_g_q,
           mem_g_k, w_br_ml, w_br_nsa, w_br_mem, w_out, g_ffn, w_router, b_router, w_gu, b_gu, w_down, b_down):
    stacked = (g_mix, w_in, ml_conv_w, ml_conv_b, ml_b_i, ml_b_f, ml_g_out, nsa_g_q, nsa_g_k, cmp_pos_k,
               cmp_w1_k, cmp_b1_k, cmp_w2_k, cmp_pos_v, cmp_w1_v, cmp_b1_v, cmp_w2_v, g_mem, w_mem_kv, mem_g_q,
               mem_g_k, w_br_ml, w_br_nsa, w_br_mem, w_out, g_ffn, w_router, b_router, w_gu, b_gu, w_down, b_down)
    h = x
    for layer in range(g_mix.shape[0]):
        h = _layer(h, mem, {name: v[layer] for name, v in zip(_PARAM_NAMES, stacked)})
    return h
```

```python
import functools

import jax
import jax.numpy as jnp
import numpy as np
from jax import lax
from jax.experimental import pallas as pl
from jax.experimental.pallas import tpu as pltpu

F32 = jnp.float32
BF16 = jnp.bfloat16

D_MODEL = 1024
ML_HEADS = 4
ML_DQK = 128
ML_W = ML_HEADS * ML_DQK
ML_CHUNK = 64
ML_CONV = 4
NSA_QH = 8
NSA_KVH = 2
NSA_DH = 64
NSA_GROUP = NSA_QH // NSA_KVH
NSA_W = NSA_QH * NSA_DH
NSA_KV_W = NSA_KVH * NSA_DH
CMP_BLOCK = 32
CMP_STRIDE = 16
CMP_HIDDEN = 256
SEL_BLOCK = 64
SEL_TOPN = 8
SEL_FORCE_BONUS = 1e4
WINDOW = 512
MEM_HEADS = 4
MEM_DH = 128
MEM_W = MEM_HEADS * MEM_DH
N_EXPERTS = 32
TOP_K = 4
D_FF = D_MODEL
SWIGLU_ALPHA = 1.702
SWIGLU_LIMIT = 7.0
N_BRANCH = 3
ROPE_THETA = 10000.0
EPS = 1e-6
NEG = -1e30

LANES = 128
VMEM_LIMIT = 56 * 1024 * 1024

IN_SPLITS = (ML_W, ML_W, ML_W, ML_W, ML_HEADS, ML_HEADS,
             NSA_W, NSA_KV_W, NSA_KV_W, NSA_KV_W, NSA_KV_W, NSA_KV_W, NSA_KV_W, NSA_QH * 3,
             MEM_W, N_BRANCH * D_MODEL)

ML_COLS = 4 * ML_W + LANES
NSA_COLS = NSA_W + 6 * NSA_KV_W + LANES


def _dot(a, b):
    return jnp.dot(a, b, preferred_element_type=F32)


def _dot_nt(a, b):
    return lax.dot_general(a, b, (((1,), (1,)), ((), ())), preferred_element_type=F32)


def _dot_tn(a, b):
    return lax.dot_general(a, b, (((0,), (0,)), ((), ())), preferred_element_type=F32)


def _split3(v):
    a = v.astype(BF16)
    r = v - a.astype(F32)
    b = r.astype(BF16)
    c = (r - b.astype(F32)).astype(BF16)
    return a, b, c


def _sigmoid(v):
    return 1.0 / (1.0 + jnp.exp(-v))


def _log_sigmoid(v):
    return jnp.minimum(v, 0.0) - jnp.log1p(jnp.exp(-jnp.abs(v)))


def _params(sem):
    return pltpu.CompilerParams(dimension_semantics=sem, vmem_limit_bytes=VMEM_LIMIT)


PROJ_TM = 512


def _proj_kernel(x_ref, g_ref, wml_ref, wnsa_ref, oml_ref, onsa_ref):
    x = x_ref[...]
    u = x * lax.rsqrt(jnp.mean(x * x, axis=-1, keepdims=True) + EPS) * g_ref[...]
    ub = u.astype(BF16)
    oml_ref[...] = _dot(ub, wml_ref[...])
    onsa_ref[...] = _dot(ub, wnsa_ref[...])


def _proj(x2, g_mix, w_ml, w_nsa):
    n = x2.shape[0]
    tm = min(PROJ_TM, n)
    return pl.pallas_call(
        _proj_kernel,
        out_shape=(jax.ShapeDtypeStruct((n, ML_COLS), F32), jax.ShapeDtypeStruct((n, NSA_COLS), F32)),
        grid=(n // tm,),
        in_specs=[pl.BlockSpec((tm, D_MODEL), lambda i: (i, 0)),
                  pl.BlockSpec((1, D_MODEL), lambda i: (0, 0)),
                  pl.BlockSpec((D_MODEL, ML_COLS), lambda i: (0, 0)),
                  pl.BlockSpec((D_MODEL, NSA_COLS), lambda i: (0, 0))],
        out_specs=(pl.BlockSpec((tm, ML_COLS), lambda i: (i, 0)),
                   pl.BlockSpec((tm, NSA_COLS), lambda i: (i, 0))),
        compiler_params=_params(("parallel",)),
        name="in_proj",
    )(x2, g_mix, w_ml, w_nsa)


ML_T = 256
CONV_PAD = 8


def _mlstm_kernel(q_ref, k_ref, v_ref, o_ref, g_ref, cw_ref, cb_ref, gb_ref, gout_ref, tri_ref, blk_ref,
                  out_ref, buf_ref, c_ref, n_ref, m_ref):
    T, L = ML_T, ML_CHUNK

    @pl.when(pl.program_id(1) == 0)
    def _():
        buf_ref[0:CONV_PAD, :] = jnp.zeros((CONV_PAD, 2 * ML_W), F32)
        c_ref[...] = jnp.zeros(c_ref.shape, F32)
        n_ref[...] = jnp.zeros(n_ref.shape, F32)
        m_ref[...] = jnp.zeros(m_ref.shape, F32)

    buf_ref[CONV_PAD:CONV_PAD + T, 0:ML_W] = q_ref[...]
    buf_ref[CONV_PAD:CONV_PAD + T, ML_W:2 * ML_W] = k_ref[...]
    y = cb_ref[...]
    for j in range(ML_CONV):
        lo = CONV_PAD - (ML_CONV - 1) + j
        y = y + cw_ref[j:j + 1, :] * buf_ref[lo:lo + T, :]
    buf_ref[0:CONV_PAD, :] = buf_ref[T:T + CONV_PAD, :]
    qk = y * _sigmoid(y)
    qf = qk[:, 0:ML_W] * (ML_DQK ** -0.5)
    kf = qk[:, ML_W:2 * ML_W]

    gg = g_ref[...] + gb_ref[...]
    gt = gg.T
    tri = tri_ref[...]
    blk = blk_ref[...]
    lf1, lf2, lf3 = _split3(_log_sigmoid(gg))
    b_col = _dot(tri, lf1) + _dot(tri, lf2) + _dot(tri, lf3)
    bl_col = _dot(blk, lf1) + _dot(blk, lf2) + _dot(blk, lf3)
    lt1, lt2, lt3 = _split3(_log_sigmoid(gt[0:8, :]))
    b_row = _dot_nt(lt1, tri) + _dot_nt(lt2, tri) + _dot_nt(lt3, tri)
    bl_row = _dot_nt(lt1, blk) + _dot_nt(lt2, blk) + _dot_nt(lt3, blk)

    r_i = lax.broadcasted_iota(jnp.int32, (T, T), 0)
    c_i = lax.broadcasted_iota(jnp.int32, (T, T), 1)
    causal = (r_i // L == c_i // L) & (c_i <= r_i)
    row_chunk = lax.broadcasted_iota(jnp.int32, (T, 1), 0) // L
    lane_chunk = lax.broadcasted_iota(jnp.int32, (1, T), 1) // L
    n_chunks = T // L

    for h in range(ML_HEADS):
        cols = slice(h * ML_DQK, (h + 1) * ML_DQK)
        f_lane = ML_HEADS + h
        bc, blc, igc = b_col[:, f_lane:f_lane + 1], bl_col[:, f_lane:f_lane + 1], gg[:, h:h + 1]
        br, blr, igr = b_row[f_lane:f_lane + 1, :], bl_row[f_lane:f_lane + 1, :], gt[h:h + 1, :]
        g_row = blr - br + igr
        g_col = blc - bc + igc

        m_prev = [m_ref[h][:, 0:1]]
        a_old, a_new, m_cs = [], [], []
        for c in range(n_chunks):
            m_c = jnp.max(jnp.where(lane_chunk == c, g_row, NEG), axis=1, keepdims=True)
            bl = blr[:, c * L:c * L + 1]
            m_next = jnp.maximum(bl + m_prev[c], m_c)
            a_old.append(jnp.exp(bl + m_prev[c] - m_next))
            a_new.append(jnp.exp(m_c - m_next))
            m_cs.append(m_c)
            m_prev.append(m_next)
        mprev_col = jnp.zeros((T, 1), F32)
        mc_col = jnp.zeros((T, 1), F32)
        for c in range(n_chunks):
            mprev_col = jnp.where(row_chunk == c, m_prev[c], mprev_col)
            mc_col = jnp.where(row_chunk == c, m_cs[c], mc_col)

        q_c = qf[:, cols]
        k_c = kf[:, cols]
        q_b = q_c.astype(BF16)
        v_b = v_ref[:, cols].astype(BF16)
        d_log = jnp.where(causal, bc - br + igr, NEG)
        a_log = bc + mprev_col
        m_t = jnp.maximum(a_log, jnp.max(d_log, axis=1, keepdims=True))
        wd = jnp.exp(d_log - m_t) * _dot_nt(q_b, k_c.astype(BF16))
        inter = jnp.exp(a_log - m_t)
        intra = _dot(wd.astype(BF16), v_b)
        wsum = jnp.sum(wd, axis=1, keepdims=True)
        kw = k_c * jnp.exp(g_col - mc_col)
        kw_b = kw.astype(BF16)

        c_st = c_ref[h]
        n_st = n_ref[h]
        qc_parts, qn_parts = [], []
        for c in range(n_chunks):
            rows = slice(c * L, (c + 1) * L)
            qc_parts.append(_dot(q_b[rows, :], c_st.astype(BF16)))
            qn_parts.append(jnp.sum(q_c[rows, :] * n_st, axis=1, keepdims=True))
            c_st = a_old[c] * c_st + a_new[c] * _dot_tn(kw_b[rows, :], v_b[rows, :])
            n_st = a_old[c] * n_st + a_new[c] * jnp.sum(kw[rows, :], axis=0, keepdims=True)
        c_ref[h] = c_st
        n_ref[h] = n_st
        m_ref[h] = jnp.broadcast_to(m_prev[n_chunks], (1, LANES))

        num = inter * jnp.concatenate(qc_parts, axis=0) + intra
        den = inter * jnp.concatenate(qn_parts, axis=0) + wsum
        hh = num / jnp.maximum(jnp.abs(den), jnp.exp(-m_t))
        hn = hh * lax.rsqrt(jnp.mean(hh * hh, axis=-1, keepdims=True) + EPS) * gout_ref[h:h + 1, :]
        out_ref[:, cols] = (hn * _sigmoid(o_ref[:, cols])).astype(BF16)


def _mlstm(ml, conv_w, conv_b, gate_b, g_out, bsz, seq):
    T = min(ML_T, seq)
    assert T == ML_T and seq % T == 0
    t_i = np.arange(T)
    blk = t_i[:, None] // ML_CHUNK == t_i[None, :] // ML_CHUNK
    tri = jnp.asarray(blk & (t_i[None, :] <= t_i[:, None]), BF16)
    blk = jnp.asarray(blk, BF16)
    wblk = lambda j: pl.BlockSpec((None, T, ML_W), lambda b, i, j=j: (b, i, j))
    full = lambda shape: pl.BlockSpec(shape, lambda b, i: (0,) * len(shape))
    return pl.pallas_call(
        _mlstm_kernel,
        out_shape=jax.ShapeDtypeStruct((bsz, seq, ML_W), BF16),
        grid=(bsz, seq // T),
        in_specs=[wblk(0), wblk(1), wblk(2), wblk(3),
                  pl.BlockSpec((None, T, LANES), lambda b, i: (b, i, 4 * ML_W // LANES)),
                  full((ML_CONV, 2 * ML_W)), full((1, 2 * ML_W)), full((1, LANES)),
                  full((ML_HEADS, ML_DQK)), full((T, T)), full((T, T))],
        out_specs=pl.BlockSpec((None, T, ML_W), lambda b, i: (b, i, 0)),
        scratch_shapes=[pltpu.VMEM((CONV_PAD + T, 2 * ML_W), F32),
                        pltpu.VMEM((ML_HEADS, ML_DQK, ML_DQK), F32),
                        pltpu.VMEM((ML_HEADS, 1, ML_DQK), F32),
                        pltpu.VMEM((ML_HEADS, 1, LANES), F32)],
        compiler_params=_params(("parallel", "arbitrary")),
        name="mlstm",
    )(ml, ml, ml, ml, ml, conv_w, conv_b, gate_b, g_out, tri, blk)


NSA_TP = 256
NSA_TQ = 256
NSA_OFF_KC, NSA_OFF_VC, NSA_OFF_KS, NSA_OFF_VS, NSA_OFF_KW, NSA_OFF_VW, NSA_OFF_GATE = (
    NSA_W + k * NSA_KV_W for k in range(7))


def _rope_tables(pos):
    half = NSA_DH // 2
    inv_freq = jnp.power(ROPE_THETA, -jnp.arange(half, dtype=F32) / half)
    ang = pos.astype(F32)[:, None] * inv_freq[None, :]
    cos, sin = jnp.cos(ang), jnp.sin(ang)
    cos = jnp.concatenate([cos, cos], axis=1)
    sin = jnp.concatenate([-sin, sin], axis=1)
    return jnp.tile(cos, (1, 2)), jnp.tile(sin, (1, 2))


def _head_norm_rope(xs, gsum, gain, cos, sin):
    a, b, c = _split3(xs * xs)
    ss = _dot(a, gsum) + _dot(b, gsum) + _dot(c, gsum)
    xn = xs * lax.rsqrt(ss * (1.0 / NSA_DH) + EPS) * gain
    lane = lax.broadcasted_iota(jnp.int32, xs.shape, 1)
    first = (lane % NSA_DH) < NSA_DH // 2
    swapped = jnp.where(first, pltpu.roll(xn, LANES - NSA_DH // 2, axis=1), pltpu.roll(xn, NSA_DH // 2, axis=1))
    return xn * cos + swapped * sin


def _head_norm_rope_t(xt, gsum, gain, cos_t, sin_t):
    a, b, c = _split3(xt * xt)
    ss = _dot(gsum, a) + _dot(gsum, b) + _dot(gsum, c)
    xn = xt * lax.rsqrt(ss * (1.0 / NSA_DH) + EPS) * gain
    row = lax.broadcasted_iota(jnp.int32, xt.shape, 0)
    first = (row % NSA_DH) < NSA_DH // 2
    swapped = jnp.where(first, pltpu.roll(xn, LANES - NSA_DH // 2, axis=0), pltpu.roll(xn, NSA_DH // 2, axis=0))
    return xn * cos_t + swapped * sin_t


def _store_keys(out, slab):
    lane = lax.broadcasted_iota(jnp.int32, slab.shape, 1)
    low = lane < NSA_DH
    rolled = pltpu.roll(slab, NSA_DH, axis=1)
    zero = jnp.zeros_like(slab)
    out[0, 0] = jnp.where(low, slab, zero).astype(BF16)
    out[0, 1] = jnp.where(low, zero, rolled).astype(BF16)
    out[1, 0] = jnp.where(low, rolled, zero).astype(BF16)
    out[1, 1] = jnp.where(low, zero, slab).astype(BF16)


def _store_values_t(out, slab):
    st = slab.T
    out[0] = st[0:NSA_DH, :].astype(BF16)
    out[1] = st[NSA_DH:2 * NSA_DH, :].astype(BF16)


def _nsa_prep_kernel(x_ref, cos_ref, sin_ref, cost_ref, sint_ref, gq_ref, gks_ref, gkw_ref, gsum_ref,
                     q_out, ks_out, vs_out, kw_out, vw_out, kc_out, vc_out):
    cos, sin, gsum = cos_ref[...], sin_ref[...], gsum_ref[...]
    for c in range(NSA_W // LANES):
        xt = x_ref[:, c * LANES:(c + 1) * LANES].T
        qn = _head_norm_rope_t(xt, gsum, gq_ref[...], cost_ref[...], sint_ref[...]) * (NSA_DH ** -0.5)
        q_out[c * LANES:(c + 1) * LANES, :] = qn.astype(BF16)
    ks = _head_norm_rope(x_ref[:, NSA_OFF_KS:NSA_OFF_KS + LANES], gsum, gks_ref[...], cos, sin)
    kw = _head_norm_rope(x_ref[:, NSA_OFF_KW:NSA_OFF_KW + LANES], gsum, gkw_ref[...], cos, sin)
    _store_keys(ks_out, ks)
    _store_keys(kw_out, kw)
    _store_values_t(vs_out, x_ref[:, NSA_OFF_VS:NSA_OFF_VS + LANES])
    _store_values_t(vw_out, x_ref[:, NSA_OFF_VW:NSA_OFF_VW + LANES])
    kc_out[...] = x_ref[:, NSA_OFF_KC:NSA_OFF_KC + LANES]
    vc_out[...] = x_ref[:, NSA_OFF_VC:NSA_OFF_VC + LANES]


def _nsa_cmp_kernel(xk_ref, xv_ref, pek_ref, pev_ref, w1ka_ref, w1kb_ref, w1va_ref, w1vb_ref,
                    b1k_ref, b1v_ref, w2k_ref, w2v_ref, gk_ref, cos_ref, sin_ref, gsum_ref,
                    kc_out, vc_out):
    rows = xk_ref.shape[0]

    def hidden(x, pe, wa, wb, b1):
        first = _dot((x + pe[0:1, :]).astype(BF16), wa[...])
        second = _dot((x + pe[1:2, :]).astype(BF16), wb[...])
        hid = first + pltpu.roll(second, rows - 1, axis=0) + b1[...]
        return (hid * _sigmoid(hid)).astype(BF16)

    kc = _dot(hidden(xk_ref[...], pek_ref, w1ka_ref, w1kb_ref, b1k_ref), w2k_ref[...])
    kc = _head_norm_rope(kc, gsum_ref[...], gk_ref[...], cos_ref[...], sin_ref[...])
    _store_keys(kc_out, kc)
    vc_t = _dot_nt(w2v_ref[...], hidden(xv_ref[...], pev_ref, w1va_ref, w1vb_ref, b1v_ref))
    vc_out[0] = vc_t[0:NSA_DH, :].astype(BF16)
    vc_out[1] = vc_t[NSA_DH:2 * NSA_DH, :].astype(BF16)


NSA_HEAD_SLOT = (0, 2, 1, 3)


def _stacked_scores(k_lo, k_hi, q_pair, bias):
    s = jnp.concatenate([_dot(k_lo, q_pair), _dot(k_hi, q_pair)], axis=1)
    return s + jnp.concatenate([bias] * NSA_GROUP, axis=1)


def _online_update(s, v_t, m_ref, l_ref, acc_ref):
    m_old = m_ref[...]
    m_new = jnp.maximum(m_old, jnp.max(s, axis=0, keepdims=True))
    alpha = jnp.exp(m_old - m_new)
    e = jnp.exp(s - jnp.maximum(m_new, 0.1 * NEG))
    l_ref[...] = alpha * l_ref[...] + jnp.sum(e, axis=0, keepdims=True)
    acc_ref[...] = alpha * acc_ref[...] + _dot(v_t, e.astype(BF16))
    m_ref[...] = m_new


def _nsa_attn_kernel(q_ref, kc_ref, vc_ref, ks_ref, vs_ref, kw_ref, vw_ref, gate_ref, mt_ref, esel_ref, eg_ref,
                     out_ref, ms_s, ls_s, accs_s, mw_s, lw_s, accw_s, *, n_cmp, n_sel):
    TQ = NSA_TQ
    i = pl.program_id(2)
    t0 = i * TQ
    t_row = t0 + lax.broadcasted_iota(jnp.int32, (1, TQ), 1)
    q_pair = jnp.concatenate([q_ref[0:LANES, :], q_ref[LANES:2 * LANES, :]], axis=1)

    rows = kc_ref.shape[1]
    n_i = lax.broadcasted_iota(jnp.int32, (rows, 1), 0)
    cmask = (n_i < n_cmp) & (n_i * CMP_STRIDE + (CMP_BLOCK - 1) <= t_row)
    s = _stacked_scores(kc_ref[0], kc_ref[1], q_pair, jnp.where(cmask, 0.0, NEG))
    e = jnp.exp(s - jnp.maximum(jnp.max(s, axis=0, keepdims=True), 0.1 * NEG))
    p = e / jnp.maximum(jnp.sum(e, axis=0, keepdims=True), 1e-30)
    o_cmp_t = _dot(vc_ref[...], p.astype(BF16))
    psum = p[:, 0:TQ] + p[:, TQ:2 * TQ] + p[:, 2 * TQ:3 * TQ] + p[:, 3 * TQ:4 * TQ]
    p1, p2, p3 = _split3(psum)
    imp = _dot(mt_ref[...], p1) + _dot(mt_ref[...], p2) + _dot(mt_ref[...], p3)
    j_i = lax.broadcasted_iota(jnp.int32, (n_sel, TQ), 0)
    cur = (t0 + lax.broadcasted_iota(jnp.int32, (n_sel, TQ), 1)) // SEL_BLOCK
    forced = (j_i == 0) | (j_i == cur) | (j_i == cur - 1)
    score = jnp.where(j_i > cur, NEG, imp[0:n_sel, :] + SEL_FORCE_BONUS * forced.astype(F32))
    rank = jnp.zeros((n_sel, TQ), jnp.int32)
    for jp in range(n_sel):
        row = score[jp:jp + 1, :]
        beats = (row > score) | ((row == score) & (j_i > jp))
        rank = rank + beats.astype(jnp.int32)
    sel_t = (rank < min(SEL_TOPN, n_sel)).astype(F32)
    sel_b = jnp.concatenate([sel_t, jnp.zeros((LANES - n_sel, TQ), F32)], axis=0).astype(BF16)

    for m_ref, l_ref, acc_ref in ((ms_s, ls_s, accs_s), (mw_s, lw_s, accw_s)):
        m_ref[...] = jnp.full(m_ref.shape, NEG, F32)
        l_ref[...] = jnp.zeros(l_ref.shape, F32)
        acc_ref[...] = jnp.zeros(acc_ref.shape, F32)

    def key_tile(jt, carry):
        k0 = pl.multiple_of(jt * TQ, TQ)
        diff = t_row - (k0 + lax.broadcasted_iota(jnp.int32, (TQ, 1), 0))
        causal = diff >= 0
        smask = (_dot(esel_ref[jt], sel_b) > 0.5) & causal
        s = _stacked_scores(ks_ref[0, pl.ds(k0, TQ), :], ks_ref[1, pl.ds(k0, TQ), :], q_pair,
                            jnp.where(smask, 0.0, NEG))
        _online_update(s, vs_ref[jt], ms_s, ls_s, accs_s)

        @pl.when(jt * TQ + TQ > t0 - WINDOW + 1)
        def _():
            wmask = causal & (diff < WINDOW)
            sw = _stacked_scores(kw_ref[0, pl.ds(k0, TQ), :], kw_ref[1, pl.ds(k0, TQ), :], q_pair,
                                 jnp.where(wmask, 0.0, NEG))
            _online_update(sw, vw_ref[jt], mw_s, lw_s, accw_s)
        return carry

    lax.fori_loop(0, i + 1, key_tile, 0)

    g1, g2, g3 = _split3(_sigmoid(gate_ref[...]))
    gexp = _dot(g1, eg_ref[...]) + _dot(g2, eg_ref[...]) + _dot(g3, eg_ref[...])
    half_w = NSA_GROUP * NSA_DH
    o_sel_t = accs_s[...] / jnp.maximum(ls_s[...], 1e-30)
    o_win_t = accw_s[...] / jnp.maximum(lw_s[...], 1e-30)

    def token_major(o_t, c):
        ce, co = NSA_HEAD_SLOT[2 * c] * TQ, NSA_HEAD_SLOT[2 * c + 1] * TQ
        return jnp.concatenate([o_t[:, ce:ce + TQ], o_t[:, co:co + TQ]], axis=0).T

    for c in range(NSA_GROUP // 2):
        o_cmp = token_major(o_cmp_t, c)
        o_sel = token_major(o_sel_t, c)
        o_win = token_major(o_win_t, c)
        cs = slice(c * LANES, (c + 1) * LANES)
        o = (gexp[:, c * LANES:(c + 1) * LANES] * o_cmp
             + gexp[:, half_w + c * LANES:half_w + (c + 1) * LANES] * o_sel
             + gexp[:, 2 * half_w + c * LANES:2 * half_w + (c + 1) * LANES] * o_win)
        out_ref[:, cs] = o.astype(BF16)


def _nsa_constants(seq):
    n_cmp = (seq - CMP_BLOCK) // CMP_STRIDE + 1
    n_sel = seq // SEL_BLOCK
    rows = seq // CMP_STRIDE
    n = np.arange(rows)[None, :]
    j = np.arange(LANES)[:, None]
    mt = ((n * CMP_STRIDE < j * SEL_BLOCK + SEL_BLOCK) & (n * CMP_STRIDE + CMP_BLOCK > j * SEL_BLOCK)
          & (n < n_cmp) & (j < n_sel))
    s = np.arange(seq)
    esel = (s[:, None] // SEL_BLOCK == np.arange(LANES)[None, :])
    esel = esel.reshape(seq // NSA_TQ, NSA_TQ, LANES)
    eg = np.zeros((NSA_KVH, LANES, N_BRANCH * NSA_GROUP * NSA_DH), np.float32)
    for h in range(NSA_KVH):
        for g in range(NSA_GROUP):
            for br in range(N_BRANCH):
                lane = (h * NSA_GROUP + g) * N_BRANCH + br
                c0 = br * NSA_GROUP * NSA_DH + g * NSA_DH
                eg[h, lane, c0:c0 + NSA_DH] = 1.0
    gsum = np.kron(np.eye(LANES // NSA_DH), np.ones((NSA_DH, NSA_DH)))
    return (n_cmp, n_sel, jnp.asarray(mt, BF16), jnp.asarray(esel, BF16), jnp.asarray(eg, BF16),
            jnp.asarray(gsum, BF16))


def _nsa(nsa, w, bsz, seq):
    assert seq % NSA_TQ == 0 and seq % NSA_TP == 0
    n_cmp, n_sel, mt, esel, eg, gsum = _nsa_constants(seq)
    rows = seq // CMP_STRIDE
    cos, sin = _rope_tables(jnp.arange(seq))
    cos_c, sin_c = _rope_tables(jnp.arange(rows) * CMP_STRIDE + (CMP_BLOCK - 1))
    c2 = lambda shape: pl.BlockSpec(shape, lambda b, i: (0,) * len(shape))
    n_tiles = seq // NSA_TP
    k_shape = jax.ShapeDtypeStruct((bsz, NSA_KVH, 2, seq, LANES), BF16)
    k_spec = pl.BlockSpec((None, NSA_KVH, 2, NSA_TP, LANES), lambda b, i: (b, 0, 0, i, 0))
    v_shape = jax.ShapeDtypeStruct((bsz, NSA_KVH, n_tiles, NSA_DH, NSA_TP), BF16)
    v_spec = pl.BlockSpec((None, NSA_KVH, None, NSA_DH, NSA_TP), lambda b, i: (b, 0, i, 0, 0))
    raw_shape = jax.ShapeDtypeStruct((bsz, seq, LANES), F32)
    raw_spec = pl.BlockSpec((None, NSA_TP, LANES), lambda b, i: (b, i, 0))
    qn, ksd, vsd, kwd, vwd, kc_raw, vc_raw = pl.pallas_call(
        _nsa_prep_kernel,
        out_shape=(jax.ShapeDtypeStruct((bsz, NSA_W, seq), BF16), k_shape, v_shape, k_shape, v_shape,
                   raw_shape, raw_shape),
        grid=(bsz, n_tiles),
        in_specs=[pl.BlockSpec((None, NSA_TP, NSA_COLS), lambda b, i: (b, i, 0)),
                  pl.BlockSpec((NSA_TP, LANES), lambda b, i: (i, 0)),
                  pl.BlockSpec((NSA_TP, LANES), lambda b, i: (i, 0)),
                  pl.BlockSpec((LANES, NSA_TP), lambda b, i: (0, i)),
                  pl.BlockSpec((LANES, NSA_TP), lambda b, i: (0, i)),
                  c2((LANES, 1)), c2((1, LANES)), c2((1, LANES)), c2((LANES, LANES))],
        out_specs=(pl.BlockSpec((None, NSA_W, NSA_TP), lambda b, i: (b, 0, i)),
                   k_spec, v_spec, k_spec, v_spec, raw_spec, raw_spec),
        compiler_params=_params(("parallel", "parallel")),
        name="nsa_prep",
    )(nsa, cos, sin, cos.T, sin.T, w["nsa_gq"].reshape(LANES, 1), w["nsa_gks"], w["nsa_gkw"], gsum)

    wide = CMP_STRIDE * LANES
    xk = kc_raw.reshape(bsz, rows, wide)
    xv = vc_raw.reshape(bsz, rows, wide)
    c1 = lambda shape: pl.BlockSpec(shape, lambda b: (0,) * len(shape))
    kc_shape = jax.ShapeDtypeStruct((bsz, NSA_KVH, 2, rows, LANES), BF16)
    kc_spec = pl.BlockSpec((None, NSA_KVH, 2, rows, LANES), lambda b: (b, 0, 0, 0, 0))
    vc_shape = jax.ShapeDtypeStruct((bsz, NSA_KVH, NSA_DH, rows), BF16)
    vc_spec = pl.BlockSpec((None, NSA_KVH, NSA_DH, rows), lambda b: (b, 0, 0, 0))
    hid2 = NSA_KVH * CMP_HIDDEN
    kcd, vcd = pl.pallas_call(
        _nsa_cmp_kernel,
        out_shape=(kc_shape, vc_shape),
        grid=(bsz,),
        in_specs=[pl.BlockSpec((None, rows, wide), lambda b: (b, 0, 0)),
                  pl.BlockSpec((None, rows, wide), lambda b: (b, 0, 0)),
                  c1((2, wide)), c1((2, wide)),
                  c1((wide, hid2)), c1((wide, hid2)), c1((wide, hid2)), c1((wide, hid2)),
                  c1((1, hid2)), c1((1, hid2)), c1((hid2, LANES)), c1((LANES, hid2)),
                  c1((1, LANES)), c1((rows, LANES)), c1((rows, LANES)), c1((LANES, LANES))],
        out_specs=(kc_spec, vc_spec),
        compiler_params=_params(("parallel",)),
        name="nsa_compress",
    )(xk, xv, w["cmp_pe_k"], w["cmp_pe_v"], w["cmp_w1k_a"], w["cmp_w1k_b"], w["cmp_w1v_a"], w["cmp_w1v_b"],
      w["cmp_b1_k"], w["cmp_b1_v"], w["cmp_w2_k"], w["cmp_w2_v"].T, w["nsa_gkc"], cos_c, sin_c, gsum)

    TQ = NSA_TQ
    half_w = NSA_GROUP * NSA_DH
    c3 = lambda shape: pl.BlockSpec(shape, lambda b, h, i: (0,) * len(shape))
    assert NSA_TP == TQ
    keys = lambda r: pl.BlockSpec((None, None, 2, r, LANES), lambda b, h, i: (b, h, 0, 0, 0))
    vals = pl.BlockSpec((None, None, seq // TQ, NSA_DH, TQ), lambda b, h, i: (b, h, 0, 0, 0))
    stat = lambda: pltpu.VMEM((1, NSA_GROUP * TQ), F32)
    accu = lambda: pltpu.VMEM((NSA_DH, NSA_GROUP * TQ), F32)
    return pl.pallas_call(
        functools.partial(_nsa_attn_kernel, n_cmp=n_cmp, n_sel=n_sel),
        out_shape=jax.ShapeDtypeStruct((bsz, seq, NSA_W), BF16),
        grid=(bsz, NSA_KVH, seq // TQ),
        in_specs=[pl.BlockSpec((None, half_w, TQ), lambda b, h, i: (b, h, i)),
                  keys(rows), pl.BlockSpec((None, None, NSA_DH, rows), lambda b, h, i: (b, h, 0, 0)),
                  keys(seq), vals, keys(seq), vals,
                  pl.BlockSpec((None, TQ, LANES), lambda b, h, i: (b, i, NSA_OFF_GATE // LANES)),
                  c3((LANES, rows)), c3((seq // TQ, TQ, LANES)),
                  pl.BlockSpec((None, LANES, N_BRANCH * half_w), lambda b, h, i: (h, 0, 0))],
        out_specs=pl.BlockSpec((None, TQ, half_w), lambda b, h, i: (b, i, h)),
        scratch_shapes=[stat(), stat(), accu(), stat(), stat(), accu()],
        compiler_params=_params(("parallel", "parallel", "arbitrary")),
        name="nsa_attention",
    )(qn, kcd, vcd, ksd, vsd, kwd, vwd, nsa, mt, esel, eg)


POST_TM = 256


def _rms(v, gain):
    return v * lax.rsqrt(jnp.mean(v * v, axis=-1, keepdims=True) + EPS) * gain


def _memkv_kernel(mem_ref, g_ref, w_ref, gk_ref, k_out, v_out):
    u = _rms(mem_ref[...], g_ref[...]).astype(BF16)
    kv = _dot(u, w_ref[...])
    for h in range(MEM_HEADS):
        cs = slice(h * MEM_DH, (h + 1) * MEM_DH)
        k_out[:, cs] = _rms(kv[:, cs], gk_ref[...]).astype(BF16)
    v_out[...] = kv[:, MEM_W:2 * MEM_W].astype(BF16)


def _memkv(mem, g_mem, w_mem_kv, g_k):
    bsz, m_len, _ = mem.shape
    c1 = lambda shape: pl.BlockSpec(shape, lambda b: (0,) * len(shape))
    shape = jax.ShapeDtypeStruct((bsz, m_len, MEM_W), BF16)
    spec = pl.BlockSpec((None, m_len, MEM_W), lambda b: (b, 0, 0))
    return pl.pallas_call(
        _memkv_kernel, out_shape=(shape, shape), grid=(bsz,),
        in_specs=[pl.BlockSpec((None, m_len, D_MODEL), lambda b: (b, 0, 0)),
                  c1((1, D_MODEL)), c1((D_MODEL, 2 * MEM_W)), c1((1, MEM_DH))],
        out_specs=(spec, spec),
        compiler_params=_params(("parallel",)),
        name="mem_kv",
    )(mem, g_mem, w_mem_kv, g_k)


def _post_kernel(x_ref, hml_ref, hnsa_ref, mk_ref, mv_ref, gmix_ref, wmq_ref, gmq_ref, wgate_ref,
                 wml_ref, wnsa_ref, wmem_ref, wout_ref, gffn_ref, wrh_ref, wrl_ref, br_ref,
                 x1_out, t_out, logit_out):
    x = x_ref[...]
    ub = _rms(x, gmix_ref[...]).astype(BF16)
    mq = _dot(ub, wmq_ref[...])
    heads = []
    for h in range(MEM_HEADS):
        cs = slice(h * MEM_DH, (h + 1) * MEM_DH)
        qn = (_rms(mq[:, cs], gmq_ref[...]) * (MEM_DH ** -0.5)).astype(BF16)
        s = _dot_nt(qn, mk_ref[:, cs])
        e = jnp.exp(s - jnp.max(s, axis=1, keepdims=True))
        p = e / jnp.sum(e, axis=1, keepdims=True)
        heads.append(_dot(p.astype(BF16), mv_ref[:, cs]))
    hmem = jnp.concatenate(heads, axis=1).astype(BF16)
    gates = _sigmoid(_dot(ub, wgate_ref[...]))
    merged = (gates[:, 0:D_MODEL] * _dot(hml_ref[...], wml_ref[...])
              + gates[:, D_MODEL:2 * D_MODEL] * _dot(hnsa_ref[...], wnsa_ref[...])
              + gates[:, 2 * D_MODEL:3 * D_MODEL] * _dot(hmem, wmem_ref[...]))
    x1 = x + _dot(merged.astype(BF16), wout_ref[...])
    x1_out[...] = x1
    t = _rms(x1, gffn_ref[...])
    t_out[...] = t
    t1, t2, _ = _split3(t)
    logit_out[...] = _dot(t1, wrh_ref[...]) + _dot(t1, wrl_ref[...]) + _dot(t2, wrh_ref[...]) + br_ref[...]


def _post(x2, hml, hnsa, mk, mv, w, seq):
    n = x2.shape[0]
    tm = POST_TM
    assert seq % tm == 0
    per = seq // tm
    m_len = mk.shape[1]
    row = lambda width: pl.BlockSpec((tm, width), lambda i: (i, 0))
    c1 = lambda shape: pl.BlockSpec(shape, lambda i: (0,) * len(shape))
    memspec = pl.BlockSpec((None, m_len, MEM_W), lambda i: (i // per, 0, 0))
    return pl.pallas_call(
        _post_kernel,
        out_shape=(jax.ShapeDtypeStruct((n, D_MODEL), F32), jax.ShapeDtypeStruct((n, D_MODEL), F32),
                   jax.ShapeDtypeStruct((n, LANES), F32)),
        grid=(n // tm,),
        in_specs=[row(D_MODEL), row(ML_W), row(NSA_W), memspec, memspec,
                  c1((1, D_MODEL)), c1((D_MODEL, MEM_W)), c1((1, MEM_DH)), c1((D_MODEL, N_BRANCH * D_MODEL)),
                  c1((ML_W, D_MODEL)), c1((NSA_W, D_MODEL)), c1((MEM_W, D_MODEL)), c1((D_MODEL, D_MODEL)),
                  c1((1, D_MODEL)), c1((D_MODEL, LANES)), c1((D_MODEL, LANES)), c1((1, LANES))],
        out_specs=(row(D_MODEL), row(D_MODEL), row(LANES)),
        compiler_params=_params(("parallel",)),
        name="post",
    )(x2, hml, hnsa, mk, mv, w["g_mix"], w["w_mq"], w["mem_gq"], w["w_gate"], w["w_br_ml"], w["w_br_nsa"],
      w["w_br_mem"], w["w_out"], w["g_ffn"], w["w_router_hi"], w["w_router_lo"], w["b_router"])


ROUTE_T = 512
MOE_BLK = 256
MOE_TD = 256
MOE_DEPTH = 3


def _route_kernel(logit_ref, e_out, wtm_out, cnt_out):
    T = ROUTE_T

    @pl.when(pl.program_id(0) == 0)
    def _():
        cnt_out[...] = jnp.zeros(cnt_out.shape, F32)

    sc = logit_ref[...].T[0:N_EXPERTS, :]
    j_i = lax.broadcasted_iota(jnp.int32, (N_EXPERTS, T), 0)
    rank = jnp.zeros((N_EXPERTS, T), jnp.int32)
    for jp in range(N_EXPERTS):
        row = sc[jp:jp + 1, :]
        beats = (row > sc) | ((row == sc) & (j_i > jp))
        rank = rank + beats.astype(jnp.int32)
    sel = rank < TOP_K
    e = jnp.where(sel, jnp.exp(sc - jnp.max(sc, axis=0, keepdims=True)), 0.0)
    wgt = e / jnp.sum(e, axis=0, keepdims=True)
    cnt_out[...] = cnt_out[...] + jnp.sum(sel.astype(F32), axis=1, keepdims=True)
    e_rows, w_rows = [], []
    for r in range(TOP_K):
        hit = rank == r
        e_rows.append(jnp.sum(jnp.where(hit, j_i, 0), axis=0, keepdims=True))
        w_rows.append(jnp.sum(jnp.where(hit, wgt, 0.0), axis=0, keepdims=True))
    e_out[...] = jnp.concatenate(e_rows, axis=0)
    w4 = jnp.concatenate(w_rows, axis=0)
    wtm_out[...] = jnp.concatenate([w4, jnp.zeros((LANES - TOP_K, T), F32)], axis=0).T


def _route(logits):
    n = logits.shape[0]
    T = ROUTE_T
    assert n % T == 0
    return pl.pallas_call(
        _route_kernel,
        out_shape=(jax.ShapeDtypeStruct((TOP_K, n), jnp.int32), jax.ShapeDtypeStruct((n, LANES), F32),
                   jax.ShapeDtypeStruct((N_EXPERTS, LANES), F32)),
        grid=(n // T,),
        in_specs=[pl.BlockSpec((T, LANES), lambda i: (i, 0))],
        out_specs=(pl.BlockSpec((TOP_K, T), lambda i: (0, i)), pl.BlockSpec((T, LANES), lambda i: (i, 0)),
                   pl.BlockSpec((N_EXPERTS, LANES), lambda i: (0, 0))),
        compiler_params=_params(("arbitrary",)),
        name="moe_route",
    )(logits)


def _ffn_fused_kernel(be_ref, base_ref, nv_ref, order_ref, t_hbm, wgu_ref, bgu_ref, wd_ref, bd_ref, yk_hbm,
                      xbuf, ybuf, gsem, ssem, wgu_s, wd_s, *, n_tok):
    i = pl.program_id(0)
    nb = pl.num_programs(0)
    slot = i % MOE_DEPTH
    last_id = n_tok * TOP_K - 1
    dump0 = n_tok * TOP_K

    def start_gather(blk, buf):
        b0 = base_ref[blk]
        for j in range(MOE_BLK):
            tok = order_ref[jnp.minimum(b0 + j, last_id)] & (n_tok - 1)
            pltpu.make_async_copy(t_hbm.at[pl.ds(tok, 1)], xbuf.at[buf, pl.ds(j, 1)], gsem.at[buf]).start()

    def start_scatter(blk, nv, buf):
        b0 = base_ref[blk]
        for j in range(MOE_BLK):
            dst = jnp.where(j < nv, order_ref[jnp.minimum(b0 + j, last_id)], dump0 + buf * MOE_BLK + j)
            pltpu.make_async_copy(ybuf.at[buf, pl.ds(j, 1)], yk_hbm.at[pl.ds(dst, 1)], ssem.at[buf]).start(priority=1)

    def wait_gather(buf):
        pltpu.make_async_copy(t_hbm.at[pl.ds(0, MOE_BLK)], xbuf.at[buf], gsem.at[buf]).wait()

    def wait_scatter(buf):
        pltpu.make_async_copy(ybuf.at[buf], yk_hbm.at[pl.ds(0, MOE_BLK)], ssem.at[buf]).wait()

    @pl.when(i == 0)
    def _():
        xbuf[...] = jnp.zeros(xbuf.shape, F32)
        ybuf[...] = jnp.zeros(ybuf.shape, F32)
        for q in range(MOE_DEPTH - 1):
            start_gather(jnp.minimum(q, nb - 1), q)
            start_scatter(0, 0, q)

    @pl.when((i == 0) | (be_ref[i] != be_ref[jnp.maximum(i - 1, 0)]))
    def _():
        wgu_s[...] = wgu_ref[...].astype(BF16)
        bits = lax.shift_right_logical(pltpu.bitcast(wd_ref[...].astype(BF16).astype(F32), jnp.uint32),
                                       jnp.uint32(16))
        wd_s[...] = bits | lax.shift_left(bits, jnp.uint32(16))

    ahead = (i + MOE_DEPTH - 1) % MOE_DEPTH
    wait_gather(slot)
    start_gather(jnp.minimum(i + MOE_DEPTH - 1, nb - 1), ahead)
    prev = jnp.maximum(i - 1, 0)
    start_scatter(prev, jnp.where(i == 0, 0, nv_ref[prev]), ahead)
    hgu = _dot(xbuf[slot].astype(BF16), wgu_s[...]) + bgu_ref[...]
    gate = jnp.minimum(hgu, SWIGLU_LIMIT)
    up1 = jnp.clip(hgu, -SWIGLU_LIMIT, SWIGLU_LIMIT) + 1.0
    even = lax.broadcasted_iota(jnp.int32, (1, 2 * D_FF), 1) % 2 == 0
    act = jnp.where(even, gate * _sigmoid(SWIGLU_ALPHA * gate) * pltpu.roll(up1, 2 * D_FF - 1, axis=1), 0.0)
    y = _dot(act.astype(BF16), pltpu.bitcast(wd_s[...], BF16)) + bd_ref[...]
    wait_scatter(slot)
    ybuf[slot] = y

    @pl.when(i == nb - 1)
    def _():
        start_scatter(i, nv_ref[i], slot)
        wait_scatter(slot)
        for q in range(1, MOE_DEPTH):
            other = (i + q) % MOE_DEPTH
            wait_gather(other)
            wait_scatter(other)


def _ffn_fused(t, blk_e, base, n_valid, order, w):
    n = t.shape[0]
    nb = blk_e.shape[0]
    espec = lambda r, c: pl.BlockSpec((None, r, c), lambda i, be, *_: (be[i], 0, 0))
    return pl.pallas_call(
        functools.partial(_ffn_fused_kernel, n_tok=n),
        out_shape=jax.ShapeDtypeStruct((n * TOP_K + MOE_DEPTH * MOE_BLK, D_MODEL), F32),
        grid_spec=pltpu.PrefetchScalarGridSpec(
            num_scalar_prefetch=4, grid=(nb,),
            in_specs=[pl.BlockSpec(memory_space=pl.ANY), espec(D_MODEL, 2 * D_FF), espec(1, 2 * D_FF),
                      espec(D_FF, D_MODEL), espec(1, D_MODEL)],
            out_specs=pl.BlockSpec(memory_space=pl.ANY),
            scratch_shapes=[pltpu.VMEM((MOE_DEPTH, MOE_BLK, D_MODEL), F32),
                            pltpu.VMEM((MOE_DEPTH, MOE_BLK, D_MODEL), F32),
                            pltpu.SemaphoreType.DMA((MOE_DEPTH,)), pltpu.SemaphoreType.DMA((MOE_DEPTH,)),
                            pltpu.VMEM((D_MODEL, 2 * D_FF), BF16), pltpu.VMEM((D_FF, D_MODEL), jnp.uint32)]),
        compiler_params=_params(("arbitrary",)),
        name="moe_ffn",
    )(blk_e, base, n_valid, order, t, w["w_gu"], w["b_gu"], w["w_down"], w["b_down"])


def _moe_sum_kernel(x1_ref, wtm_ref, y0_ref, y1_ref, y2_ref, y3_ref, out_ref):
    wtm = wtm_ref[...]
    acc = x1_ref[...]
    for r, y_ref in enumerate((y0_ref, y1_ref, y2_ref, y3_ref)):
        acc = acc + wtm[:, r:r + 1] * y_ref[...]
    out_ref[...] = acc


def _moe_sum(x1, wtm, yk):
    n = x1.shape[0]
    per = n // MOE_TD
    row = lambda width: pl.BlockSpec((MOE_TD, width), lambda i: (i, 0))
    ysp = lambda r: pl.BlockSpec((MOE_TD, D_MODEL), lambda i, r=r: (r * per + i, 0))
    return pl.pallas_call(
        _moe_sum_kernel,
        out_shape=jax.ShapeDtypeStruct((n, D_MODEL), F32),
        grid=(per,),
        in_specs=[row(D_MODEL), row(LANES), ysp(0), ysp(1), ysp(2), ysp(3)],
        out_specs=row(D_MODEL),
        compiler_params=_params(("parallel",)),
        name="moe_sum",
    )(x1, wtm, yk, yk, yk, yk)


def _moe(x1, t, logits, w):
    n = x1.shape[0]
    e_idx, wtm, cnt = _route(logits)
    counts = cnt[:, 0].astype(jnp.int32)
    padded = (counts + MOE_BLK - 1) // MOE_BLK * MOE_BLK
    pad_end = jnp.cumsum(padded)
    pad_start = pad_end - padded
    start = jnp.cumsum(counts) - counts
    nb = -(-(n * TOP_K + N_EXPERTS * (MOE_BLK - 1)) // MOE_BLK)
    first = jnp.arange(nb, dtype=jnp.int32) * MOE_BLK
    blk_e = jnp.minimum(jnp.sum((pad_end[None, :] <= first[:, None]).astype(jnp.int32), axis=1), N_EXPERTS - 1)
    within = first - pad_start[blk_e]
    n_valid = jnp.clip(counts[blk_e] - within, 0, MOE_BLK).astype(jnp.int32)
    base = (start[blk_e] + within).astype(jnp.int32)
    assert n & (n - 1) == 0
    flat = jnp.argsort(e_idx.T.reshape(-1), stable=True).astype(jnp.int32)
    order = (flat % TOP_K) * n + flat // TOP_K
    yk = _ffn_fused(t, blk_e, base, n_valid, order, w)
    return _moe_sum(x1, wtm, yk)


def _cmp_weights(pos, w1, b1, w2):
    eye = jnp.eye(NSA_KVH, dtype=F32)
    halves = []
    for part in (w1[:CMP_STRIDE], w1[CMP_STRIDE:]):
        big = jnp.einsum("ldf,hg->lhdgf", part, eye)
        halves.append(big.reshape(CMP_STRIDE * LANES, NSA_KVH * CMP_HIDDEN).astype(BF16))
    pe = jnp.stack([jnp.broadcast_to(part[:, None, :], (CMP_STRIDE, NSA_KVH, NSA_DH)).reshape(-1)
                    for part in (pos[:CMP_STRIDE], pos[CMP_STRIDE:])])
    w2big = jnp.einsum("fd,hg->hfgd", w2, eye).reshape(NSA_KVH * CMP_HIDDEN, LANES).astype(BF16)
    return halves[0], halves[1], pe, jnp.tile(b1, NSA_KVH).reshape(1, -1), w2big


def _prep_weights(p):
    w = {}
    offs = np.concatenate([[0], np.cumsum(IN_SPLITS)])
    col = lambda k: p["w_in"][:, offs[k]:offs[k + 1]]
    zpad = lambda n: jnp.zeros((D_MODEL, n), F32)
    w["w_ml"] = jnp.concatenate([col(0), col(1), col(2), col(3), col(4), col(5),
                                 zpad(LANES - 2 * ML_HEADS)], axis=1).astype(BF16)
    w["w_nsa"] = jnp.concatenate([col(k) for k in range(6, 14)] + [zpad(LANES - NSA_QH * 3)], axis=1).astype(BF16)
    w["g_mix"] = p["g_mix"].reshape(1, D_MODEL)
    w["conv_w"] = p["ml_conv_w"]
    w["conv_b"] = p["ml_conv_b"].reshape(1, 2 * ML_W)
    w["ml_gate_b"] = jnp.concatenate([p["ml_b_i"], p["ml_b_f"], jnp.zeros((LANES - 2 * ML_HEADS,), F32)]).reshape(1, LANES)
    w["ml_g_out"] = p["ml_g_out"]

    pair = lambda g: jnp.tile(g, LANES // NSA_DH).reshape(1, LANES)
    w["nsa_gq"] = pair(p["nsa_g_q"])
    w["nsa_gkc"], w["nsa_gks"], w["nsa_gkw"] = (pair(p["nsa_g_k"][k]) for k in range(3))
    (w["cmp_w1k_a"], w["cmp_w1k_b"], w["cmp_pe_k"], w["cmp_b1_k"], w["cmp_w2_k"]) = _cmp_weights(
        p["cmp_pos_k"], p["cmp_w1_k"], p["cmp_b1_k"], p["cmp_w2_k"])
    (w["cmp_w1v_a"], w["cmp_w1v_b"], w["cmp_pe_v"], w["cmp_b1_v"], w["cmp_w2_v"]) = _cmp_weights(
        p["cmp_pos_v"], p["cmp_w1_v"], p["cmp_b1_v"], p["cmp_w2_v"])

    w["g_mem"] = p["g_mem"].reshape(1, D_MODEL)
    w["w_mem_kv"] = p["w_mem_kv"].astype(BF16)
    w["mem_gq"] = p["mem_g_q"].reshape(1, MEM_DH)
    w["mem_gk"] = p["mem_g_k"].reshape(1, MEM_DH)
    w["w_mq"] = col(14).astype(BF16)
    w["w_gate"] = col(15).astype(BF16)
    for k in ("w_br_ml", "w_br_nsa", "w_br_mem", "w_out"):
        w[k] = p[k].astype(BF16)
    w["g_ffn"] = p["g_ffn"].reshape(1, D_MODEL)
    w_r = jnp.concatenate([p["w_router"], jnp.zeros((D_MODEL, LANES - N_EXPERTS), F32)], axis=1)
    w["w_router_hi"] = w_r.astype(BF16)
    w["w_router_lo"] = (w_r - w["w_router_hi"].astype(F32)).astype(BF16)
    w["b_router"] = jnp.concatenate([p["b_router"], jnp.zeros((LANES - N_EXPERTS,), F32)]).reshape(1, LANES)
    w["w_gu"] = p["w_gu"]
    w["b_gu"] = p["b_gu"][:, None, :]
    w["w_down"] = p["w_down"]
    w["b_down"] = p["b_down"][:, None, :]
    return w


_PARAM_NAMES = ("g_mix", "w_in", "ml_conv_w", "ml_conv_b", "ml_b_i", "ml_b_f", "ml_g_out", "nsa_g_q", "nsa_g_k",
                "cmp_pos_k", "cmp_w1_k", "cmp_b1_k", "cmp_w2_k", "cmp_pos_v", "cmp_w1_v", "cmp_b1_v", "cmp_w2_v",
                "g_mem", "w_mem_kv", "mem_g_q", "mem_g_k", "w_br_ml", "w_br_nsa", "w_br_mem", "w_out", "g_ffn",
                "w_router", "b_router", "w_gu", "b_gu", "w_down", "b_down")


def _layer(x, mem, p):
    bsz, seq, d = x.shape
    w = _prep_weights(p)
    x2 = x.reshape(bsz * seq, d)
    ml, nsa = _proj(x2, w["g_mix"], w["w_ml"], w["w_nsa"])
    h_ml = _mlstm(ml.reshape(bsz, seq, ML_COLS), w["conv_w"], w["conv_b"], w["ml_gate_b"], w["ml_g_out"], bsz, seq)
    h_nsa = _nsa(nsa.reshape(bsz, seq, NSA_COLS), w, bsz, seq)
    mk, mv = _memkv(mem, w["g_mem"], w["w_mem_kv"], w["mem_gk"])
    x1, t, logits = _post(x2, h_ml.reshape(bsz * seq, ML_W), h_nsa.reshape(bsz * seq, NSA_W), mk, mv, w, seq)
    return _moe(x1, t, logits, w).reshape(bsz, seq, d)


def kernel(x, mem, g_mix, w_in, ml_conv_w, ml_conv_b, ml_b_i, ml_b_f, ml_g_out, nsa_g_q, nsa_g_k, cmp_pos_k,
           cmp_w1_k, cmp_b1_k, cmp_w2_k, cmp_pos_v, cmp_w1_v, cmp_b1_v, cmp_w2_v, g_mem, w_mem_kv, mem_g_q,
           mem_g_k, w_br_ml, w_br_nsa, w_br_mem, w_out, g_ffn, w_router, b_router, w_gu, b_gu, w_down, b_down):
    stacked = (g_mix, w_in, ml_conv_w, ml_conv_b, ml_b_i, ml_b_f, ml_g_out, nsa_g_q, nsa_g_k, cmp_pos_k,
               cmp_w1_k, cmp_b1_k, cmp_w2_k, cmp_pos_v, cmp_w1_v, cmp_b1_v, cmp_w2_v, g_mem, w_mem_kv, mem_g_q,
               mem_g_k, w_br_ml, w_br_nsa, w_br_mem, w_out, g_ffn, w_router, b_router, w_gu, b_gu, w_down, b_down)
    h = x
    for layer in range(g_mix.shape[0]):
        h = _layer(h, mem, {name: v[layer] for name, v in zip(_PARAM_NAMES, stacked)})
    return h
```

```python
import functools

import jax
import jax.numpy as jnp
import numpy as np
from jax import lax
from jax.experimental import pallas as pl
from jax.experimental.pallas import tpu as pltpu

F32 = jnp.float32
BF16 = jnp.bfloat16

D_MODEL = 1024
ML_HEADS = 4
ML_DQK = 128
ML_W = ML_HEADS * ML_DQK
ML_CHUNK = 64
ML_CONV = 4
NSA_QH = 8
NSA_KVH = 2
NSA_DH = 64
NSA_GROUP = NSA_QH // NSA_KVH
NSA_W = NSA_QH * NSA_DH
NSA_KV_W = NSA_KVH * NSA_DH
CMP_BLOCK = 32
CMP_STRIDE = 16
CMP_HIDDEN = 256
SEL_BLOCK = 64
SEL_TOPN = 8
SEL_FORCE_BONUS = 1e4
WINDOW = 512
MEM_HEADS = 4
MEM_DH = 128
MEM_W = MEM_HEADS * MEM_DH
N_EXPERTS = 32
TOP_K = 4
D_FF = D_MODEL
SWIGLU_ALPHA = 1.702
SWIGLU_LIMIT = 7.0
N_BRANCH = 3
ROPE_THETA = 10000.0
EPS = 1e-6
NEG = -1e30

LANES = 128
VMEM_LIMIT = 56 * 1024 * 1024

IN_SPLITS = (ML_W, ML_W, ML_W, ML_W, ML_HEADS, ML_HEADS,
             NSA_W, NSA_KV_W, NSA_KV_W, NSA_KV_W, NSA_KV_W, NSA_KV_W, NSA_KV_W, NSA_QH * 3,
             MEM_W, N_BRANCH * D_MODEL)

ML_COLS = 4 * ML_W + LANES
NSA_COLS = NSA_W + 6 * NSA_KV_W + LANES


def _dot(a, b):
    return jnp.dot(a, b, preferred_element_type=F32)


def _dot_nt(a, b):
    return lax.dot_general(a, b, (((1,), (1,)), ((), ())), preferred_element_type=F32)


def _dot_tn(a, b):
    return lax.dot_general(a, b, (((0,), (0,)), ((), ())), preferred_element_type=F32)


def _split3(v):
    a = v.astype(BF16)
    r = v - a.astype(F32)
    b = r.astype(BF16)
    c = (r - b.astype(F32)).astype(BF16)
    return a, b, c


def _sigmoid(v):
    return 1.0 / (1.0 + jnp.exp(-v))


def _log_sigmoid(v):
    return jnp.minimum(v, 0.0) - jnp.log1p(jnp.exp(-jnp.abs(v)))


def _params(sem):
    return pltpu.CompilerParams(dimension_semantics=sem, vmem_limit_bytes=VMEM_LIMIT)


PROJ_TM = 512


def _proj_kernel(x_ref, g_ref, wml_ref, wnsa_ref, oml_ref, onsa_ref):
    x = x_ref[...]
    u = x * lax.rsqrt(jnp.mean(x * x, axis=-1, keepdims=True) + EPS) * g_ref[...]
    ub = u.astype(BF16)
    oml_ref[...] = _dot(ub, wml_ref[...])
    onsa_ref[...] = _dot(ub, wnsa_ref[...])


def _proj(x2, g_mix, w_ml, w_nsa):
    n = x2.shape[0]
    tm = min(PROJ_TM, n)
    return pl.pallas_call(
        _proj_kernel,
        out_shape=(jax.ShapeDtypeStruct((n, ML_COLS), F32), jax.ShapeDtypeStruct((n, NSA_COLS), F32)),
        grid=(n // tm,),
        in_specs=[pl.BlockSpec((tm, D_MODEL), lambda i: (i, 0)),
                  pl.BlockSpec((1, D_MODEL), lambda i: (0, 0)),
                  pl.BlockSpec((D_MODEL, ML_COLS), lambda i: (0, 0)),
                  pl.BlockSpec((D_MODEL, NSA_COLS), lambda i: (0, 0))],
        out_specs=(pl.BlockSpec((tm, ML_COLS), lambda i: (i, 0)),
                   pl.BlockSpec((tm, NSA_COLS), lambda i: (i, 0))),
        compiler_params=_params(("parallel",)),
        name="in_proj",
    )(x2, g_mix, w_ml, w_nsa)


ML_T = 256
CONV_PAD = 8


def _mlstm_kernel(q_ref, k_ref, v_ref, o_ref, g_ref, cw_ref, cb_ref, gb_ref, gout_ref, tri_ref, blk_ref,
                  out_ref, buf_ref, c_ref, n_ref, m_ref):
    T, L = ML_T, ML_CHUNK

    @pl.when(pl.program_id(1) == 0)
    def _():
        buf_ref[0:CONV_PAD, :] = jnp.zeros((CONV_PAD, 2 * ML_W), F32)
        c_ref[...] = jnp.zeros(c_ref.shape, F32)
        n_ref[...] = jnp.zeros(n_ref.shape, F32)
        m_ref[...] = jnp.zeros(m_ref.shape, F32)

    buf_ref[CONV_PAD:CONV_PAD + T, 0:ML_W] = q_ref[...]
    buf_ref[CONV_PAD:CONV_PAD + T, ML_W:2 * ML_W] = k_ref[...]
    y = cb_ref[...]
    for j in range(ML_CONV):
        lo = CONV_PAD - (ML_CONV - 1) + j
        y = y + cw_ref[j:j + 1, :] * buf_ref[lo:lo + T, :]
    buf_ref[0:CONV_PAD, :] = buf_ref[T:T + CONV_PAD, :]
    qk = y * _sigmoid(y)
    qf = qk[:, 0:ML_W] * (ML_DQK ** -0.5)
    kf = qk[:, ML_W:2 * ML_W]

    gg = g_ref[...] + gb_ref[...]
    gt = gg.T
    tri = tri_ref[...]
    blk = blk_ref[...]
    lf1, lf2, lf3 = _split3(_log_sigmoid(gg))
    b_col = _dot(tri, lf1) + _dot(tri, lf2) + _dot(tri, lf3)
    bl_col = _dot(blk, lf1) + _dot(blk, lf2) + _dot(blk, lf3)
    lt1, lt2, lt3 = _split3(_log_sigmoid(gt[0:8, :]))
    b_row = _dot_nt(lt1, tri) + _dot_nt(lt2, tri) + _dot_nt(lt3, tri)
    bl_row = _dot_nt(lt1, blk) + _dot_nt(lt2, blk) + _dot_nt(lt3, blk)

    r_i = lax.broadcasted_iota(jnp.int32, (T, T), 0)
    c_i = lax.broadcasted_iota(jnp.int32, (T, T), 1)
    causal = (r_i // L == c_i // L) & (c_i <= r_i)
    row_chunk = lax.broadcasted_iota(jnp.int32, (T, 1), 0) // L
    lane_chunk = lax.broadcasted_iota(jnp.int32, (1, T), 1) // L
    n_chunks = T // L

    for h in range(ML_HEADS):
        cols = slice(h * ML_DQK, (h + 1) * ML_DQK)
        f_lane = ML_HEADS + h
        bc, blc, igc = b_col[:, f_lane:f_lane + 1], bl_col[:, f_lane:f_lane + 1], gg[:, h:h + 1]
        br, blr, igr = b_row[f_lane:f_lane + 1, :], bl_row[f_lane:f_lane + 1, :], gt[h:h + 1, :]
        g_row = blr - br + igr
        g_col = blc - bc + igc

        m_prev = [m_ref[h][:, 0:1]]
        a_old, a_new, m_cs = [], [], []
        for c in range(n_chunks):
            m_c = jnp.max(jnp.where(lane_chunk == c, g_row, NEG), axis=1, keepdims=True)
            bl = blr[:, c * L:c * L + 1]
            m_next = jnp.maximum(bl + m_prev[c], m_c)
            a_old.append(jnp.exp(bl + m_prev[c] - m_next))
            a_new.append(jnp.exp(m_c - m_next))
            m_cs.append(m_c)
            m_prev.append(m_next)
        mprev_col = jnp.zeros((T, 1), F32)
        mc_col = jnp.zeros((T, 1), F32)
        for c in range(n_chunks):
            mprev_col = jnp.where(row_chunk == c, m_prev[c], mprev_col)
            mc_col = jnp.where(row_chunk == c, m_cs[c], mc_col)

        q_c = qf[:, cols]
        k_c = kf[:, cols]
        q_b = q_c.astype(BF16)
        v_b = v_ref[:, cols].astype(BF16)
        d_log = jnp.where(causal, bc - br + igr, NEG)
        a_log = bc + mprev_col
        m_t = jnp.maximum(a_log, jnp.max(d_log, axis=1, keepdims=True))
        wd = jnp.exp(d_log - m_t) * _dot_nt(q_b, k_c.astype(BF16))
        inter = jnp.exp(a_log - m_t)
        intra = _dot(wd.astype(BF16), v_b)
        wsum = jnp.sum(wd, axis=1, keepdims=True)
        kw = k_c * jnp.exp(g_col - mc_col)
        kw_b = kw.astype(BF16)

        c_st = c_ref[h]
        n_st = n_ref[h]
        qc_parts, qn_parts = [], []
        for c in range(n_chunks):
            rows = slice(c * L, (c + 1) * L)
            qc_parts.append(_dot(q_b[rows, :], c_st.astype(BF16)))
            qn_parts.append(jnp.sum(q_c[rows, :] * n_st, axis=1, keepdims=True))
            c_st = a_old[c] * c_st + a_new[c] * _dot_tn(kw_b[rows, :], v_b[rows, :])
            n_st = a_old[c] * n_st + a_new[c] * jnp.sum(kw[rows, :], axis=0, keepdims=True)
        c_ref[h] = c_st
        n_ref[h] = n_st
        m_ref[h] = jnp.broadcast_to(m_prev[n_chunks], (1, LANES))

        num = inter * jnp.concatenate(qc_parts, axis=0) + intra
        den = inter * jnp.concatenate(qn_parts, axis=0) + wsum
        hh = num / jnp.maximum(jnp.abs(den), jnp.exp(-m_t))
        hn = hh * lax.rsqrt(jnp.mean(hh * hh, axis=-1, keepdims=True) + EPS) * gout_ref[h:h + 1, :]
        out_ref[:, cols] = (hn * _sigmoid(o_ref[:, cols])).astype(BF16)


def _mlstm(ml, conv_w, conv_b, gate_b, g_out, bsz, seq):
    T = min(ML_T, seq)
    assert T == ML_T and seq % T == 0
    t_i = np.arange(T)
    blk = t_i[:, None] // ML_CHUNK == t_i[None, :] // ML_CHUNK
    tri = jnp.asarray(blk & (t_i[None, :] <= t_i[:, None]), BF16)
    blk = jnp.asarray(blk, BF16)
    wblk = lambda j: pl.BlockSpec((None, T, ML_W), lambda b, i, j=j: (b, i, j))
    full = lambda shape: pl.BlockSpec(shape, lambda b, i: (0,) * len(shape))
    return pl.pallas_call(
        _mlstm_kernel,
        out_shape=jax.ShapeDtypeStruct((bsz, seq, ML_W), BF16),
        grid=(bsz, seq // T),
        in_specs=[wblk(0), wblk(1), wblk(2), wblk(3),
                  pl.BlockSpec((None, T, LANES), lambda b, i: (b, i, 4 * ML_W // LANES)),
                  full((ML_CONV, 2 * ML_W)), full((1, 2 * ML_W)), full((1, LANES)),
                  full((ML_HEADS, ML_DQK)), full((T, T)), full((T, T))],
        out_specs=pl.BlockSpec((None, T, ML_W), lambda b, i: (b, i, 0)),
        scratch_shapes=[pltpu.VMEM((CONV_PAD + T, 2 * ML_W), F32),
                        pltpu.VMEM((ML_HEADS, ML_DQK, ML_DQK), F32),
                        pltpu.VMEM((ML_HEADS, 1, ML_DQK), F32),
                        pltpu.VMEM((ML_HEADS, 1, LANES), F32)],
        compiler_params=_params(("parallel", "arbitrary")),
        name="mlstm",
    )(ml, ml, ml, ml, ml, conv_w, conv_b, gate_b, g_out, tri, blk)


NSA_TP = 256
NSA_TQ = 256
NSA_OFF_KC, NSA_OFF_VC, NSA_OFF_KS, NSA_OFF_VS, NSA_OFF_KW, NSA_OFF_VW, NSA_OFF_GATE = (
    NSA_W + k * NSA_KV_W for k in range(7))


def _rope_tables(pos):
    half = NSA_DH // 2
    inv_freq = jnp.power(ROPE_THETA, -jnp.arange(half, dtype=F32) / half)
    ang = pos.astype(F32)[:, None] * inv_freq[None, :]
    cos, sin = jnp.cos(ang), jnp.sin(ang)
    cos = jnp.concatenate([cos, cos], axis=1)
    sin = jnp.concatenate([-sin, sin], axis=1)
    return jnp.tile(cos, (1, 2)), jnp.tile(sin, (1, 2))


def _head_norm_rope(xs, gsum, gain, cos, sin):
    a, b, c = _split3(xs * xs)
    ss = _dot(a, gsum) + _dot(b, gsum) + _dot(c, gsum)
    xn = xs * lax.rsqrt(ss * (1.0 / NSA_DH) + EPS) * gain
    lane = lax.broadcasted_iota(jnp.int32, xs.shape, 1)
    first = (lane % NSA_DH) < NSA_DH // 2
    swapped = jnp.where(first, pltpu.roll(xn, LANES - NSA_DH // 2, axis=1), pltpu.roll(xn, NSA_DH // 2, axis=1))
    return xn * cos + swapped * sin


def _head_norm_rope_t(xt, gsum, gain, cos_t, sin_t):
    a, b, c = _split3(xt * xt)
    ss = _dot(gsum, a) + _dot(gsum, b) + _dot(gsum, c)
    xn = xt * lax.rsqrt(ss * (1.0 / NSA_DH) + EPS) * gain
    row = lax.broadcasted_iota(jnp.int32, xt.shape, 0)
    first = (row % NSA_DH) < NSA_DH // 2
    swapped = jnp.where(first, pltpu.roll(xn, LANES - NSA_DH // 2, axis=0), pltpu.roll(xn, NSA_DH // 2, axis=0))
    return xn * cos_t + swapped * sin_t


def _store_keys(out, slab):
    lane = lax.broadcasted_iota(jnp.int32, slab.shape, 1)
    low = lane < NSA_DH
    rolled = pltpu.roll(slab, NSA_DH, axis=1)
    zero = jnp.zeros_like(slab)
    out[0, 0] = jnp.where(low, slab, zero).astype(BF16)
    out[0, 1] = jnp.where(low, zero, rolled).astype(BF16)
    out[1, 0] = jnp.where(low, rolled, zero).astype(BF16)
    out[1, 1] = jnp.where(low, zero, slab).astype(BF16)


def _store_values_t(out, slab):
    st = slab.T
    out[0] = st[0:NSA_DH, :].astype(BF16)
    out[1] = st[NSA_DH:2 * NSA_DH, :].astype(BF16)


def _nsa_prep_kernel(x_ref, cos_ref, sin_ref, cost_ref, sint_ref, gq_ref, gks_ref, gkw_ref, gsum_ref,
                     q_out, ks_out, vs_out, kw_out, vw_out, kc_out, vc_out):
    cos, sin, gsum = cos_ref[...], sin_ref[...], gsum_ref[...]
    for c in range(NSA_W // LANES):
        xt = x_ref[:, c * LANES:(c + 1) * LANES].T
        qn = _head_norm_rope_t(xt, gsum, gq_ref[...], cost_ref[...], sint_ref[...]) * (NSA_DH ** -0.5)
        q_out[c * LANES:(c + 1) * LANES, :] = qn.astype(BF16)
    ks = _head_norm_rope(x_ref[:, NSA_OFF_KS:NSA_OFF_KS + LANES], gsum, gks_ref[...], cos, sin)
    kw = _head_norm_rope(x_ref[:, NSA_OFF_KW:NSA_OFF_KW + LANES], gsum, gkw_ref[...], cos, sin)
    _store_keys(ks_out, ks)
    _store_keys(kw_out, kw)
    _store_values_t(vs_out, x_ref[:, NSA_OFF_VS:NSA_OFF_VS + LANES])
    _store_values_t(vw_out, x_ref[:, NSA_OFF_VW:NSA_OFF_VW + LANES])
    kc_out[...] = x_ref[:, NSA_OFF_KC:NSA_OFF_KC + LANES]
    vc_out[...] = x_ref[:, NSA_OFF_VC:NSA_OFF_VC + LANES]


def _nsa_cmp_kernel(xk_ref, xv_ref, pek_ref, pev_ref, w1ka_ref, w1kb_ref, w1va_ref, w1vb_ref,
                    b1k_ref, b1v_ref, w2k_ref, w2v_ref, gk_ref, cos_ref, sin_ref, gsum_ref,
                    kc_out, vc_out):
    rows = xk_ref.shape[0]

    def hidden(x, pe, wa, wb, b1):
        first = _dot((x + pe[0:1, :]).astype(BF16), wa[...])
        second = _dot((x + pe[1:2, :]).astype(BF16), wb[...])
        hid = first + pltpu.roll(second, rows - 1, axis=0) + b1[...]
        return (hid * _sigmoid(hid)).astype(BF16)

    kc = _dot(hidden(xk_ref[...], pek_ref, w1ka_ref, w1kb_ref, b1k_ref), w2k_ref[...])
    kc = _head_norm_rope(kc, gsum_ref[...], gk_ref[...], cos_ref[...], sin_ref[...])
    _store_keys(kc_out, kc)
    vc_t = _dot_nt(w2v_ref[...], hidden(xv_ref[...], pev_ref, w1va_ref, w1vb_ref, b1v_ref))
    vc_out[0] = vc_t[0:NSA_DH, :].astype(BF16)
    vc_out[1] = vc_t[NSA_DH:2 * NSA_DH, :].astype(BF16)


NSA_HEAD_SLOT = (0, 2, 1, 3)


def _stacked_scores(k_lo, k_hi, q_pair, bias):
    s = jnp.concatenate([_dot(k_lo, q_pair), _dot(k_hi, q_pair)], axis=1)
    return s + jnp.concatenate([bias] * NSA_GROUP, axis=1)


def _online_update(s, v_t, m_ref, l_ref, acc_ref):
    m_old = m_ref[...]
    m_new = jnp.maximum(m_old, jnp.max(s, axis=0, keepdims=True))
    alpha = jnp.exp(m_old - m_new)
    e = jnp.exp(s - jnp.maximum(m_new, 0.1 * NEG))
    l_ref[...] = alpha * l_ref[...] + jnp.sum(e, axis=0, keepdims=True)
    acc_ref[...] = alpha * acc_ref[...] + _dot(v_t, e.astype(BF16))
    m_ref[...] = m_new


def _nsa_attn_kernel(q_ref, kc_ref, vc_ref, ks_ref, vs_ref, kw_ref, vw_ref, gate_ref, mt_ref, esel_ref, eg_ref,
                     out_ref, ms_s, ls_s, accs_s, mw_s, lw_s, accw_s, *, n_cmp, n_sel):
    TQ = NSA_TQ
    i = pl.program_id(2)
    t0 = i * TQ
    t_row = t0 + lax.broadcasted_iota(jnp.int32, (1, TQ), 1)
    q_pair = jnp.concatenate([q_ref[0:LANES, :], q_ref[LANES:2 * LANES, :]], axis=1)

    rows = kc_ref.shape[1]
    n_i = lax.broadcasted_iota(jnp.int32, (rows, 1), 0)
    cmask = (n_i < n_cmp) & (n_i * CMP_STRIDE + (CMP_BLOCK - 1) <= t_row)
    s = _stacked_scores(kc_ref[0], kc_ref[1], q_pair, jnp.where(cmask, 0.0, NEG))
    e = jnp.exp(s - jnp.maximum(jnp.max(s, axis=0, keepdims=True), 0.1 * NEG))
    p = e / jnp.maximum(jnp.sum(e, axis=0, keepdims=True), 1e-30)
    o_cmp_t = _dot(vc_ref[...], p.astype(BF16))
    psum = p[:, 0:TQ] + p[:, TQ:2 * TQ] + p[:, 2 * TQ:3 * TQ] + p[:, 3 * TQ:4 * TQ]
    p1, p2, p3 = _split3(psum)
    imp = _dot(mt_ref[...], p1) + _dot(mt_ref[...], p2) + _dot(mt_ref[...], p3)
    j_i = lax.broadcasted_iota(jnp.int32, (n_sel, TQ), 0)
    cur = (t0 + lax.broadcasted_iota(jnp.int32, (n_sel, TQ), 1)) // SEL_BLOCK
    forced = (j_i == 0) | (j_i == cur) | (j_i == cur - 1)
    score = jnp.where(j_i > cur, NEG, imp[0:n_sel, :] + SEL_FORCE_BONUS * forced.astype(F32))
    rank = jnp.zeros((n_sel, TQ), jnp.int32)
    for jp in range(n_sel):
        row = score[jp:jp + 1, :]
        beats = (row > score) | ((row == score) & (j_i > jp))
        rank = rank + beats.astype(jnp.int32)
    sel_t = (rank < min(SEL_TOPN, n_sel)).astype(F32)
    sel_b = jnp.concatenate([sel_t, jnp.zeros((LANES - n_sel, TQ), F32)], axis=0).astype(BF16)

    for m_ref, l_ref, acc_ref in ((ms_s, ls_s, accs_s), (mw_s, lw_s, accw_s)):
        m_ref[...] = jnp.full(m_ref.shape, NEG, F32)
        l_ref[...] = jnp.zeros(l_ref.shape, F32)
        acc_ref[...] = jnp.zeros(acc_ref.shape, F32)

    def key_tile(jt, carry, with_window):
        k0 = pl.multiple_of(jt * TQ, TQ)
        diff = t_row - (k0 + lax.broadcasted_iota(jnp.int32, (TQ, 1), 0))
        causal = diff >= 0
        smask = (_dot(esel_ref[jt], sel_b) > 0.5) & causal
        s = _stacked_scores(ks_ref[0, pl.ds(k0, TQ), :], ks_ref[1, pl.ds(k0, TQ), :], q_pair,
                            jnp.where(smask, 0.0, NEG))
        _online_update(s, vs_ref[jt], ms_s, ls_s, accs_s)
        if not with_window:
            return carry
        wmask = causal & (diff < WINDOW)
        sw = _stacked_scores(kw_ref[0, pl.ds(k0, TQ), :], kw_ref[1, pl.ds(k0, TQ), :], q_pair,
                             jnp.where(wmask, 0.0, NEG))
        _online_update(sw, vw_ref[jt], mw_s, lw_s, accw_s)
        return carry

    first_win = jnp.maximum(i - (WINDOW + TQ - 2) // TQ, 0)
    lax.fori_loop(0, first_win, functools.partial(key_tile, with_window=False), 0)
    lax.fori_loop(first_win, i + 1, functools.partial(key_tile, with_window=True), 0)

    g1, g2, g3 = _split3(_sigmoid(gate_ref[...]))
    gexp = _dot(g1, eg_ref[...]) + _dot(g2, eg_ref[...]) + _dot(g3, eg_ref[...])
    half_w = NSA_GROUP * NSA_DH
    o_sel_t = accs_s[...] / jnp.maximum(ls_s[...], 1e-30)
    o_win_t = accw_s[...] / jnp.maximum(lw_s[...], 1e-30)

    def token_major(o_t, c):
        ce, co = NSA_HEAD_SLOT[2 * c] * TQ, NSA_HEAD_SLOT[2 * c + 1] * TQ
        return jnp.concatenate([o_t[:, ce:ce + TQ], o_t[:, co:co + TQ]], axis=0).T

    for c in range(NSA_GROUP // 2):
        o_cmp = token_major(o_cmp_t, c)
        o_sel = token_major(o_sel_t, c)
        o_win = token_major(o_win_t, c)
        cs = slice(c * LANES, (c + 1) * LANES)
        o = (gexp[:, c * LANES:(c + 1) * LANES] * o_cmp
             + gexp[:, half_w + c * LANES:half_w + (c + 1) * LANES] * o_sel
             + gexp[:, 2 * half_w + c * LANES:2 * half_w + (c + 1) * LANES] * o_win)
        out_ref[:, cs] = o.astype(BF16)


def _nsa_constants(seq):
    n_cmp = (seq - CMP_BLOCK) // CMP_STRIDE + 1
    n_sel = seq // SEL_BLOCK
    rows = seq // CMP_STRIDE
    n = np.arange(rows)[None, :]
    j = np.arange(LANES)[:, None]
    mt = ((n * CMP_STRIDE < j * SEL_BLOCK + SEL_BLOCK) & (n * CMP_STRIDE + CMP_BLOCK > j * SEL_BLOCK)
          & (n < n_cmp) & (j < n_sel))
    s = np.arange(seq)
    esel = (s[:, None] // SEL_BLOCK == np.arange(LANES)[None, :])
    esel = esel.reshape(seq // NSA_TQ, NSA_TQ, LANES)
    eg = np.zeros((NSA_KVH, LANES, N_BRANCH * NSA_GROUP * NSA_DH), np.float32)
    for h in range(NSA_KVH):
        for g in range(NSA_GROUP):
            for br in range(N_BRANCH):
                lane = (h * NSA_GROUP + g) * N_BRANCH + br
                c0 = br * NSA_GROUP * NSA_DH + g * NSA_DH
                eg[h, lane, c0:c0 + NSA_DH] = 1.0
    gsum = np.kron(np.eye(LANES // NSA_DH), np.ones((NSA_DH, NSA_DH)))
    return (n_cmp, n_sel, jnp.asarray(mt, BF16), jnp.asarray(esel, BF16), jnp.asarray(eg, BF16),
            jnp.asarray(gsum, BF16))


def _nsa(nsa, w, bsz, seq):
    assert seq % NSA_TQ == 0 and seq % NSA_TP == 0
    n_cmp, n_sel, mt, esel, eg, gsum = _nsa_constants(seq)
    rows = seq // CMP_STRIDE
    cos, sin = _rope_tables(jnp.arange(seq))
    cos_c, sin_c = _rope_tables(jnp.arange(rows) * CMP_STRIDE + (CMP_BLOCK - 1))
    c2 = lambda shape: pl.BlockSpec(shape, lambda b, i: (0,) * len(shape))
    n_tiles = seq // NSA_TP
    k_shape = jax.ShapeDtypeStruct((bsz, NSA_KVH, 2, seq, LANES), BF16)
    k_spec = pl.BlockSpec((None, NSA_KVH, 2, NSA_TP, LANES), lambda b, i: (b, 0, 0, i, 0))
    v_shape = jax.ShapeDtypeStruct((bsz, NSA_KVH, n_tiles, NSA_DH, NSA_TP), BF16)
    v_spec = pl.BlockSpec((None, NSA_KVH, None, NSA_DH, NSA_TP), lambda b, i: (b, 0, i, 0, 0))
    raw_shape = jax.ShapeDtypeStruct((bsz, seq, LANES), F32)
    raw_spec = pl.BlockSpec((None, NSA_TP, LANES), lambda b, i: (b, i, 0))
    qn, ksd, vsd, kwd, vwd, kc_raw, vc_raw = pl.pallas_call(
        _nsa_prep_kernel,
        out_shape=(jax.ShapeDtypeStruct((bsz, NSA_W, seq), BF16), k_shape, v_shape, k_shape, v_shape,
                   raw_shape, raw_shape),
        grid=(bsz, n_tiles),
        in_specs=[pl.BlockSpec((None, NSA_TP, NSA_COLS), lambda b, i: (b, i, 0)),
                  pl.BlockSpec((NSA_TP, LANES), lambda b, i: (i, 0)),
                  pl.BlockSpec((NSA_TP, LANES), lambda b, i: (i, 0)),
                  pl.BlockSpec((LANES, NSA_TP), lambda b, i: (0, i)),
                  pl.BlockSpec((LANES, NSA_TP), lambda b, i: (0, i)),
                  c2((LANES, 1)), c2((1, LANES)), c2((1, LANES)), c2((LANES, LANES))],
        out_specs=(pl.BlockSpec((None, NSA_W, NSA_TP), lambda b, i: (b, 0, i)),
                   k_spec, v_spec, k_spec, v_spec, raw_spec, raw_spec),
        compiler_params=_params(("parallel", "parallel")),
        name="nsa_prep",
    )(nsa, cos, sin, cos.T, sin.T, w["nsa_gq"].reshape(LANES, 1), w["nsa_gks"], w["nsa_gkw"], gsum)

    wide = CMP_STRIDE * LANES
    xk = kc_raw.reshape(bsz, rows, wide)
    xv = vc_raw.reshape(bsz, rows, wide)
    c1 = lambda shape: pl.BlockSpec(shape, lambda b: (0,) * len(shape))
    kc_shape = jax.ShapeDtypeStruct((bsz, NSA_KVH, 2, rows, LANES), BF16)
    kc_spec = pl.BlockSpec((None, NSA_KVH, 2, rows, LANES), lambda b: (b, 0, 0, 0, 0))
    vc_shape = jax.ShapeDtypeStruct((bsz, NSA_KVH, NSA_DH, rows), BF16)
    vc_spec = pl.BlockSpec((None, NSA_KVH, NSA_DH, rows), lambda b: (b, 0, 0, 0))
    hid2 = NSA_KVH * CMP_HIDDEN
    kcd, vcd = pl.pallas_call(
        _nsa_cmp_kernel,
        out_shape=(kc_shape, vc_shape),
        grid=(bsz,),
        in_specs=[pl.BlockSpec((None, rows, wide), lambda b: (b, 0, 0)),
                  pl.BlockSpec((None, rows, wide), lambda b: (b, 0, 0)),
                  c1((2, wide)), c1((2, wide)),
                  c1((wide, hid2)), c1((wide, hid2)), c1((wide, hid2)), c1((wide, hid2)),
                  c1((1, hid2)), c1((1, hid2)), c1((hid2, LANES)), c1((LANES, hid2)),
                  c1((1, LANES)), c1((rows, LANES)), c1((rows, LANES)), c1((LANES, LANES))],
        out_specs=(kc_spec, vc_spec),
        compiler_params=_params(("parallel",)),
        name="nsa_compress",
    )(xk, xv, w["cmp_pe_k"], w["cmp_pe_v"], w["cmp_w1k_a"], w["cmp_w1k_b"], w["cmp_w1v_a"], w["cmp_w1v_b"],
      w["cmp_b1_k"], w["cmp_b1_v"], w["cmp_w2_k"], w["cmp_w2_v"].T, w["nsa_gkc"], cos_c, sin_c, gsum)

    TQ = NSA_TQ
    half_w = NSA_GROUP * NSA_DH
    c3 = lambda shape: pl.BlockSpec(shape, lambda b, h, i: (0,) * len(shape))
    assert NSA_TP == TQ
    keys = lambda r: pl.BlockSpec((None, None, 2, r, LANES), lambda b, h, i: (b, h, 0, 0, 0))
    vals = pl.BlockSpec((None, None, seq // TQ, NSA_DH, TQ), lambda b, h, i: (b, h, 0, 0, 0))
    stat = lambda: pltpu.VMEM((1, NSA_GROUP * TQ), F32)
    accu = lambda: pltpu.VMEM((NSA_DH, NSA_GROUP * TQ), F32)
    return pl.pallas_call(
        functools.partial(_nsa_attn_kernel, n_cmp=n_cmp, n_sel=n_sel),
        out_shape=jax.ShapeDtypeStruct((bsz, seq, NSA_W), BF16),
        grid=(bsz, NSA_KVH, seq // TQ),
        in_specs=[pl.BlockSpec((None, half_w, TQ), lambda b, h, i: (b, h, i)),
                  keys(rows), pl.BlockSpec((None, None, NSA_DH, rows), lambda b, h, i: (b, h, 0, 0)),
                  keys(seq), vals, keys(seq), vals,
                  pl.BlockSpec((None, TQ, LANES), lambda b, h, i: (b, i, NSA_OFF_GATE // LANES)),
                  c3((LANES, rows)), c3((seq // TQ, TQ, LANES)),
                  pl.BlockSpec((None, LANES, N_BRANCH * half_w), lambda b, h, i: (h, 0, 0))],
        out_specs=pl.BlockSpec((None, TQ, half_w), lambda b, h, i: (b, i, h)),
        scratch_shapes=[stat(), stat(), accu(), stat(), stat(), accu()],
        compiler_params=_params(("parallel", "parallel", "arbitrary")),
        name="nsa_attention",
    )(qn, kcd, vcd, ksd, vsd, kwd, vwd, nsa, mt, esel, eg)


POST_TM = 512


def _rms(v, gain):
    return v * lax.rsqrt(jnp.mean(v * v, axis=-1, keepdims=True) + EPS) * gain


def _memkv_kernel(mem_ref, g_ref, w_ref, gk_ref, k_out, v_out):
    u = _rms(mem_ref[...], g_ref[...]).astype(BF16)
    kv = _dot(u, w_ref[...])
    for h in range(MEM_HEADS):
        cs = slice(h * MEM_DH, (h + 1) * MEM_DH)
        k_out[:, cs] = _rms(kv[:, cs], gk_ref[...]).astype(BF16)
    v_out[...] = kv[:, MEM_W:2 * MEM_W].astype(BF16)


def _memkv(mem, g_mem, w_mem_kv, g_k):
    bsz, m_len, _ = mem.shape
    c1 = lambda shape: pl.BlockSpec(shape, lambda b: (0,) * len(shape))
    shape = jax.ShapeDtypeStruct((bsz, m_len, MEM_W), BF16)
    spec = pl.BlockSpec((None, m_len, MEM_W), lambda b: (b, 0, 0))
    return pl.pallas_call(
        _memkv_kernel, out_shape=(shape, shape), grid=(bsz,),
        in_specs=[pl.BlockSpec((None, m_len, D_MODEL), lambda b: (b, 0, 0)),
                  c1((1, D_MODEL)), c1((D_MODEL, 2 * MEM_W)), c1((1, MEM_DH))],
        out_specs=(spec, spec),
        compiler_params=_params(("parallel",)),
        name="mem_kv",
    )(mem, g_mem, w_mem_kv, g_k)


def _post_kernel(x_ref, hml_ref, hnsa_ref, mk_ref, mv_ref, gmix_ref, wmq_ref, gmq_ref, wgate_ref,
                 wml_ref, wnsa_ref, wmem_ref, wout_ref, gffn_ref, wrh_ref, wrl_ref, br_ref,
                 x1_out, t_out, logit_out):
    x = x_ref[...]
    ub = _rms(x, gmix_ref[...]).astype(BF16)
    mq = _dot(ub, wmq_ref[...])
    heads = []
    for h in range(MEM_HEADS):
        cs = slice(h * MEM_DH, (h + 1) * MEM_DH)
        qn = (_rms(mq[:, cs], gmq_ref[...]) * (MEM_DH ** -0.5)).astype(BF16)
        s = _dot_nt(qn, mk_ref[:, cs])
        e = jnp.exp(s - jnp.max(s, axis=1, keepdims=True))
        p = e / jnp.sum(e, axis=1, keepdims=True)
        heads.append(_dot(p.astype(BF16), mv_ref[:, cs]))
    hmem = jnp.concatenate(heads, axis=1).astype(BF16)
    gates = _sigmoid(_dot(ub, wgate_ref[...]))
    merged = (gates[:, 0:D_MODEL] * _dot(hml_ref[...], wml_ref[...])
              + gates[:, D_MODEL:2 * D_MODEL] * _dot(hnsa_ref[...], wnsa_ref[...])
              + gates[:, 2 * D_MODEL:3 * D_MODEL] * _dot(hmem, wmem_ref[...]))
    x1 = x + _dot(merged.astype(BF16), wout_ref[...])
    x1_out[...] = x1
    t = _rms(x1, gffn_ref[...])
    t_out[...] = t
    t1, t2, _ = _split3(t)
    logit_out[...] = _dot(t1, wrh_ref[...]) + _dot(t1, wrl_ref[...]) + _dot(t2, wrh_ref[...]) + br_ref[...]


def _post(x2, hml, hnsa, mk, mv, w, seq):
    n = x2.shape[0]
    tm = POST_TM
    assert seq % tm == 0
    per = seq // tm
    m_len = mk.shape[1]
    row = lambda width: pl.BlockSpec((tm, width), lambda i: (i, 0))
    c1 = lambda shape: pl.BlockSpec(shape, lambda i: (0,) * len(shape))
    memspec = pl.BlockSpec((None, m_len, MEM_W), lambda i: (i // per, 0, 0))
    return pl.pallas_call(
        _post_kernel,
        out_shape=(jax.ShapeDtypeStruct((n, D_MODEL), F32), jax.ShapeDtypeStruct((n, D_MODEL), F32),
                   jax.ShapeDtypeStruct((n, LANES), F32)),
        grid=(n // tm,),
        in_specs=[row(D_MODEL), row(ML_W), row(NSA_W), memspec, memspec,
                  c1((1, D_MODEL)), c1((D_MODEL, MEM_W)), c1((1, MEM_DH)), c1((D_MODEL, N_BRANCH * D_MODEL)),
                  c1((ML_W, D_MODEL)), c1((NSA_W, D_MODEL)), c1((MEM_W, D_MODEL)), c1((D_MODEL, D_MODEL)),
                  c1((1, D_MODEL)), c1((D_MODEL, LANES)), c1((D_MODEL, LANES)), c1((1, LANES))],
        out_specs=(row(D_MODEL), row(D_MODEL), row(LANES)),
        compiler_params=_params(("parallel",)),
        name="post",
    )(x2, hml, hnsa, mk, mv, w["g_mix"], w["w_mq"], w["mem_gq"], w["w_gate"], w["w_br_ml"], w["w_br_nsa"],
      w["w_br_mem"], w["w_out"], w["g_ffn"], w["w_router_hi"], w["w_router_lo"], w["b_router"])


ROUTE_T = 512
MOE_BLK = 256
MOE_TD = 256
MOE_DEPTH = 3


def _route_kernel(logit_ref, e_out, wtm_out, cnt_out):
    T = ROUTE_T

    @pl.when(pl.program_id(0) == 0)
    def _():
        cnt_out[...] = jnp.zeros(cnt_out.shape, F32)

    sc = logit_ref[...].T[0:N_EXPERTS, :]
    j_i = lax.broadcasted_iota(jnp.int32, (N_EXPERTS, T), 0)
    rank = jnp.zeros((N_EXPERTS, T), jnp.int32)
    for jp in range(N_EXPERTS):
        row = sc[jp:jp + 1, :]
        beats = (row > sc) | ((row == sc) & (j_i > jp))
        rank = rank + beats.astype(jnp.int32)
    sel = rank < TOP_K
    e = jnp.where(sel, jnp.exp(sc - jnp.max(sc, axis=0, keepdims=True)), 0.0)
    wgt = e / jnp.sum(e, axis=0, keepdims=True)
    cnt_out[...] = cnt_out[...] + jnp.sum(sel.astype(F32), axis=1, keepdims=True)
    e_rows, w_rows = [], []
    for r in range(TOP_K):
        hit = rank == r
        e_rows.append(jnp.sum(jnp.where(hit, j_i, 0), axis=0, keepdims=True))
        w_rows.append(jnp.sum(jnp.where(hit, wgt, 0.0), axis=0, keepdims=True))
    e_out[...] = jnp.concatenate(e_rows, axis=0)
    w4 = jnp.concatenate(w_rows, axis=0)
    wtm_out[...] = jnp.concatenate([w4, jnp.zeros((LANES - TOP_K, T), F32)], axis=0).T


def _route(logits):
    n = logits.shape[0]
    T = ROUTE_T
    assert n % T == 0
    return pl.pallas_call(
        _route_kernel,
        out_shape=(jax.ShapeDtypeStruct((TOP_K, n), jnp.int32), jax.ShapeDtypeStruct((n, LANES), F32),
                   jax.ShapeDtypeStruct((N_EXPERTS, LANES), F32)),
        grid=(n // T,),
        in_specs=[pl.BlockSpec((T, LANES), lambda i: (i, 0))],
        out_specs=(pl.BlockSpec((TOP_K, T), lambda i: (0, i)), pl.BlockSpec((T, LANES), lambda i: (i, 0)),
                   pl.BlockSpec((N_EXPERTS, LANES), lambda i: (0, 0))),
        compiler_params=_params(("arbitrary",)),
        name="moe_route",
    )(logits)


def _ffn_fused_kernel(be_ref, base_ref, nv_ref, order_ref, t_hbm, wgu_ref, bgu_ref, wd_ref, bd_ref, yk_hbm,
                      xbuf, ybuf, gsem, ssem, wgu_s, wd_s, *, n_tok):
    i = pl.program_id(0)
    nb = pl.num_programs(0)
    slot = i % MOE_DEPTH
    last_id = n_tok * TOP_K - 1
    dump0 = n_tok * TOP_K

    def start_gather(blk, buf):
        b0 = base_ref[blk]
        for j in range(MOE_BLK):
            tok = order_ref[jnp.minimum(b0 + j, last_id)] & (n_tok - 1)
            pltpu.make_async_copy(t_hbm.at[pl.ds(tok, 1)], xbuf.at[buf, pl.ds(j, 1)], gsem.at[buf]).start()

    def start_scatter(blk, nv, buf):
        b0 = base_ref[blk]
        for j in range(MOE_BLK):
            dst = jnp.where(j < nv, order_ref[jnp.minimum(b0 + j, last_id)], dump0 + buf * MOE_BLK + j)
            pltpu.make_async_copy(ybuf.at[buf, pl.ds(j, 1)], yk_hbm.at[pl.ds(dst, 1)], ssem.at[buf]).start()

    def wait_gather(buf):
        pltpu.make_async_copy(t_hbm.at[pl.ds(0, MOE_BLK)], xbuf.at[buf], gsem.at[buf]).wait()

    def wait_scatter(buf):
        pltpu.make_async_copy(ybuf.at[buf], yk_hbm.at[pl.ds(0, MOE_BLK)], ssem.at[buf]).wait()

    @pl.when(i == 0)
    def _():
        xbuf[...] = jnp.zeros(xbuf.shape, F32)
        ybuf[...] = jnp.zeros(ybuf.shape, F32)
        for q in range(MOE_DEPTH - 1):
            start_gather(jnp.minimum(q, nb - 1), q)
            start_scatter(0, 0, q)

    @pl.when((i == 0) | (be_ref[i] != be_ref[jnp.maximum(i - 1, 0)]))
    def _():
        wgu_s[...] = wgu_ref[...].astype(BF16)
        bits = lax.shift_right_logical(pltpu.bitcast(wd_ref[...].astype(BF16).astype(F32), jnp.uint32),
                                       jnp.uint32(16))
        wd_s[...] = bits | lax.shift_left(bits, jnp.uint32(16))

    for par in range(MOE_DEPTH):
        ahead = (par + MOE_DEPTH - 1) % MOE_DEPTH

        @pl.when(slot == par)
        def _(par=par, ahead=ahead):
            wait_gather(par)
            start_gather(jnp.minimum(i + MOE_DEPTH - 1, nb - 1), ahead)
            prev = jnp.maximum(i - 1, 0)
            start_scatter(prev, jnp.where(i == 0, 0, nv_ref[prev]), ahead)
            hgu = _dot(xbuf[par].astype(BF16), wgu_s[...]) + bgu_ref[...]
            gate = jnp.minimum(hgu, SWIGLU_LIMIT)
            up1 = jnp.clip(hgu, -SWIGLU_LIMIT, SWIGLU_LIMIT) + 1.0
            even = lax.broadcasted_iota(jnp.int32, (1, 2 * D_FF), 1) % 2 == 0
            act = jnp.where(even, gate * _sigmoid(SWIGLU_ALPHA * gate) * pltpu.roll(up1, 2 * D_FF - 1, axis=1), 0.0)
            y = _dot(act.astype(BF16), pltpu.bitcast(wd_s[...], BF16)) + bd_ref[...]
            wait_scatter(par)
            ybuf[par] = y

            @pl.when(i == nb - 1)
            def _():
                start_scatter(i, nv_ref[i], par)
                for q in range(MOE_DEPTH):
                    if q != par:
                        wait_gather(q)
                    wait_scatter(q)


def _ffn_fused(t, blk_e, base, n_valid, order, w):
    n = t.shape[0]
    nb = blk_e.shape[0]
    espec = lambda r, c: pl.BlockSpec((None, r, c), lambda i, be, *_: (be[i], 0, 0))
    return pl.pallas_call(
        functools.partial(_ffn_fused_kernel, n_tok=n),
        out_shape=jax.ShapeDtypeStruct((n * TOP_K + MOE_DEPTH * MOE_BLK, D_MODEL), F32),
        grid_spec=pltpu.PrefetchScalarGridSpec(
            num_scalar_prefetch=4, grid=(nb,),
            in_specs=[pl.BlockSpec(memory_space=pl.ANY), espec(D_MODEL, 2 * D_FF), espec(1, 2 * D_FF),
                      espec(D_FF, D_MODEL), espec(1, D_MODEL)],
            out_specs=pl.BlockSpec(memory_space=pl.ANY),
            scratch_shapes=[pltpu.VMEM((MOE_DEPTH, MOE_BLK, D_MODEL), F32),
                            pltpu.VMEM((MOE_DEPTH, MOE_BLK, D_MODEL), F32),
                            pltpu.SemaphoreType.DMA((MOE_DEPTH,)), pltpu.SemaphoreType.DMA((MOE_DEPTH,)),
                            pltpu.VMEM((D_MODEL, 2 * D_FF), BF16), pltpu.VMEM((D_FF, D_MODEL), jnp.uint32)]),
        compiler_params=_params(("arbitrary",)),
        name="moe_ffn",
    )(blk_e, base, n_valid, order, t, w["w_gu"], w["b_gu"], w["w_down"], w["b_down"])


def _moe_sum_kernel(x1_ref, wtm_ref, y0_ref, y1_ref, y2_ref, y3_ref, out_ref):
    wtm = wtm_ref[...]
    acc = x1_ref[...]
    for r, y_ref in enumerate((y0_ref, y1_ref, y2_ref, y3_ref)):
        acc = acc + wtm[:, r:r + 1] * y_ref[...]
    out_ref[...] = acc


def _moe_sum(x1, wtm, yk):
    n = x1.shape[0]
    per = n // MOE_TD
    row = lambda width: pl.BlockSpec((MOE_TD, width), lambda i: (i, 0))
    ysp = lambda r: pl.BlockSpec((MOE_TD, D_MODEL), lambda i, r=r: (r * per + i, 0))
    return pl.pallas_call(
        _moe_sum_kernel,
        out_shape=jax.ShapeDtypeStruct((n, D_MODEL), F32),
        grid=(per,),
        in_specs=[row(D_MODEL), row(LANES), ysp(0), ysp(1), ysp(2), ysp(3)],
        out_specs=row(D_MODEL),
        compiler_params=_params(("parallel",)),
        name="moe_sum",
    )(x1, wtm, yk, yk, yk, yk)


def _moe(x1, t, logits, w):
    n = x1.shape[0]
    e_idx, wtm, cnt = _route(logits)
    counts = cnt[:, 0].astype(jnp.int32)
    padded = (counts + MOE_BLK - 1) // MOE_BLK * MOE_BLK
    pad_end = jnp.cumsum(padded)
    pad_start = pad_end - padded
    start = jnp.cumsum(counts) - counts
    nb = -(-(n * TOP_K + N_EXPERTS * (MOE_BLK - 1)) // MOE_BLK)
    first = jnp.arange(nb, dtype=jnp.int32) * MOE_BLK
    blk_e = jnp.minimum(jnp.sum((pad_end[None, :] <= first[:, None]).astype(jnp.int32), axis=1), N_EXPERTS - 1)
    within = first - pad_start[blk_e]
    n_valid = jnp.clip(counts[blk_e] - within, 0, MOE_BLK).astype(jnp.int32)
    base = (start[blk_e] + within).astype(jnp.int32)
    assert n & (n - 1) == 0
    flat = jnp.argsort(e_idx.T.reshape(-1), stable=True).astype(jnp.int32)
    order = (flat % TOP_K) * n + flat // TOP_K
    yk = _ffn_fused(t, blk_e, base, n_valid, order, w)
    return _moe_sum(x1, wtm, yk)


def _cmp_weights(pos, w1, b1, w2):
    eye = jnp.eye(NSA_KVH, dtype=F32)
    halves = []
    for part in (w1[:CMP_STRIDE], w1[CMP_STRIDE:]):
        big = jnp.einsum("ldf,hg->lhdgf", part, eye)
        halves.append(big.reshape(CMP_STRIDE * LANES, NSA_KVH * CMP_HIDDEN).astype(BF16))
    pe = jnp.stack([jnp.broadcast_to(part[:, None, :], (CMP_STRIDE, NSA_KVH, NSA_DH)).reshape(-1)
                    for part in (pos[:CMP_STRIDE], pos[CMP_STRIDE:])])
    w2big = jnp.einsum("fd,hg->hfgd", w2, eye).reshape(NSA_KVH * CMP_HIDDEN, LANES).astype(BF16)
    return halves[0], halves[1], pe, jnp.tile(b1, NSA_KVH).reshape(1, -1), w2big


def _prep_weights(p):
    w = {}
    offs = np.concatenate([[0], np.cumsum(IN_SPLITS)])
    col = lambda k: p["w_in"][:, offs[k]:offs[k + 1]]
    zpad = lambda n: jnp.zeros((D_MODEL, n), F32)
    w["w_ml"] = jnp.concatenate([col(0), col(1), col(2), col(3), col(4), col(5),
                                 zpad(LANES - 2 * ML_HEADS)], axis=1).astype(BF16)
    w["w_nsa"] = jnp.concatenate([col(k) for k in range(6, 14)] + [zpad(LANES - NSA_QH * 3)], axis=1).astype(BF16)
    w["g_mix"] = p["g_mix"].reshape(1, D_MODEL)
    w["conv_w"] = p["ml_conv_w"]
    w["conv_b"] = p["ml_conv_b"].reshape(1, 2 * ML_W)
    w["ml_gate_b"] = jnp.concatenate([p["ml_b_i"], p["ml_b_f"], jnp.zeros((LANES - 2 * ML_HEADS,), F32)]).reshape(1, LANES)
    w["ml_g_out"] = p["ml_g_out"]

    pair = lambda g: jnp.tile(g, LANES // NSA_DH).reshape(1, LANES)
    w["nsa_gq"] = pair(p["nsa_g_q"])
    w["nsa_gkc"], w["nsa_gks"], w["nsa_gkw"] = (pair(p["nsa_g_k"][k]) for k in range(3))
    (w["cmp_w1k_a"], w["cmp_w1k_b"], w["cmp_pe_k"], w["cmp_b1_k"], w["cmp_w2_k"]) = _cmp_weights(
        p["cmp_pos_k"], p["cmp_w1_k"], p["cmp_b1_k"], p["cmp_w2_k"])
    (w["cmp_w1v_a"], w["cmp_w1v_b"], w["cmp_pe_v"], w["cmp_b1_v"], w["cmp_w2_v"]) = _cmp_weights(
        p["cmp_pos_v"], p["cmp_w1_v"], p["cmp_b1_v"], p["cmp_w2_v"])

    w["g_mem"] = p["g_mem"].reshape(1, D_MODEL)
    w["w_mem_kv"] = p["w_mem_kv"].astype(BF16)
    w["mem_gq"] = p["mem_g_q"].reshape(1, MEM_DH)
    w["mem_gk"] = p["mem_g_k"].reshape(1, MEM_DH)
    w["w_mq"] = col(14).astype(BF16)
    w["w_gate"] = col(15).astype(BF16)
    for k in ("w_br_ml", "w_br_nsa", "w_br_mem", "w_out"):
        w[k] = p[k].astype(BF16)
    w["g_ffn"] = p["g_ffn"].reshape(1, D_MODEL)
    w_r = jnp.concatenate([p["w_router"], jnp.zeros((D_MODEL, LANES - N_EXPERTS), F32)], axis=1)
    w["w_router_hi"] = w_r.astype(BF16)
    w["w_router_lo"] = (w_r - w["w_router_hi"].astype(F32)).astype(BF16)
    w["b_router"] = jnp.concatenate([p["b_router"], jnp.zeros((LANES - N_EXPERTS,), F32)]).reshape(1, LANES)
    w["w_gu"] = p["w_gu"]
    w["b_gu"] = p["b_gu"][:, None, :]
    w["w_down"] = p["w_down"]
    w["b_down"] = p["b_down"][:, None, :]
    return w


_PARAM_NAMES = ("g_mix", "w_in", "ml_conv_w", "ml_conv_b", "ml_b_i", "ml_b_f", "ml_g_out", "nsa_g_q", "nsa_g_k",
                "cmp_pos_k", "cmp_w1_k", "cmp_b1_k", "cmp_w2_k", "cmp_pos_v", "cmp_w1_v", "cmp_b1_v", "cmp_w2_v",
                "g_mem", "w_mem_kv", "mem_g_q", "mem_g_k", "w_br_ml", "w_br_nsa", "w_br_mem", "w_out", "g_ffn",
                "w_router", "b_router", "w_gu", "b_gu", "w_down", "b_down")


def _layer(x, mem, p):
    bsz, seq, d = x.shape
    w = _prep_weights(p)
    x2 = x.reshape(bsz * seq, d)
    ml, nsa = _proj(x2, w["g_mix"], w["w_ml"], w["w_nsa"])
    h_ml = _mlstm(ml.reshape(bsz, seq, ML_COLS), w["conv_w"], w["conv_b"], w["ml_gate_b"], w["ml_g_out"], bsz, seq)
    h_nsa = _nsa(nsa.reshape(bsz, seq, NSA_COLS), w, bsz, seq)
    mk, mv = _memkv(mem, w["g_mem"], w["w_mem_kv"], w["mem_gk"])
    x1, t, logits = _post(x2, h_ml.reshape(bsz * seq, ML_W), h_nsa.reshape(bsz * seq, NSA_W), mk, mv, w, seq)
    return _moe(x1, t, logits, w).reshape(bsz, seq, d)


def kernel(x, mem, g_mix, w_in, ml_conv_w, ml_conv_b, ml_b_i, ml_b_f, ml_g_out, nsa_g_q, nsa_g_k, cmp_pos_k,
           cmp_w1_k, cmp_b1_k, cmp_w2_k, cmp_pos_v, cmp_w1_v, cmp_b1_v, cmp_w2_v, g_mem, w_mem_kv, mem_g_q,
           mem_g_k, w_br_ml, w_br_nsa, w_br_mem, w_out, g_ffn, w_router, b_router, w_gu, b_gu, w_down, b_down):
    stacked = (g_mix, w_in, ml_conv_w, ml_conv_b, ml_b_i, ml_b_f, ml_g_out, nsa_g_q, nsa_g_k, cmp_pos_k,
               cmp_w1_k, cmp_b1_k, cmp_w2_k, cmp_pos_v, cmp_w1_v, cmp_b1_v, cmp_w2_v, g_mem, w_mem_kv, mem_g_q,
               mem_g_k, w_br_ml, w_br_nsa, w_br_mem, w_out, g_ffn, w_router, b_router, w_gu, b_gu, w_down, b_down)
    h = x
    for layer in range(g_mix.shape[0]):
        h = _layer(h, mem, {name: v[layer] for name, v in zip(_PARAM_NAMES, stacked)})
    return h
```

```python
import functools

import jax
import jax.numpy as jnp
import numpy as np
from jax import lax
from jax.experimental import pallas as pl
from jax.experimental.pallas import tpu as pltpu

F32 = jnp.float32
BF16 = jnp.bfloat16

D_MODEL = 1024
ML_HEADS = 4
ML_DQK = 128
ML_W = ML_HEADS * ML_DQK
ML_CHUNK = 64
ML_CONV = 4
NSA_QH = 8
NSA_KVH = 2
NSA_DH = 64
NSA_GROUP = NSA_QH // NSA_KVH
NSA_W = NSA_QH * NSA_DH
NSA_KV_W = NSA_KVH * NSA_DH
CMP_BLOCK = 32
CMP_STRIDE = 16
CMP_HIDDEN = 256
SEL_BLOCK = 64
SEL_TOPN = 8
SEL_FORCE_BONUS = 1e4
WINDOW = 512
MEM_HEADS = 4
MEM_DH = 128
MEM_W = MEM_HEADS * MEM_DH
N_EXPERTS = 32
TOP_K = 4
D_FF = D_MODEL
SWIGLU_ALPHA = 1.702
SWIGLU_LIMIT = 7.0
N_BRANCH = 3
ROPE_THETA = 10000.0
EPS = 1e-6
NEG = -1e30

LANES = 128
VMEM_LIMIT = 56 * 1024 * 1024

IN_SPLITS = (ML_W, ML_W, ML_W, ML_W, ML_HEADS, ML_HEADS,
             NSA_W, NSA_KV_W, NSA_KV_W, NSA_KV_W, NSA_KV_W, NSA_KV_W, NSA_KV_W, NSA_QH * 3,
             MEM_W, N_BRANCH * D_MODEL)

ML_COLS = 4 * ML_W + LANES
NSA_COLS = NSA_W + 6 * NSA_KV_W + LANES


def _dot(a, b):
    return jnp.dot(a, b, preferred_element_type=F32)


def _dot_nt(a, b):
    return lax.dot_general(a, b, (((1,), (1,)), ((), ())), preferred_element_type=F32)


def _dot_tn(a, b):
    return lax.dot_general(a, b, (((0,), (0,)), ((), ())), preferred_element_type=F32)


def _split3(v):
    a = v.astype(BF16)
    r = v - a.astype(F32)
    b = r.astype(BF16)
    c = (r - b.astype(F32)).astype(BF16)
    return a, b, c


def _sigmoid(v):
    return 1.0 / (1.0 + jnp.exp(-v))


def _log_sigmoid(v):
    return jnp.minimum(v, 0.0) - jnp.log1p(jnp.exp(-jnp.abs(v)))


def _params(sem):
    return pltpu.CompilerParams(dimension_semantics=sem, vmem_limit_bytes=VMEM_LIMIT)


PROJ_TM = 512


def _proj_kernel(x_ref, g_ref, wml_ref, wnsa_ref, oml_ref, onsa_ref):
    x = x_ref[...]
    u = x * lax.rsqrt(jnp.mean(x * x, axis=-1, keepdims=True) + EPS) * g_ref[...]
    ub = u.astype(BF16)
    oml_ref[...] = _dot(ub, wml_ref[...])
    onsa_ref[...] = _dot(ub, wnsa_ref[...])


def _proj(x2, g_mix, w_ml, w_nsa):
    n = x2.shape[0]
    tm = min(PROJ_TM, n)
    return pl.pallas_call(
        _proj_kernel,
        out_shape=(jax.ShapeDtypeStruct((n, ML_COLS), F32), jax.ShapeDtypeStruct((n, NSA_COLS), F32)),
        grid=(n // tm,),
        in_specs=[pl.BlockSpec((tm, D_MODEL), lambda i: (i, 0)),
                  pl.BlockSpec((1, D_MODEL), lambda i: (0, 0)),
                  pl.BlockSpec((D_MODEL, ML_COLS), lambda i: (0, 0)),
                  pl.BlockSpec((D_MODEL, NSA_COLS), lambda i: (0, 0))],
        out_specs=(pl.BlockSpec((tm, ML_COLS), lambda i: (i, 0)),
                   pl.BlockSpec((tm, NSA_COLS), lambda i: (i, 0))),
        compiler_params=_params(("parallel",)),
        name="in_proj",
    )(x2, g_mix, w_ml, w_nsa)


ML_T = 256
CONV_PAD = 8


def _mlstm_kernel(q_ref, k_ref, v_ref, o_ref, g_ref, cw_ref, cb_ref, gb_ref, gout_ref, tri_ref, blk_ref,
                  out_ref, buf_ref, c_ref, n_ref, m_ref):
    T, L = ML_T, ML_CHUNK

    @pl.when(pl.program_id(1) == 0)
    def _():
        buf_ref[0:CONV_PAD, :] = jnp.zeros((CONV_PAD, 2 * ML_W), F32)
        c_ref[...] = jnp.zeros(c_ref.shape, F32)
        n_ref[...] = jnp.zeros(n_ref.shape, F32)
        m_ref[...] = jnp.zeros(m_ref.shape, F32)

    buf_ref[CONV_PAD:CONV_PAD + T, 0:ML_W] = q_ref[...]
    buf_ref[CONV_PAD:CONV_PAD + T, ML_W:2 * ML_W] = k_ref[...]
    y = cb_ref[...]
    for j in range(ML_CONV):
        lo = CONV_PAD - (ML_CONV - 1) + j
        y = y + cw_ref[j:j + 1, :] * buf_ref[lo:lo + T, :]
    buf_ref[0:CONV_PAD, :] = buf_ref[T:T + CONV_PAD, :]
    qk = y * _sigmoid(y)
    qf = qk[:, 0:ML_W] * (ML_DQK ** -0.5)
    kf = qk[:, ML_W:2 * ML_W]

    gg = g_ref[...] + gb_ref[...]
    gt = gg.T
    tri = tri_ref[...]
    blk = blk_ref[...]
    lf1, lf2, lf3 = _split3(_log_sigmoid(gg))
    b_col = _dot(tri, lf1) + _dot(tri, lf2) + _dot(tri, lf3)
    bl_col = _dot(blk, lf1) + _dot(blk, lf2) + _dot(blk, lf3)
    lt1, lt2, lt3 = _split3(_log_sigmoid(gt[0:8, :]))
    b_row = _dot_nt(lt1, tri) + _dot_nt(lt2, tri) + _dot_nt(lt3, tri)
    bl_row = _dot_nt(lt1, blk) + _dot_nt(lt2, blk) + _dot_nt(lt3, blk)

    r_i = lax.broadcasted_iota(jnp.int32, (T, T), 0)
    c_i = lax.broadcasted_iota(jnp.int32, (T, T), 1)
    causal = (r_i // L == c_i // L) & (c_i <= r_i)
    row_chunk = lax.broadcasted_iota(jnp.int32, (T, 1), 0) // L
    lane_chunk = lax.broadcasted_iota(jnp.int32, (1, T), 1) // L
    n_chunks = T // L

    for h in range(ML_HEADS):
        cols = slice(h * ML_DQK, (h + 1) * ML_DQK)
        f_lane = ML_HEADS + h
        bc, blc, igc = b_col[:, f_lane:f_lane + 1], bl_col[:, f_lane:f_lane + 1], gg[:, h:h + 1]
        br, blr, igr = b_row[f_lane:f_lane + 1, :], bl_row[f_lane:f_lane + 1, :], gt[h:h + 1, :]
        g_row = blr - br + igr
        g_col = blc - bc + igc

        m_prev = [m_ref[h][:, 0:1]]
        a_old, a_new, m_cs = [], [], []
        for c in range(n_chunks):
            m_c = jnp.max(jnp.where(lane_chunk == c, g_row, NEG), axis=1, keepdims=True)
            bl = blr[:, c * L:c * L + 1]
            m_next = jnp.maximum(bl + m_prev[c], m_c)
            a_old.append(jnp.exp(bl + m_prev[c] - m_next))
            a_new.append(jnp.exp(m_c - m_next))
            m_cs.append(m_c)
            m_prev.append(m_next)
        mprev_col = jnp.zeros((T, 1), F32)
        mc_col = jnp.zeros((T, 1), F32)
        for c in range(n_chunks):
            mprev_col = jnp.where(row_chunk == c, m_prev[c], mprev_col)
            mc_col = jnp.where(row_chunk == c, m_cs[c], mc_col)

        q_c = qf[:, cols]
        k_c = kf[:, cols]
        q_b = q_c.astype(BF16)
        v_b = v_ref[:, cols].astype(BF16)
        d_log = jnp.where(causal, bc - br + igr, NEG)
        a_log = bc + mprev_col
        m_t = jnp.maximum(a_log, jnp.max(d_log, axis=1, keepdims=True))
        wd = jnp.exp(d_log - m_t) * _dot_nt(q_b, k_c.astype(BF16))
        inter = jnp.exp(a_log - m_t)
        intra = _dot(wd.astype(BF16), v_b)
        wsum = jnp.sum(wd, axis=1, keepdims=True)
        kw = k_c * jnp.exp(g_col - mc_col)
        kw_b = kw.astype(BF16)

        c_st = c_ref[h]
        n_st = n_ref[h]
        qc_parts, qn_parts = [], []
        for c in range(n_chunks):
            rows = slice(c * L, (c + 1) * L)
            qc_parts.append(_dot(q_b[rows, :], c_st.astype(BF16)))
            qn_parts.append(jnp.sum(q_c[rows, :] * n_st, axis=1, keepdims=True))
            c_st = a_old[c] * c_st + a_new[c] * _dot_tn(kw_b[rows, :], v_b[rows, :])
            n_st = a_old[c] * n_st + a_new[c] * jnp.sum(kw[rows, :], axis=0, keepdims=True)
        c_ref[h] = c_st
        n_ref[h] = n_st
        m_ref[h] = jnp.broadcast_to(m_prev[n_chunks], (1, LANES))

        num = inter * jnp.concatenate(qc_parts, axis=0) + intra
        den = inter * jnp.concatenate(qn_parts, axis=0) + wsum
        hh = num / jnp.maximum(jnp.abs(den), jnp.exp(-m_t))
        hn = hh * lax.rsqrt(jnp.mean(hh * hh, axis=-1, keepdims=True) + EPS) * gout_ref[h:h + 1, :]
        out_ref[:, cols] = (hn * _sigmoid(o_ref[:, cols])).astype(BF16)


def _mlstm(ml, conv_w, conv_b, gate_b, g_out, bsz, seq):
    T = min(ML_T, seq)
    assert T == ML_T and seq % T == 0
    t_i = np.arange(T)
    blk = t_i[:, None] // ML_CHUNK == t_i[None, :] // ML_CHUNK
    tri = jnp.asarray(blk & (t_i[None, :] <= t_i[:, None]), BF16)
    blk = jnp.asarray(blk, BF16)
    wblk = lambda j: pl.BlockSpec((None, T, ML_W), lambda b, i, j=j: (b, i, j))
    full = lambda shape: pl.BlockSpec(shape, lambda b, i: (0,) * len(shape))
    return pl.pallas_call(
        _mlstm_kernel,
        out_shape=jax.ShapeDtypeStruct((bsz, seq, ML_W), BF16),
        grid=(bsz, seq // T),
        in_specs=[wblk(0), wblk(1), wblk(2), wblk(3),
                  pl.BlockSpec((None, T, LANES), lambda b, i: (b, i, 4 * ML_W // LANES)),
                  full((ML_CONV, 2 * ML_W)), full((1, 2 * ML_W)), full((1, LANES)),
                  full((ML_HEADS, ML_DQK)), full((T, T)), full((T, T))],
        out_specs=pl.BlockSpec((None, T, ML_W), lambda b, i: (b, i, 0)),
        scratch_shapes=[pltpu.VMEM((CONV_PAD + T, 2 * ML_W), F32),
                        pltpu.VMEM((ML_HEADS, ML_DQK, ML_DQK), F32),
                        pltpu.VMEM((ML_HEADS, 1, ML_DQK), F32),
                        pltpu.VMEM((ML_HEADS, 1, LANES), F32)],
        compiler_params=_params(("parallel", "arbitrary")),
        name="mlstm",
    )(ml, ml, ml, ml, ml, conv_w, conv_b, gate_b, g_out, tri, blk)


NSA_TP = 512
NSA_TQ = 512
NSA_OFF_KC, NSA_OFF_VC, NSA_OFF_KS, NSA_OFF_VS, NSA_OFF_KW, NSA_OFF_VW, NSA_OFF_GATE = (
    NSA_W + k * NSA_KV_W for k in range(7))


def _rope_tables(pos):
    half = NSA_DH // 2
    inv_freq = jnp.power(ROPE_THETA, -jnp.arange(half, dtype=F32) / half)
    ang = pos.astype(F32)[:, None] * inv_freq[None, :]
    cos, sin = jnp.cos(ang), jnp.sin(ang)
    cos = jnp.concatenate([cos, cos], axis=1)
    sin = jnp.concatenate([-sin, sin], axis=1)
    return jnp.tile(cos, (1, 2)), jnp.tile(sin, (1, 2))


def _head_norm_rope(xs, gsum, gain, cos, sin):
    a, b, c = _split3(xs * xs)
    ss = _dot(a, gsum) + _dot(b, gsum) + _dot(c, gsum)
    xn = xs * lax.rsqrt(ss * (1.0 / NSA_DH) + EPS) * gain
    lane = lax.broadcasted_iota(jnp.int32, xs.shape, 1)
    first = (lane % NSA_DH) < NSA_DH // 2
    swapped = jnp.where(first, pltpu.roll(xn, LANES - NSA_DH // 2, axis=1), pltpu.roll(xn, NSA_DH // 2, axis=1))
    return xn * cos + swapped * sin


def _head_norm_rope_t(xt, gsum, gain, cos_t, sin_t):
    a, b, c = _split3(xt * xt)
    ss = _dot(gsum, a) + _dot(gsum, b) + _dot(gsum, c)
    xn = xt * lax.rsqrt(ss * (1.0 / NSA_DH) + EPS) * gain
    row = lax.broadcasted_iota(jnp.int32, xt.shape, 0)
    first = (row % NSA_DH) < NSA_DH // 2
    swapped = jnp.where(first, pltpu.roll(xn, LANES - NSA_DH // 2, axis=0), pltpu.roll(xn, NSA_DH // 2, axis=0))
    return xn * cos_t + swapped * sin_t


def _store_keys(out, slab):
    lane = lax.broadcasted_iota(jnp.int32, slab.shape, 1)
    low = lane < NSA_DH
    rolled = pltpu.roll(slab, NSA_DH, axis=1)
    zero = jnp.zeros_like(slab)
    out[0, 0] = jnp.where(low, slab, zero).astype(BF16)
    out[0, 1] = jnp.where(low, zero, rolled).astype(BF16)
    out[1, 0] = jnp.where(low, rolled, zero).astype(BF16)
    out[1, 1] = jnp.where(low, zero, slab).astype(BF16)


def _store_values_t(out, slab):
    st = slab.T
    out[0] = st[0:NSA_DH, :].astype(BF16)
    out[1] = st[NSA_DH:2 * NSA_DH, :].astype(BF16)


def _nsa_prep_kernel(x_ref, cos_ref, sin_ref, cost_ref, sint_ref, gq_ref, gks_ref, gkw_ref, gsum_ref,
                     q_out, ks_out, vs_out, kw_out, vw_out, kc_out, vc_out):
    cos, sin, gsum = cos_ref[...], sin_ref[...], gsum_ref[...]
    for c in range(NSA_W // LANES):
        xt = x_ref[:, c * LANES:(c + 1) * LANES].T
        qn = _head_norm_rope_t(xt, gsum, gq_ref[...], cost_ref[...], sint_ref[...]) * (NSA_DH ** -0.5)
        q_out[c * LANES:(c + 1) * LANES, :] = qn.astype(BF16)
    ks = _head_norm_rope(x_ref[:, NSA_OFF_KS:NSA_OFF_KS + LANES], gsum, gks_ref[...], cos, sin)
    kw = _head_norm_rope(x_ref[:, NSA_OFF_KW:NSA_OFF_KW + LANES], gsum, gkw_ref[...], cos, sin)
    _store_keys(ks_out, ks)
    _store_keys(kw_out, kw)
    _store_values_t(vs_out, x_ref[:, NSA_OFF_VS:NSA_OFF_VS + LANES])
    _store_values_t(vw_out, x_ref[:, NSA_OFF_VW:NSA_OFF_VW + LANES])
    kc_out[...] = x_ref[:, NSA_OFF_KC:NSA_OFF_KC + LANES]
    vc_out[...] = x_ref[:, NSA_OFF_VC:NSA_OFF_VC + LANES]


def _nsa_cmp_kernel(xk_ref, xv_ref, pek_ref, pev_ref, w1ka_ref, w1kb_ref, w1va_ref, w1vb_ref,
                    b1k_ref, b1v_ref, w2k_ref, w2v_ref, gk_ref, cos_ref, sin_ref, gsum_ref,
                    kc_out, vc_out):
    rows = xk_ref.shape[0]

    def hidden(x, pe, wa, wb, b1):
        first = _dot((x + pe[0:1, :]).astype(BF16), wa[...])
        second = _dot((x + pe[1:2, :]).astype(BF16), wb[...])
        hid = first + pltpu.roll(second, rows - 1, axis=0) + b1[...]
        return (hid * _sigmoid(hid)).astype(BF16)

    kc = _dot(hidden(xk_ref[...], pek_ref, w1ka_ref, w1kb_ref, b1k_ref), w2k_ref[...])
    kc = _head_norm_rope(kc, gsum_ref[...], gk_ref[...], cos_ref[...], sin_ref[...])
    _store_keys(kc_out, kc)
    vc_t = _dot_nt(w2v_ref[...], hidden(xv_ref[...], pev_ref, w1va_ref, w1vb_ref, b1v_ref))
    vc_out[0] = vc_t[0:NSA_DH, :].astype(BF16)
    vc_out[1] = vc_t[NSA_DH:2 * NSA_DH, :].astype(BF16)


NSA_HEAD_SLOT = (0, 2, 1, 3)


def _stacked_scores(k_lo, k_hi, q_pair, bias):
    s = jnp.concatenate([_dot(k_lo, q_pair), _dot(k_hi, q_pair)], axis=1)
    return s + jnp.concatenate([bias] * NSA_GROUP, axis=1)


def _online_update(s, v_t, m_ref, l_ref, acc_ref):
    m_old = m_ref[...]
    m_new = jnp.maximum(m_old, jnp.max(s, axis=0, keepdims=True))
    alpha = jnp.exp(m_old - m_new)
    e = jnp.exp(s - jnp.maximum(m_new, 0.1 * NEG))
    l_ref[...] = alpha * l_ref[...] + jnp.sum(e, axis=0, keepdims=True)
    acc_ref[...] = alpha * acc_ref[...] + _dot(v_t, e.astype(BF16))
    m_ref[...] = m_new


def _nsa_attn_kernel(q_ref, kc_ref, vc_ref, ks_ref, vs_ref, kw_ref, vw_ref, gate_ref, mt_ref, esel_ref, eg_ref,
                     out_ref, ms_s, ls_s, accs_s, mw_s, lw_s, accw_s, *, n_cmp, n_sel):
    TQ = NSA_TQ
    i = pl.program_id(2)
    t0 = i * TQ
    t_row = t0 + lax.broadcasted_iota(jnp.int32, (1, TQ), 1)
    q_pair = jnp.concatenate([q_ref[0:LANES, :], q_ref[LANES:2 * LANES, :]], axis=1)

    rows = kc_ref.shape[1]
    n_i = lax.broadcasted_iota(jnp.int32, (rows, 1), 0)
    cmask = (n_i < n_cmp) & (n_i * CMP_STRIDE + (CMP_BLOCK - 1) <= t_row)
    s = _stacked_scores(kc_ref[0], kc_ref[1], q_pair, jnp.where(cmask, 0.0, NEG))
    e = jnp.exp(s - jnp.maximum(jnp.max(s, axis=0, keepdims=True), 0.1 * NEG))
    p = e / jnp.maximum(jnp.sum(e, axis=0, keepdims=True), 1e-30)
    o_cmp_t = _dot(vc_ref[...], p.astype(BF16))
    psum = p[:, 0:TQ] + p[:, TQ:2 * TQ] + p[:, 2 * TQ:3 * TQ] + p[:, 3 * TQ:4 * TQ]
    p1, p2, p3 = _split3(psum)
    imp = _dot(mt_ref[...], p1) + _dot(mt_ref[...], p2) + _dot(mt_ref[...], p3)
    j_i = lax.broadcasted_iota(jnp.int32, (n_sel, TQ), 0)
    cur = (t0 + lax.broadcasted_iota(jnp.int32, (n_sel, TQ), 1)) // SEL_BLOCK
    forced = (j_i == 0) | (j_i == cur) | (j_i == cur - 1)
    score = jnp.where(j_i > cur, NEG, imp[0:n_sel, :] + SEL_FORCE_BONUS * forced.astype(F32))
    rank = jnp.zeros((n_sel, TQ), jnp.int32)
    for jp in range(n_sel):
        row = score[jp:jp + 1, :]
        beats = (row > score) | ((row == score) & (j_i > jp))
        rank = rank + beats.astype(jnp.int32)
    sel_t = (rank < min(SEL_TOPN, n_sel)).astype(F32)
    sel_b = jnp.concatenate([sel_t, jnp.zeros((LANES - n_sel, TQ), F32)], axis=0).astype(BF16)

    for m_ref, l_ref, acc_ref in ((ms_s, ls_s, accs_s), (mw_s, lw_s, accw_s)):
        m_ref[...] = jnp.full(m_ref.shape, NEG, F32)
        l_ref[...] = jnp.zeros(l_ref.shape, F32)
        acc_ref[...] = jnp.zeros(acc_ref.shape, F32)

    def key_tile(jt, carry, with_window):
        k0 = pl.multiple_of(jt * TQ, TQ)
        diff = t_row - (k0 + lax.broadcasted_iota(jnp.int32, (TQ, 1), 0))
        causal = diff >= 0
        smask = (_dot(esel_ref[jt], sel_b) > 0.5) & causal
        s = _stacked_scores(ks_ref[0, pl.ds(k0, TQ), :], ks_ref[1, pl.ds(k0, TQ), :], q_pair,
                            jnp.where(smask, 0.0, NEG))
        _online_update(s, vs_ref[jt], ms_s, ls_s, accs_s)
        if not with_window:
            return carry
        wmask = causal & (diff < WINDOW)
        sw = _stacked_scores(kw_ref[0, pl.ds(k0, TQ), :], kw_ref[1, pl.ds(k0, TQ), :], q_pair,
                             jnp.where(wmask, 0.0, NEG))
        _online_update(sw, vw_ref[jt], mw_s, lw_s, accw_s)
        return carry

    first_win = jnp.maximum(i - (WINDOW + TQ - 2) // TQ, 0)
    lax.fori_loop(0, first_win, functools.partial(key_tile, with_window=False), 0)
    lax.fori_loop(first_win, i + 1, functools.partial(key_tile, with_window=True), 0)

    g1, g2, g3 = _split3(_sigmoid(gate_ref[...]))
    gexp = _dot(g1, eg_ref[...]) + _dot(g2, eg_ref[...]) + _dot(g3, eg_ref[...])
    half_w = NSA_GROUP * NSA_DH
    o_sel_t = accs_s[...] / jnp.maximum(ls_s[...], 1e-30)
    o_win_t = accw_s[...] / jnp.maximum(lw_s[...], 1e-30)

    def token_major(o_t, c):
        ce, co = NSA_HEAD_SLOT[2 * c] * TQ, NSA_HEAD_SLOT[2 * c + 1] * TQ
        return jnp.concatenate([o_t[:, ce:ce + TQ], o_t[:, co:co + TQ]], axis=0).T

    for c in range(NSA_GROUP // 2):
        o_cmp = token_major(o_cmp_t, c)
        o_sel = token_major(o_sel_t, c)
        o_win = token_major(o_win_t, c)
        cs = slice(c * LANES, (c + 1) * LANES)
        o = (gexp[:, c * LANES:(c + 1) * LANES] * o_cmp
             + gexp[:, half_w + c * LANES:half_w + (c + 1) * LANES] * o_sel
             + gexp[:, 2 * half_w + c * LANES:2 * half_w + (c + 1) * LANES] * o_win)
        out_ref[:, cs] = o.astype(BF16)


def _nsa_constants(seq):
    n_cmp = (seq - CMP_BLOCK) // CMP_STRIDE + 1
    n_sel = seq // SEL_BLOCK
    rows = seq // CMP_STRIDE
    n = np.arange(rows)[None, :]
    j = np.arange(LANES)[:, None]
    mt = ((n * CMP_STRIDE < j * SEL_BLOCK + SEL_BLOCK) & (n * CMP_STRIDE + CMP_BLOCK > j * SEL_BLOCK)
          & (n < n_cmp) & (j < n_sel))
    s = np.arange(seq)
    esel = (s[:, None] // SEL_BLOCK == np.arange(LANES)[None, :])
    esel = esel.reshape(seq // NSA_TQ, NSA_TQ, LANES)
    eg = np.zeros((NSA_KVH, LANES, N_BRANCH * NSA_GROUP * NSA_DH), np.float32)
    for h in range(NSA_KVH):
        for g in range(NSA_GROUP):
            for br in range(N_BRANCH):
                lane = (h * NSA_GROUP + g) * N_BRANCH + br
                c0 = br * NSA_GROUP * NSA_DH + g * NSA_DH
                eg[h, lane, c0:c0 + NSA_DH] = 1.0
    gsum = np.kron(np.eye(LANES // NSA_DH), np.ones((NSA_DH, NSA_DH)))
    return (n_cmp, n_sel, jnp.asarray(mt, BF16), jnp.asarray(esel, BF16), jnp.asarray(eg, BF16),
            jnp.asarray(gsum, BF16))


def _nsa(nsa, w, bsz, seq):
    assert seq % NSA_TQ == 0 and seq % NSA_TP == 0
    n_cmp, n_sel, mt, esel, eg, gsum = _nsa_constants(seq)
    rows = seq // CMP_STRIDE
    cos, sin = _rope_tables(jnp.arange(seq))
    cos_c, sin_c = _rope_tables(jnp.arange(rows) * CMP_STRIDE + (CMP_BLOCK - 1))
    c2 = lambda shape: pl.BlockSpec(shape, lambda b, i: (0,) * len(shape))
    n_tiles = seq // NSA_TP
    k_shape = jax.ShapeDtypeStruct((bsz, NSA_KVH, 2, seq, LANES), BF16)
    k_spec = pl.BlockSpec((None, NSA_KVH, 2, NSA_TP, LANES), lambda b, i: (b, 0, 0, i, 0))
    v_shape = jax.ShapeDtypeStruct((bsz, NSA_KVH, n_tiles, NSA_DH, NSA_TP), BF16)
    v_spec = pl.BlockSpec((None, NSA_KVH, None, NSA_DH, NSA_TP), lambda b, i: (b, 0, i, 0, 0))
    raw_shape = jax.ShapeDtypeStruct((bsz, seq, LANES), F32)
    raw_spec = pl.BlockSpec((None, NSA_TP, LANES), lambda b, i: (b, i, 0))
    qn, ksd, vsd, kwd, vwd, kc_raw, vc_raw = pl.pallas_call(
        _nsa_prep_kernel,
        out_shape=(jax.ShapeDtypeStruct((bsz, NSA_W, seq), BF16), k_shape, v_shape, k_shape, v_shape,
                   raw_shape, raw_shape),
        grid=(bsz, n_tiles),
        in_specs=[pl.BlockSpec((None, NSA_TP, NSA_COLS), lambda b, i: (b, i, 0)),
                  pl.BlockSpec((NSA_TP, LANES), lambda b, i: (i, 0)),
                  pl.BlockSpec((NSA_TP, LANES), lambda b, i: (i, 0)),
                  pl.BlockSpec((LANES, NSA_TP), lambda b, i: (0, i)),
                  pl.BlockSpec((LANES, NSA_TP), lambda b, i: (0, i)),
                  c2((LANES, 1)), c2((1, LANES)), c2((1, LANES)), c2((LANES, LANES))],
        out_specs=(pl.BlockSpec((None, NSA_W, NSA_TP), lambda b, i: (b, 0, i)),
                   k_spec, v_spec, k_spec, v_spec, raw_spec, raw_spec),
        compiler_params=_params(("parallel", "parallel")),
        name="nsa_prep",
    )(nsa, cos, sin, cos.T, sin.T, w["nsa_gq"].reshape(LANES, 1), w["nsa_gks"], w["nsa_gkw"], gsum)

    wide = CMP_STRIDE * LANES
    xk = kc_raw.reshape(bsz, rows, wide)
    xv = vc_raw.reshape(bsz, rows, wide)
    c1 = lambda shape: pl.BlockSpec(shape, lambda b: (0,) * len(shape))
    kc_shape = jax.ShapeDtypeStruct((bsz, NSA_KVH, 2, rows, LANES), BF16)
    kc_spec = pl.BlockSpec((None, NSA_KVH, 2, rows, LANES), lambda b: (b, 0, 0, 0, 0))
    vc_shape = jax.ShapeDtypeStruct((bsz, NSA_KVH, NSA_DH, rows), BF16)
    vc_spec = pl.BlockSpec((None, NSA_KVH, NSA_DH, rows), lambda b: (b, 0, 0, 0))
    hid2 = NSA_KVH * CMP_HIDDEN
    kcd, vcd = pl.pallas_call(
        _nsa_cmp_kernel,
        out_shape=(kc_shape, vc_shape),
        grid=(bsz,),
        in_specs=[pl.BlockSpec((None, rows, wide), lambda b: (b, 0, 0)),
                  pl.BlockSpec((None, rows, wide), lambda b: (b, 0, 0)),
                  c1((2, wide)), c1((2, wide)),
                  c1((wide, hid2)), c1((wide, hid2)), c1((wide, hid2)), c1((wide, hid2)),
                  c1((1, hid2)), c1((1, hid2)), c1((hid2, LANES)), c1((LANES, hid2)),
                  c1((1, LANES)), c1((rows, LANES)), c1((rows, LANES)), c1((LANES, LANES))],
        out_specs=(kc_spec, vc_spec),
        compiler_params=_params(("parallel",)),
        name="nsa_compress",
    )(xk, xv, w["cmp_pe_k"], w["cmp_pe_v"], w["cmp_w1k_a"], w["cmp_w1k_b"], w["cmp_w1v_a"], w["cmp_w1v_b"],
      w["cmp_b1_k"], w["cmp_b1_v"], w["cmp_w2_k"], w["cmp_w2_v"].T, w["nsa_gkc"], cos_c, sin_c, gsum)

    TQ = NSA_TQ
    half_w = NSA_GROUP * NSA_DH
    c3 = lambda shape: pl.BlockSpec(shape, lambda b, h, i: (0,) * len(shape))
    assert NSA_TP == TQ
    keys = lambda r: pl.BlockSpec((None, None, 2, r, LANES), lambda b, h, i: (b, h, 0, 0, 0))
    vals = pl.BlockSpec((None, None, seq // TQ, NSA_DH, TQ), lambda b, h, i: (b, h, 0, 0, 0))
    stat = lambda: pltpu.VMEM((1, NSA_GROUP * TQ), F32)
    accu = lambda: pltpu.VMEM((NSA_DH, NSA_GROUP * TQ), F32)
    return pl.pallas_call(
        functools.partial(_nsa_attn_kernel, n_cmp=n_cmp, n_sel=n_sel),
        out_shape=jax.ShapeDtypeStruct((bsz, seq, NSA_W), BF16),
        grid=(bsz, NSA_KVH, seq // TQ),
        in_specs=[pl.BlockSpec((None, half_w, TQ), lambda b, h, i: (b, h, i)),
                  keys(rows), pl.BlockSpec((None, None, NSA_DH, rows), lambda b, h, i: (b, h, 0, 0)),
                  keys(seq), vals, keys(seq), vals,
                  pl.BlockSpec((None, TQ, LANES), lambda b, h, i: (b, i, NSA_OFF_GATE // LANES)),
                  c3((LANES, rows)), c3((seq // TQ, TQ, LANES)),
                  pl.BlockSpec((None, LANES, N_BRANCH * half_w), lambda b, h, i: (h, 0, 0))],
        out_specs=pl.BlockSpec((None, TQ, half_w), lambda b, h, i: (b, i, h)),
        scratch_shapes=[stat(), stat(), accu(), stat(), stat(), accu()],
        compiler_params=_params(("parallel", "parallel", "arbitrary")),
        name="nsa_attention",
    )(qn, kcd, vcd, ksd, vsd, kwd, vwd, nsa, mt, esel, eg)


POST_TM = 512


def _rms(v, gain):
    return v * lax.rsqrt(jnp.mean(v * v, axis=-1, keepdims=True) + EPS) * gain


def _memkv_kernel(mem_ref, g_ref, w_ref, gk_ref, k_out, v_out):
    u = _rms(mem_ref[...], g_ref[...]).astype(BF16)
    kv = _dot(u, w_ref[...])
    for h in range(MEM_HEADS):
        cs = slice(h * MEM_DH, (h + 1) * MEM_DH)
        k_out[:, cs] = _rms(kv[:, cs], gk_ref[...]).astype(BF16)
    v_out[...] = kv[:, MEM_W:2 * MEM_W].astype(BF16)


def _memkv(mem, g_mem, w_mem_kv, g_k):
    bsz, m_len, _ = mem.shape
    c1 = lambda shape: pl.BlockSpec(shape, lambda b: (0,) * len(shape))
    shape = jax.ShapeDtypeStruct((bsz, m_len, MEM_W), BF16)
    spec = pl.BlockSpec((None, m_len, MEM_W), lambda b: (b, 0, 0))
    return pl.pallas_call(
        _memkv_kernel, out_shape=(shape, shape), grid=(bsz,),
        in_specs=[pl.BlockSpec((None, m_len, D_MODEL), lambda b: (b, 0, 0)),
                  c1((1, D_MODEL)), c1((D_MODEL, 2 * MEM_W)), c1((1, MEM_DH))],
        out_specs=(spec, spec),
        compiler_params=_params(("parallel",)),
        name="mem_kv",
    )(mem, g_mem, w_mem_kv, g_k)


def _post_kernel(x_ref, hml_ref, hnsa_ref, mk_ref, mv_ref, gmix_ref, wmq_ref, gmq_ref, wgate_ref,
                 wml_ref, wnsa_ref, wmem_ref, wout_ref, gffn_ref, wrh_ref, wrl_ref, br_ref,
                 x1_out, t_out, logit_out):
    x = x_ref[...]
    ub = _rms(x, gmix_ref[...]).astype(BF16)
    mq = _dot(ub, wmq_ref[...])
    heads = []
    for h in range(MEM_HEADS):
        cs = slice(h * MEM_DH, (h + 1) * MEM_DH)
        qn = (_rms(mq[:, cs], gmq_ref[...]) * (MEM_DH ** -0.5)).astype(BF16)
        s = _dot_nt(qn, mk_ref[:, cs])
        e = jnp.exp(s - jnp.max(s, axis=1, keepdims=True))
        p = e / jnp.sum(e, axis=1, keepdims=True)
        heads.append(_dot(p.astype(BF16), mv_ref[:, cs]))
    hmem = jnp.concatenate(heads, axis=1).astype(BF16)
    gates = _sigmoid(_dot(ub, wgate_ref[...]))
    merged = (gates[:, 0:D_MODEL] * _dot(hml_ref[...], wml_ref[...])
              + gates[:, D_MODEL:2 * D_MODEL] * _dot(hnsa_ref[...], wnsa_ref[...])
              + gates[:, 2 * D_MODEL:3 * D_MODEL] * _dot(hmem, wmem_ref[...]))
    x1 = x + _dot(merged.astype(BF16), wout_ref[...])
    x1_out[...] = x1
    t = _rms(x1, gffn_ref[...])
    t_out[...] = t
    t1, t2, _ = _split3(t)
    logit_out[...] = _dot(t1, wrh_ref[...]) + _dot(t1, wrl_ref[...]) + _dot(t2, wrh_ref[...]) + br_ref[...]


def _post(x2, hml, hnsa, mk, mv, w, seq):
    n = x2.shape[0]
    tm = POST_TM
    assert seq % tm == 0
    per = seq // tm
    m_len = mk.shape[1]
    row = lambda width: pl.BlockSpec((tm, width), lambda i: (i, 0))
    c1 = lambda shape: pl.BlockSpec(shape, lambda i: (0,) * len(shape))
    memspec = pl.BlockSpec((None, m_len, MEM_W), lambda i: (i // per, 0, 0))
    return pl.pallas_call(
        _post_kernel,
        out_shape=(jax.ShapeDtypeStruct((n, D_MODEL), F32), jax.ShapeDtypeStruct((n, D_MODEL), F32),
                   jax.ShapeDtypeStruct((n, LANES), F32)),
        grid=(n // tm,),
        in_specs=[row(D_MODEL), row(ML_W), row(NSA_W), memspec, memspec,
                  c1((1, D_MODEL)), c1((D_MODEL, MEM_W)), c1((1, MEM_DH)), c1((D_MODEL, N_BRANCH * D_MODEL)),
                  c1((ML_W, D_MODEL)), c1((NSA_W, D_MODEL)), c1((MEM_W, D_MODEL)), c1((D_MODEL, D_MODEL)),
                  c1((1, D_MODEL)), c1((D_MODEL, LANES)), c1((D_MODEL, LANES)), c1((1, LANES))],
        out_specs=(row(D_MODEL), row(D_MODEL), row(LANES)),
        compiler_params=_params(("parallel",)),
        name="post",
    )(x2, hml, hnsa, mk, mv, w["g_mix"], w["w_mq"], w["mem_gq"], w["w_gate"], w["w_br_ml"], w["w_br_nsa"],
      w["w_br_mem"], w["w_out"], w["g_ffn"], w["w_router_hi"], w["w_router_lo"], w["b_router"])


ROUTE_T = 512
MOE_BLK = 256
MOE_TD = 256
MOE_DEPTH = 3


def _route_kernel(logit_ref, e_out, wtm_out, cnt_out):
    T = ROUTE_T

    @pl.when(pl.program_id(0) == 0)
    def _():
        cnt_out[...] = jnp.zeros(cnt_out.shape, F32)

    sc = logit_ref[...].T[0:N_EXPERTS, :]
    j_i = lax.broadcasted_iota(jnp.int32, (N_EXPERTS, T), 0)
    rank = jnp.zeros((N_EXPERTS, T), jnp.int32)
    for jp in range(N_EXPERTS):
        row = sc[jp:jp + 1, :]
        beats = (row > sc) | ((row == sc) & (j_i > jp))
        rank = rank + beats.astype(jnp.int32)
    sel = rank < TOP_K
    e = jnp.where(sel, jnp.exp(sc - jnp.max(sc, axis=0, keepdims=True)), 0.0)
    wgt = e / jnp.sum(e, axis=0, keepdims=True)
    cnt_out[...] = cnt_out[...] + jnp.sum(sel.astype(F32), axis=1, keepdims=True)
    e_rows, w_rows = [], []
    for r in range(TOP_K):
        hit = rank == r
        e_rows.append(jnp.sum(jnp.where(hit, j_i, 0), axis=0, keepdims=True))
        w_rows.append(jnp.sum(jnp.where(hit, wgt, 0.0), axis=0, keepdims=True))
    e_out[...] = jnp.concatenate(e_rows, axis=0)
    w4 = jnp.concatenate(w_rows, axis=0)
    wtm_out[...] = jnp.concatenate([w4, jnp.zeros((LANES - TOP_K, T), F32)], axis=0).T


def _route(logits):
    n = logits.shape[0]
    T = ROUTE_T
    assert n % T == 0
    return pl.pallas_call(
        _route_kernel,
        out_shape=(jax.ShapeDtypeStruct((TOP_K, n), jnp.int32), jax.ShapeDtypeStruct((n, LANES), F32),
                   jax.ShapeDtypeStruct((N_EXPERTS, LANES), F32)),
        grid=(n // T,),
        in_specs=[pl.BlockSpec((T, LANES), lambda i: (i, 0))],
        out_specs=(pl.BlockSpec((TOP_K, T), lambda i: (0, i)), pl.BlockSpec((T, LANES), lambda i: (i, 0)),
                   pl.BlockSpec((N_EXPERTS, LANES), lambda i: (0, 0))),
        compiler_params=_params(("arbitrary",)),
        name="moe_route",
    )(logits)


def _ffn_fused_kernel(be_ref, base_ref, nv_ref, order_ref, t_hbm, wgu_ref, bgu_ref, wd_ref, bd_ref, yk_hbm,
                      xbuf, ybuf, gsem, ssem, wgu_s, wd_s, *, n_tok):
    i = pl.program_id(0)
    nb = pl.num_programs(0)
    slot = i % MOE_DEPTH
    last_id = n_tok * TOP_K - 1
    dump0 = n_tok * TOP_K

    def start_gather(blk, buf):
        b0 = base_ref[blk]
        for j in range(MOE_BLK):
            tok = order_ref[jnp.minimum(b0 + j, last_id)] & (n_tok - 1)
            pltpu.make_async_copy(t_hbm.at[pl.ds(tok, 1)], xbuf.at[buf, pl.ds(j, 1)], gsem.at[buf]).start()

    def start_scatter(blk, nv, buf):
        b0 = base_ref[blk]
        for j in range(MOE_BLK):
            dst = jnp.where(j < nv, order_ref[jnp.minimum(b0 + j, last_id)], dump0 + buf * MOE_BLK + j)
            pltpu.make_async_copy(ybuf.at[buf, pl.ds(j, 1)], yk_hbm.at[pl.ds(dst, 1)], ssem.at[buf]).start()

    def wait_gather(buf):
        pltpu.make_async_copy(t_hbm.at[pl.ds(0, MOE_BLK)], xbuf.at[buf], gsem.at[buf]).wait()

    def wait_scatter(buf):
        pltpu.make_async_copy(ybuf.at[buf], yk_hbm.at[pl.ds(0, MOE_BLK)], ssem.at[buf]).wait()

    @pl.when(i == 0)
    def _():
        xbuf[...] = jnp.zeros(xbuf.shape, F32)
        ybuf[...] = jnp.zeros(ybuf.shape, F32)
        for q in range(MOE_DEPTH - 1):
            start_gather(jnp.minimum(q, nb - 1), q)
            start_scatter(0, 0, q)

    @pl.when((i == 0) | (be_ref[i] != be_ref[jnp.maximum(i - 1, 0)]))
    def _():
        wgu_s[...] = wgu_ref[...].astype(BF16)
        bits = lax.shift_right_logical(pltpu.bitcast(wd_ref[...].astype(BF16).astype(F32), jnp.uint32),
                                       jnp.uint32(16))
        wd_s[...] = bits | lax.shift_left(bits, jnp.uint32(16))

    for par in range(MOE_DEPTH):
        ahead = (par + MOE_DEPTH - 1) % MOE_DEPTH

        @pl.when(slot == par)
        def _(par=par, ahead=ahead):
            wait_gather(par)
            start_gather(jnp.minimum(i + MOE_DEPTH - 1, nb - 1), ahead)
            prev = jnp.maximum(i - 1, 0)
            start_scatter(prev, jnp.where(i == 0, 0, nv_ref[prev]), ahead)
            hgu = _dot(xbuf[par].astype(BF16), wgu_s[...]) + bgu_ref[...]
            gate = jnp.minimum(hgu, SWIGLU_LIMIT)
            up1 = jnp.clip(hgu, -SWIGLU_LIMIT, SWIGLU_LIMIT) + 1.0
            even = lax.broadcasted_iota(jnp.int32, (1, 2 * D_FF), 1) % 2 == 0
            act = jnp.where(even, gate * _sigmoid(SWIGLU_ALPHA * gate) * pltpu.roll(up1, 2 * D_FF - 1, axis=1), 0.0)
            y = _dot(act.astype(BF16), pltpu.bitcast(wd_s[...], BF16)) + bd_ref[...]
            wait_scatter(par)
            ybuf[par] = y

            @pl.when(i == nb - 1)
            def _():
                start_scatter(i, nv_ref[i], par)
                for q in range(MOE_DEPTH):
                    if q != par:
                        wait_gather(q)
                    wait_scatter(q)


def _ffn_fused(t, blk_e, base, n_valid, order, w):
    n = t.shape[0]
    nb = blk_e.shape[0]
    espec = lambda r, c: pl.BlockSpec((None, r, c), lambda i, be, *_: (be[i], 0, 0))
    return pl.pallas_call(
        functools.partial(_ffn_fused_kernel, n_tok=n),
        out_shape=jax.ShapeDtypeStruct((n * TOP_K + MOE_DEPTH * MOE_BLK, D_MODEL), F32),
        grid_spec=pltpu.PrefetchScalarGridSpec(
            num_scalar_prefetch=4, grid=(nb,),
            in_specs=[pl.BlockSpec(memory_space=pl.ANY), espec(D_MODEL, 2 * D_FF), espec(1, 2 * D_FF),
                      espec(D_FF, D_MODEL), espec(1, D_MODEL)],
            out_specs=pl.BlockSpec(memory_space=pl.ANY),
            scratch_shapes=[pltpu.VMEM((MOE_DEPTH, MOE_BLK, D_MODEL), F32),
                            pltpu.VMEM((MOE_DEPTH, MOE_BLK, D_MODEL), F32),
                            pltpu.SemaphoreType.DMA((MOE_DEPTH,)), pltpu.SemaphoreType.DMA((MOE_DEPTH,)),
                            pltpu.VMEM((D_MODEL, 2 * D_FF), BF16), pltpu.VMEM((D_FF, D_MODEL), jnp.uint32)]),
        compiler_params=_params(("arbitrary",)),
        name="moe_ffn",
    )(blk_e, base, n_valid, order, t, w["w_gu"], w["b_gu"], w["w_down"], w["b_down"])


def _moe_sum_kernel(x1_ref, wtm_ref, y0_ref, y1_ref, y2_ref, y3_ref, out_ref):
    wtm = wtm_ref[...]
    acc = x1_ref[...]
    for r, y_ref in enumerate((y0_ref, y1_ref, y2_ref, y3_ref)):
        acc = acc + wtm[:, r:r + 1] * y_ref[...]
    out_ref[...] = acc


def _moe_sum(x1, wtm, yk):
    n = x1.shape[0]
    per = n // MOE_TD
    row = lambda width: pl.BlockSpec((MOE_TD, width), lambda i: (i, 0))
    ysp = lambda r: pl.BlockSpec((MOE_TD, D_MODEL), lambda i, r=r: (r * per + i, 0))
    return pl.pallas_call(
        _moe_sum_kernel,
        out_shape=jax.ShapeDtypeStruct((n, D_MODEL), F32),
        grid=(per,),
        in_specs=[row(D_MODEL), row(LANES), ysp(0), ysp(1), ysp(2), ysp(3)],
        out_specs=row(D_MODEL),
        compiler_params=_params(("parallel",)),
        name="moe_sum",
    )(x1, wtm, yk, yk, yk, yk)


def _moe(x1, t, logits, w):
    n = x1.shape[0]
    e_idx, wtm, cnt = _route(logits)
    counts = cnt[:, 0].astype(jnp.int32)
    padded = (counts + MOE_BLK - 1) // MOE_BLK * MOE_BLK
    pad_end = jnp.cumsum(padded)
    pad_start = pad_end - padded
    start = jnp.cumsum(counts) - counts
    nb = -(-(n * TOP_K + N_EXPERTS * (MOE_BLK - 1)) // MOE_BLK)
    first = jnp.arange(nb, dtype=jnp.int32) * MOE_BLK
    blk_e = jnp.minimum(jnp.sum((pad_end[None, :] <= first[:, None]).astype(jnp.int32), axis=1), N_EXPERTS - 1)
    within = first - pad_start[blk_e]
    n_valid = jnp.clip(counts[blk_e] - within, 0, MOE_BLK).astype(jnp.int32)
    base = (start[blk_e] + within).astype(jnp.int32)
    assert n & (n - 1) == 0
    flat = jnp.argsort(e_idx.T.reshape(-1), stable=True).astype(jnp.int32)
    order = (flat % TOP_K) * n + flat // TOP_K
    yk = _ffn_fused(t, blk_e, base, n_valid, order, w)
    return _moe_sum(x1, wtm, yk)


def _cmp_weights(pos, w1, b1, w2):
    eye = jnp.eye(NSA_KVH, dtype=F32)
    halves = []
    for part in (w1[:CMP_STRIDE], w1[CMP_STRIDE:]):
        big = jnp.einsum("ldf,hg->lhdgf", part, eye)
        halves.append(big.reshape(CMP_STRIDE * LANES, NSA_KVH * CMP_HIDDEN).astype(BF16))
    pe = jnp.stack([jnp.broadcast_to(part[:, None, :], (CMP_STRIDE, NSA_KVH, NSA_DH)).reshape(-1)
                    for part in (pos[:CMP_STRIDE], pos[CMP_STRIDE:])])
    w2big = jnp.einsum("fd,hg->hfgd", w2, eye).reshape(NSA_KVH * CMP_HIDDEN, LANES).astype(BF16)
    return halves[0], halves[1], pe, jnp.tile(b1, NSA_KVH).reshape(1, -1), w2big


def _prep_weights(p):
    w = {}
    offs = np.concatenate([[0], np.cumsum(IN_SPLITS)])
    col = lambda k: p["w_in"][:, offs[k]:offs[k + 1]]
    zpad = lambda n: jnp.zeros((D_MODEL, n), F32)
    w["w_ml"] = jnp.concatenate([col(0), col(1), col(2), col(3), col(4), col(5),
                                 zpad(LANES - 2 * ML_HEADS)], axis=1).astype(BF16)
    w["w_nsa"] = jnp.concatenate([col(k) for k in range(6, 14)] + [zpad(LANES - NSA_QH * 3)], axis=1).astype(BF16)
    w["g_mix"] = p["g_mix"].reshape(1, D_MODEL)
    w["conv_w"] = p["ml_conv_w"]
    w["conv_b"] = p["ml_conv_b"].reshape(1, 2 * ML_W)
    w["ml_gate_b"] = jnp.concatenate([p["ml_b_i"], p["ml_b_f"], jnp.zeros((LANES - 2 * ML_HEADS,), F32)]).reshape(1, LANES)
    w["ml_g_out"] = p["ml_g_out"]

    pair = lambda g: jnp.tile(g, LANES // NSA_DH).reshape(1, LANES)
    w["nsa_gq"] = pair(p["nsa_g_q"])
    w["nsa_gkc"], w["nsa_gks"], w["nsa_gkw"] = (pair(p["nsa_g_k"][k]) for k in range(3))
    (w["cmp_w1k_a"], w["cmp_w1k_b"], w["cmp_pe_k"], w["cmp_b1_k"], w["cmp_w2_k"]) = _cmp_weights(
        p["cmp_pos_k"], p["cmp_w1_k"], p["cmp_b1_k"], p["cmp_w2_k"])
    (w["cmp_w1v_a"], w["cmp_w1v_b"], w["cmp_pe_v"], w["cmp_b1_v"], w["cmp_w2_v"]) = _cmp_weights(
        p["cmp_pos_v"], p["cmp_w1_v"], p["cmp_b1_v"], p["cmp_w2_v"])

    w["g_mem"] = p["g_mem"].reshape(1, D_MODEL)
    w["w_mem_kv"] = p["w_mem_kv"].astype(BF16)
    w["mem_gq"] = p["mem_g_q"].reshape(1, MEM_DH)
    w["mem_gk"] = p["mem_g_k"].reshape(1, MEM_DH)
    w["w_mq"] = col(14).astype(BF16)
    w["w_gate"] = col(15).astype(BF16)
    for k in ("w_br_ml", "w_br_nsa", "w_br_mem", "w_out"):
        w[k] = p[k].astype(BF16)
    w["g_ffn"] = p["g_ffn"].reshape(1, D_MODEL)
    w_r = jnp.concatenate([p["w_router"], jnp.zeros((D_MODEL, LANES - N_EXPERTS), F32)], axis=1)
    w["w_router_hi"] = w_r.astype(BF16)
    w["w_router_lo"] = (w_r - w["w_router_hi"].astype(F32)).astype(BF16)
    w["b_router"] = jnp.concatenate([p["b_router"], jnp.zeros((LANES - N_EXPERTS,), F32)]).reshape(1, LANES)
    w["w_gu"] = p["w_gu"]
    w["b_gu"] = p["b_gu"][:, None, :]
    w["w_down"] = p["w_down"]
    w["b_down"] = p["b_down"][:, None, :]
    return w


_PARAM_NAMES = ("g_mix", "w_in", "ml_conv_w", "ml_conv_b", "ml_b_i", "ml_b_f", "ml_g_out", "nsa_g_q", "nsa_g_k",
                "cmp_pos_k", "cmp_w1_k", "cmp_b1_k", "cmp_w2_k", "cmp_pos_v", "cmp_w1_v", "cmp_b1_v", "cmp_w2_v",
                "g_mem", "w_mem_kv", "mem_g_q", "mem_g_k", "w_br_ml", "w_br_nsa", "w_br_mem", "w_out", "g_ffn",
                "w_router", "b_router", "w_gu", "b_gu", "w_down", "b_down")


def _layer(x, mem, p):
    bsz, seq, d = x.shape
    w = _prep_weights(p)
    x2 = x.reshape(bsz * seq, d)
    ml, nsa = _proj(x2, w["g_mix"], w["w_ml"], w["w_nsa"])
    h_ml = _mlstm(ml.reshape(bsz, seq, ML_COLS), w["conv_w"], w["conv_b"], w["ml_gate_b"], w["ml_g_out"], bsz, seq)
    h_nsa = _nsa(nsa.reshape(bsz, seq, NSA_COLS), w, bsz, seq)
    mk, mv = _memkv(mem, w["g_mem"], w["w_mem_kv"], w["mem_gk"])
    x1, t, logits = _post(x2, h_ml.reshape(bsz * seq, ML_W), h_nsa.reshape(bsz * seq, NSA_W), mk, mv, w, seq)
    return _moe(x1, t, logits, w).reshape(bsz, seq, d)


def kernel(x, mem, g_mix, w_in, ml_conv_w, ml_conv_b, ml_b_i, ml_b_f, ml_g_out, nsa_g_q, nsa_g_k, cmp_pos_k,
           cmp_w1_k, cmp_b1_k, cmp_w2_k, cmp_pos_v, cmp_w1_v, cmp_b1_v, cmp_w2_v, g_mem, w_mem_kv, mem_g_q,
           mem_g_k, w_br_ml, w_br_nsa, w_br_mem, w_out, g_ffn, w_router, b_router, w_gu, b_gu, w_down, b_down):
    stacked = (g_mix, w_in, ml_conv_w, ml_conv_b, ml_b_i, ml_b_f, ml_g_out, nsa_g_q, nsa_g_k, cmp_pos_k,
               cmp_w1_k, cmp_b1_k, cmp_w2_k, cmp_pos_v, cmp_w1_v, cmp_b1_v, cmp_w2_v, g_mem, w_mem_kv, mem_g_q,
               mem_g_k, w_br_ml, w_br_nsa, w_br_mem, w_out, g_ffn, w_router, b_router, w_gu, b_gu, w_down, b_down)
    h = x
    for layer in range(g_mix.shape[0]):
        h = _layer(h, mem, {name: v[layer] for name, v in zip(_PARAM_NAMES, stacked)})
    return h
```

```python
import functools

import jax
import jax.numpy as jnp
import numpy as np
from jax import lax
from jax.experimental import pallas as pl
from jax.experimental.pallas import tpu as pltpu

F32 = jnp.float32
BF16 = jnp.bfloat16

D_MODEL = 1024
ML_HEADS = 4
ML_DQK = 128
ML_W = ML_HEADS * ML_DQK
ML_CHUNK = 64
ML_CONV = 4
NSA_QH = 8
NSA_KVH = 2
NSA_DH = 64
NSA_GROUP = NSA_QH // NSA_KVH
NSA_W = NSA_QH * NSA_DH
NSA_KV_W = NSA_KVH * NSA_DH
CMP_BLOCK = 32
CMP_STRIDE = 16
CMP_HIDDEN = 256
SEL_BLOCK = 64
SEL_TOPN = 8
SEL_FORCE_BONUS = 1e4
WINDOW = 512
MEM_HEADS = 4
MEM_DH = 128
MEM_W = MEM_HEADS * MEM_DH
N_EXPERTS = 32
TOP_K = 4
D_FF = D_MODEL
SWIGLU_ALPHA = 1.702
SWIGLU_LIMIT = 7.0
N_BRANCH = 3
ROPE_THETA = 10000.0
EPS = 1e-6
NEG = -1e30

LANES = 128
VMEM_LIMIT = 56 * 1024 * 1024

IN_SPLITS = (ML_W, ML_W, ML_W, ML_W, ML_HEADS, ML_HEADS,
             NSA_W, NSA_KV_W, NSA_KV_W, NSA_KV_W, NSA_KV_W, NSA_KV_W, NSA_KV_W, NSA_QH * 3,
             MEM_W, N_BRANCH * D_MODEL)

ML_COLS = 4 * ML_W + LANES
NSA_COLS = NSA_W + 6 * NSA_KV_W + LANES


def _dot(a, b):
    return jnp.dot(a, b, preferred_element_type=F32)


def _dot_nt(a, b):
    return lax.dot_general(a, b, (((1,), (1,)), ((), ())), preferred_element_type=F32)


def _dot_tn(a, b):
    return lax.dot_general(a, b, (((0,), (0,)), ((), ())), preferred_element_type=F32)


def _split3(v):
    a = v.astype(BF16)
    r = v - a.astype(F32)
    b = r.astype(BF16)
    c = (r - b.astype(F32)).astype(BF16)
    return a, b, c


def _sigmoid(v):
    return 1.0 / (1.0 + jnp.exp(-v))


def _log_sigmoid(v):
    return jnp.minimum(v, 0.0) - jnp.log1p(jnp.exp(-jnp.abs(v)))


def _params(sem):
    return pltpu.CompilerParams(dimension_semantics=sem, vmem_limit_bytes=VMEM_LIMIT)


PROJ_TM = 512


def _proj_kernel(x_ref, g_ref, wml_ref, wnsa_ref, oml_ref, onsa_ref):
    x = x_ref[...]
    u = x * lax.rsqrt(jnp.mean(x * x, axis=-1, keepdims=True) + EPS) * g_ref[...]
    ub = u.astype(BF16)
    oml_ref[...] = _dot(ub, wml_ref[...])
    onsa_ref[...] = _dot(ub, wnsa_ref[...])


def _proj(x2, g_mix, w_ml, w_nsa):
    n = x2.shape[0]
    tm = min(PROJ_TM, n)
    return pl.pallas_call(
        _proj_kernel,
        out_shape=(jax.ShapeDtypeStruct((n, ML_COLS), F32), jax.ShapeDtypeStruct((n, NSA_COLS), F32)),
        grid=(n // tm,),
        in_specs=[pl.BlockSpec((tm, D_MODEL), lambda i: (i, 0)),
                  pl.BlockSpec((1, D_MODEL), lambda i: (0, 0)),
                  pl.BlockSpec((D_MODEL, ML_COLS), lambda i: (0, 0)),
                  pl.BlockSpec((D_MODEL, NSA_COLS), lambda i: (0, 0))],
        out_specs=(pl.BlockSpec((tm, ML_COLS), lambda i: (i, 0)),
                   pl.BlockSpec((tm, NSA_COLS), lambda i: (i, 0))),
        compiler_params=_params(("parallel",)),
        name="in_proj",
    )(x2, g_mix, w_ml, w_nsa)


ML_T = 256
CONV_PAD = 8


ML_G = 2


def _mlstm_kernel(q_ref, k_ref, v_ref, o_ref, g_ref, cw_ref, cb_ref, gb_ref, gout_ref, tri_ref, blk_ref,
                  out_ref, buf_ref, c_ref, n_ref, m_ref):
    @pl.when(pl.program_id(1) == 0)
    def _():
        buf_ref[:, 0:CONV_PAD, :] = jnp.zeros((ML_G, CONV_PAD, 2 * ML_W), F32)
        c_ref[...] = jnp.zeros(c_ref.shape, F32)
        n_ref[...] = jnp.zeros(n_ref.shape, F32)
        m_ref[...] = jnp.zeros(m_ref.shape, F32)

    for g in range(ML_G):
        _mlstm_tile(q_ref.at[g], k_ref.at[g], v_ref.at[g], o_ref.at[g], g_ref.at[g], cw_ref, cb_ref, gb_ref,
                    gout_ref, tri_ref, blk_ref, out_ref.at[g], buf_ref.at[g], c_ref.at[g], n_ref.at[g], m_ref.at[g])


def _mlstm_tile(q_ref, k_ref, v_ref, o_ref, g_ref, cw_ref, cb_ref, gb_ref, gout_ref, tri_ref, blk_ref,
                out_ref, buf_ref, c_ref, n_ref, m_ref):
    T, L = ML_T, ML_CHUNK
    buf_ref[CONV_PAD:CONV_PAD + T, 0:ML_W] = q_ref[...]
    buf_ref[CONV_PAD:CONV_PAD + T, ML_W:2 * ML_W] = k_ref[...]
    y = cb_ref[...]
    for j in range(ML_CONV):
        lo = CONV_PAD - (ML_CONV - 1) + j
        y = y + cw_ref[j:j + 1, :] * buf_ref[lo:lo + T, :]
    buf_ref[0:CONV_PAD, :] = buf_ref[T:T + CONV_PAD, :]
    qk = y * _sigmoid(y)
    qf = qk[:, 0:ML_W] * (ML_DQK ** -0.5)
    kf = qk[:, ML_W:2 * ML_W]

    gg = g_ref[...] + gb_ref[...]
    gt = gg.T
    tri = tri_ref[...]
    blk = blk_ref[...]
    lf1, lf2, lf3 = _split3(_log_sigmoid(gg))
    b_col = _dot(tri, lf1) + _dot(tri, lf2) + _dot(tri, lf3)
    bl_col = _dot(blk, lf1) + _dot(blk, lf2) + _dot(blk, lf3)
    lt1, lt2, lt3 = _split3(_log_sigmoid(gt[0:8, :]))
    b_row = _dot_nt(lt1, tri) + _dot_nt(lt2, tri) + _dot_nt(lt3, tri)
    bl_row = _dot_nt(lt1, blk) + _dot_nt(lt2, blk) + _dot_nt(lt3, blk)

    r_i = lax.broadcasted_iota(jnp.int32, (T, T), 0)
    c_i = lax.broadcasted_iota(jnp.int32, (T, T), 1)
    causal = (r_i // L == c_i // L) & (c_i <= r_i)
    row_chunk = lax.broadcasted_iota(jnp.int32, (T, 1), 0) // L
    lane_chunk = lax.broadcasted_iota(jnp.int32, (1, T), 1) // L
    n_chunks = T // L

    for h in range(ML_HEADS):
        cols = slice(h * ML_DQK, (h + 1) * ML_DQK)
        f_lane = ML_HEADS + h
        bc, blc, igc = b_col[:, f_lane:f_lane + 1], bl_col[:, f_lane:f_lane + 1], gg[:, h:h + 1]
        br, blr, igr = b_row[f_lane:f_lane + 1, :], bl_row[f_lane:f_lane + 1, :], gt[h:h + 1, :]
        g_row = blr - br + igr
        g_col = blc - bc + igc

        m_prev = [m_ref[h][:, 0:1]]
        a_old, a_new, m_cs = [], [], []
        for c in range(n_chunks):
            m_c = jnp.max(jnp.where(lane_chunk == c, g_row, NEG), axis=1, keepdims=True)
            bl = blr[:, c * L:c * L + 1]
            m_next = jnp.maximum(bl + m_prev[c], m_c)
            a_old.append(jnp.exp(bl + m_prev[c] - m_next))
            a_new.append(jnp.exp(m_c - m_next))
            m_cs.append(m_c)
            m_prev.append(m_next)
        mprev_col = jnp.zeros((T, 1), F32)
        mc_col = jnp.zeros((T, 1), F32)
        for c in range(n_chunks):
            mprev_col = jnp.where(row_chunk == c, m_prev[c], mprev_col)
            mc_col = jnp.where(row_chunk == c, m_cs[c], mc_col)

        q_c = qf[:, cols]
        k_c = kf[:, cols]
        q_b = q_c.astype(BF16)
        v_b = v_ref[:, cols].astype(BF16)
        d_log = jnp.where(causal, bc - br + igr, NEG)
        a_log = bc + mprev_col
        m_t = jnp.maximum(a_log, jnp.max(d_log, axis=1, keepdims=True))
        wd = jnp.exp(d_log - m_t) * _dot_nt(q_b, k_c.astype(BF16))
        inter = jnp.exp(a_log - m_t)
        intra = _dot(wd.astype(BF16), v_b)
        wsum = jnp.sum(wd, axis=1, keepdims=True)
        kw = k_c * jnp.exp(g_col - mc_col)
        kw_b = kw.astype(BF16)

        c_st = c_ref[h]
        n_st = n_ref[h]
        qc_parts, qn_parts = [], []
        for c in range(n_chunks):
            rows = slice(c * L, (c + 1) * L)
            qc_parts.append(_dot(q_b[rows, :], c_st.astype(BF16)))
            qn_parts.append(jnp.sum(q_c[rows, :] * n_st, axis=1, keepdims=True))
            c_st = a_old[c] * c_st + a_new[c] * _dot_tn(kw_b[rows, :], v_b[rows, :])
            n_st = a_old[c] * n_st + a_new[c] * jnp.sum(kw[rows, :], axis=0, keepdims=True)
        c_ref[h] = c_st
        n_ref[h] = n_st
        m_ref[h] = jnp.broadcast_to(m_prev[n_chunks], (1, LANES))

        num = inter * jnp.concatenate(qc_parts, axis=0) + intra
        den = inter * jnp.concatenate(qn_parts, axis=0) + wsum
        hh = num / jnp.maximum(jnp.abs(den), jnp.exp(-m_t))
        hn = hh * lax.rsqrt(jnp.mean(hh * hh, axis=-1, keepdims=True) + EPS) * gout_ref[h:h + 1, :]
        out_ref[:, cols] = (hn * _sigmoid(o_ref[:, cols])).astype(BF16)


def _mlstm(ml, conv_w, conv_b, gate_b, g_out, bsz, seq):
    T = min(ML_T, seq)
    assert T == ML_T and seq % T == 0
    t_i = np.arange(T)
    blk = t_i[:, None] // ML_CHUNK == t_i[None, :] // ML_CHUNK
    tri = jnp.asarray(blk & (t_i[None, :] <= t_i[:, None]), BF16)
    blk = jnp.asarray(blk, BF16)
    assert bsz % ML_G == 0
    wblk = lambda j: pl.BlockSpec((ML_G, T, ML_W), lambda b, i, j=j: (b, i, j))
    full = lambda shape: pl.BlockSpec(shape, lambda b, i: (0,) * len(shape))
    return pl.pallas_call(
        _mlstm_kernel,
        out_shape=jax.ShapeDtypeStruct((bsz, seq, ML_W), BF16),
        grid=(bsz // ML_G, seq // T),
        in_specs=[wblk(0), wblk(1), wblk(2), wblk(3),
                  pl.BlockSpec((ML_G, T, LANES), lambda b, i: (b, i, 4 * ML_W // LANES)),
                  full((ML_CONV, 2 * ML_W)), full((1, 2 * ML_W)), full((1, LANES)),
                  full((ML_HEADS, ML_DQK)), full((T, T)), full((T, T))],
        out_specs=pl.BlockSpec((ML_G, T, ML_W), lambda b, i: (b, i, 0)),
        scratch_shapes=[pltpu.VMEM((ML_G, CONV_PAD + T, 2 * ML_W), F32),
                        pltpu.VMEM((ML_G, ML_HEADS, ML_DQK, ML_DQK), F32),
                        pltpu.VMEM((ML_G, ML_HEADS, 1, ML_DQK), F32),
                        pltpu.VMEM((ML_G, ML_HEADS, 1, LANES), F32)],
        compiler_params=_params(("parallel", "arbitrary")),
        name="mlstm",
    )(ml, ml, ml, ml, ml, conv_w, conv_b, gate_b, g_out, tri, blk)


NSA_TP = 512
NSA_TQ = 512
NSA_OFF_KC, NSA_OFF_VC, NSA_OFF_KS, NSA_OFF_VS, NSA_OFF_KW, NSA_OFF_VW, NSA_OFF_GATE = (
    NSA_W + k * NSA_KV_W for k in range(7))


def _rope_tables(pos):
    half = NSA_DH // 2
    inv_freq = jnp.power(ROPE_THETA, -jnp.arange(half, dtype=F32) / half)
    ang = pos.astype(F32)[:, None] * inv_freq[None, :]
    cos, sin = jnp.cos(ang), jnp.sin(ang)
    cos = jnp.concatenate([cos, cos], axis=1)
    sin = jnp.concatenate([-sin, sin], axis=1)
    return jnp.tile(cos, (1, 2)), jnp.tile(sin, (1, 2))


def _head_norm_rope(xs, gsum, gain, cos, sin):
    a, b, c = _split3(xs * xs)
    ss = _dot(a, gsum) + _dot(b, gsum) + _dot(c, gsum)
    xn = xs * lax.rsqrt(ss * (1.0 / NSA_DH) + EPS) * gain
    lane = lax.broadcasted_iota(jnp.int32, xs.shape, 1)
    first = (lane % NSA_DH) < NSA_DH // 2
    swapped = jnp.where(first, pltpu.roll(xn, LANES - NSA_DH // 2, axis=1), pltpu.roll(xn, NSA_DH // 2, axis=1))
    return xn * cos + swapped * sin


def _head_norm_rope_t(xt, gsum, gain, cos_t, sin_t):
    a, b, c = _split3(xt * xt)
    ss = _dot(gsum, a) + _dot(gsum, b) + _dot(gsum, c)
    xn = xt * lax.rsqrt(ss * (1.0 / NSA_DH) + EPS) * gain
    row = lax.broadcasted_iota(jnp.int32, xt.shape, 0)
    first = (row % NSA_DH) < NSA_DH // 2
    swapped = jnp.where(first, pltpu.roll(xn, LANES - NSA_DH // 2, axis=0), pltpu.roll(xn, NSA_DH // 2, axis=0))
    return xn * cos_t + swapped * sin_t


def _store_keys(out, slab):
    lane = lax.broadcasted_iota(jnp.int32, slab.shape, 1)
    low = lane < NSA_DH
    rolled = pltpu.roll(slab, NSA_DH, axis=1)
    zero = jnp.zeros_like(slab)
    out[0, 0] = jnp.where(low, slab, zero).astype(BF16)
    out[0, 1] = jnp.where(low, zero, rolled).astype(BF16)
    out[1, 0] = jnp.where(low, rolled, zero).astype(BF16)
    out[1, 1] = jnp.where(low, zero, slab).astype(BF16)


def _store_values_t(out, slab):
    st = slab.T
    out[0] = st[0:NSA_DH, :].astype(BF16)
    out[1] = st[NSA_DH:2 * NSA_DH, :].astype(BF16)


def _nsa_prep_kernel(x_ref, cos_ref, sin_ref, cost_ref, sint_ref, gq_ref, gks_ref, gkw_ref, gsum_ref,
                     q_out, ks_out, vs_out, kw_out, vw_out, kc_out, vc_out):
    cos, sin, gsum = cos_ref[...], sin_ref[...], gsum_ref[...]
    for c in range(NSA_W // LANES):
        xt = x_ref[:, c * LANES:(c + 1) * LANES].T
        qn = _head_norm_rope_t(xt, gsum, gq_ref[...], cost_ref[...], sint_ref[...]) * (NSA_DH ** -0.5)
        q_out[c * LANES:(c + 1) * LANES, :] = qn.astype(BF16)
    ks = _head_norm_rope(x_ref[:, NSA_OFF_KS:NSA_OFF_KS + LANES], gsum, gks_ref[...], cos, sin)
    kw = _head_norm_rope(x_ref[:, NSA_OFF_KW:NSA_OFF_KW + LANES], gsum, gkw_ref[...], cos, sin)
    _store_keys(ks_out, ks)
    _store_keys(kw_out, kw)
    _store_values_t(vs_out, x_ref[:, NSA_OFF_VS:NSA_OFF_VS + LANES])
    _store_values_t(vw_out, x_ref[:, NSA_OFF_VW:NSA_OFF_VW + LANES])
    kc_out[...] = x_ref[:, NSA_OFF_KC:NSA_OFF_KC + LANES]
    vc_out[...] = x_ref[:, NSA_OFF_VC:NSA_OFF_VC + LANES]


def _nsa_cmp_kernel(xk_ref, xv_ref, pek_ref, pev_ref, w1ka_ref, w1kb_ref, w1va_ref, w1vb_ref,
                    b1k_ref, b1v_ref, w2k_ref, w2v_ref, gk_ref, cos_ref, sin_ref, gsum_ref,
                    kc_out, vc_out):
    rows = xk_ref.shape[0]

    def hidden(x, pe, wa, wb, b1):
        first = _dot((x + pe[0:1, :]).astype(BF16), wa[...])
        second = _dot((x + pe[1:2, :]).astype(BF16), wb[...])
        hid = first + pltpu.roll(second, rows - 1, axis=0) + b1[...]
        return (hid * _sigmoid(hid)).astype(BF16)

    kc = _dot(hidden(xk_ref[...], pek_ref, w1ka_ref, w1kb_ref, b1k_ref), w2k_ref[...])
    kc = _head_norm_rope(kc, gsum_ref[...], gk_ref[...], cos_ref[...], sin_ref[...])
    _store_keys(kc_out, kc)
    vc_t = _dot_nt(w2v_ref[...], hidden(xv_ref[...], pev_ref, w1va_ref, w1vb_ref, b1v_ref))
    vc_out[0] = vc_t[0:NSA_DH, :].astype(BF16)
    vc_out[1] = vc_t[NSA_DH:2 * NSA_DH, :].astype(BF16)


NSA_HEAD_SLOT = (0, 2, 1, 3)


def _stacked_scores(k_lo, k_hi, q_pair, bias):
    s = jnp.concatenate([_dot(k_lo, q_pair), _dot(k_hi, q_pair)], axis=1)
    return s + jnp.concatenate([bias] * NSA_GROUP, axis=1)


def _online_update(s, v_t, m_ref, l_ref, acc_ref):
    m_old = m_ref[...]
    m_new = jnp.maximum(m_old, jnp.max(s, axis=0, keepdims=True))
    alpha = jnp.exp(m_old - m_new)
    e = jnp.exp(s - jnp.maximum(m_new, 0.1 * NEG))
    l_ref[...] = alpha * l_ref[...] + jnp.sum(e, axis=0, keepdims=True)
    acc_ref[...] = alpha * acc_ref[...] + _dot(v_t, e.astype(BF16))
    m_ref[...] = m_new


def _nsa_attn_kernel(q_ref, kc_ref, vc_ref, ks_ref, vs_ref, kw_ref, vw_ref, gate_ref, mt_ref, esel_ref, eg_ref,
                     out_ref, ms_s, ls_s, accs_s, mw_s, lw_s, accw_s, *, n_cmp, n_sel):
    TQ = NSA_TQ
    i = pl.program_id(2)
    t0 = i * TQ
    t_row = t0 + lax.broadcasted_iota(jnp.int32, (1, TQ), 1)
    q_pair = jnp.concatenate([q_ref[0:LANES, :], q_ref[LANES:2 * LANES, :]], axis=1)

    rows = kc_ref.shape[1]
    n_i = lax.broadcasted_iota(jnp.int32, (rows, 1), 0)
    cmask = (n_i < n_cmp) & (n_i * CMP_STRIDE + (CMP_BLOCK - 1) <= t_row)
    s = _stacked_scores(kc_ref[0], kc_ref[1], q_pair, jnp.where(cmask, 0.0, NEG))
    e = jnp.exp(s - jnp.maximum(jnp.max(s, axis=0, keepdims=True), 0.1 * NEG))
    p = e / jnp.maximum(jnp.sum(e, axis=0, keepdims=True), 1e-30)
    o_cmp_t = _dot(vc_ref[...], p.astype(BF16))
    psum = p[:, 0:TQ] + p[:, TQ:2 * TQ] + p[:, 2 * TQ:3 * TQ] + p[:, 3 * TQ:4 * TQ]
    p1, p2, p3 = _split3(psum)
    imp = _dot(mt_ref[...], p1) + _dot(mt_ref[...], p2) + _dot(mt_ref[...], p3)
    j_i = lax.broadcasted_iota(jnp.int32, (n_sel, TQ), 0)
    cur = (t0 + lax.broadcasted_iota(jnp.int32, (n_sel, TQ), 1)) // SEL_BLOCK
    forced = (j_i == 0) | (j_i == cur) | (j_i == cur - 1)
    score = jnp.where(j_i > cur, NEG, imp[0:n_sel, :] + SEL_FORCE_BONUS * forced.astype(F32))
    rank = jnp.zeros((n_sel, TQ), jnp.int32)
    for jp in range(n_sel):
        row = score[jp:jp + 1, :]
        beats = (row > score) | ((row == score) & (j_i > jp))
        rank = rank + beats.astype(jnp.int32)
    sel_t = (rank < min(SEL_TOPN, n_sel)).astype(F32)
    sel_b = jnp.concatenate([sel_t, jnp.zeros((LANES - n_sel, TQ), F32)], axis=0).astype(BF16)

    for m_ref, l_ref, acc_ref in ((ms_s, ls_s, accs_s), (mw_s, lw_s, accw_s)):
        m_ref[...] = jnp.full(m_ref.shape, NEG, F32)
        l_ref[...] = jnp.zeros(l_ref.shape, F32)
        acc_ref[...] = jnp.zeros(acc_ref.shape, F32)

    def key_tile(jt, carry, with_window):
        k0 = pl.multiple_of(jt * TQ, TQ)
        diff = t_row - (k0 + lax.broadcasted_iota(jnp.int32, (TQ, 1), 0))
        causal = diff >= 0
        smask = (_dot(esel_ref[jt], sel_b) > 0.5) & causal
        s = _stacked_scores(ks_ref[0, pl.ds(k0, TQ), :], ks_ref[1, pl.ds(k0, TQ), :], q_pair,
                            jnp.where(smask, 0.0, NEG))
        _online_update(s, vs_ref[jt], ms_s, ls_s, accs_s)
        if not with_window:
            return carry
        wmask = causal & (diff < WINDOW)
        sw = _stacked_scores(kw_ref[0, pl.ds(k0, TQ), :], kw_ref[1, pl.ds(k0, TQ), :], q_pair,
                             jnp.where(wmask, 0.0, NEG))
        _online_update(sw, vw_ref[jt], mw_s, lw_s, accw_s)
        return carry

    first_win = jnp.maximum(i - (WINDOW + TQ - 2) // TQ, 0)
    lax.fori_loop(0, first_win, functools.partial(key_tile, with_window=False), 0)
    lax.fori_loop(first_win, i + 1, functools.partial(key_tile, with_window=True), 0)

    g1, g2, g3 = _split3(_sigmoid(gate_ref[...]))
    gexp = _dot(g1, eg_ref[...]) + _dot(g2, eg_ref[...]) + _dot(g3, eg_ref[...])
    half_w = NSA_GROUP * NSA_DH
    o_sel_t = accs_s[...] / jnp.maximum(ls_s[...], 1e-30)
    o_win_t = accw_s[...] / jnp.maximum(lw_s[...], 1e-30)

    def token_major(o_t, c):
        ce, co = NSA_HEAD_SLOT[2 * c] * TQ, NSA_HEAD_SLOT[2 * c + 1] * TQ
        return jnp.concatenate([o_t[:, ce:ce + TQ], o_t[:, co:co + TQ]], axis=0).T

    for c in range(NSA_GROUP // 2):
        o_cmp = token_major(o_cmp_t, c)
        o_sel = token_major(o_sel_t, c)
        o_win = token_major(o_win_t, c)
        cs = slice(c * LANES, (c + 1) * LANES)
        o = (gexp[:, c * LANES:(c + 1) * LANES] * o_cmp
             + gexp[:, half_w + c * LANES:half_w + (c + 1) * LANES] * o_sel
             + gexp[:, 2 * half_w + c * LANES:2 * half_w + (c + 1) * LANES] * o_win)
        out_ref[:, cs] = o.astype(BF16)


def _nsa_constants(seq):
    n_cmp = (seq - CMP_BLOCK) // CMP_STRIDE + 1
    n_sel = seq // SEL_BLOCK
    rows = seq // CMP_STRIDE
    n = np.arange(rows)[None, :]
    j = np.arange(LANES)[:, None]
    mt = ((n * CMP_STRIDE < j * SEL_BLOCK + SEL_BLOCK) & (n * CMP_STRIDE + CMP_BLOCK > j * SEL_BLOCK)
          & (n < n_cmp) & (j < n_sel))
    s = np.arange(seq)
    esel = (s[:, None] // SEL_BLOCK == np.arange(LANES)[None, :])
    esel = esel.reshape(seq // NSA_TQ, NSA_TQ, LANES)
    eg = np.zeros((NSA_KVH, LANES, N_BRANCH * NSA_GROUP * NSA_DH), np.float32)
    for h in range(NSA_KVH):
        for g in range(NSA_GROUP):
            for br in range(N_BRANCH):
                lane = (h * NSA_GROUP + g) * N_BRANCH + br
                c0 = br * NSA_GROUP * NSA_DH + g * NSA_DH
                eg[h, lane, c0:c0 + NSA_DH] = 1.0
    gsum = np.kron(np.eye(LANES // NSA_DH), np.ones((NSA_DH, NSA_DH)))
    return (n_cmp, n_sel, jnp.asarray(mt, BF16), jnp.asarray(esel, BF16), jnp.asarray(eg, BF16),
            jnp.asarray(gsum, BF16))


def _nsa(nsa, w, bsz, seq):
    assert seq % NSA_TQ == 0 and seq % NSA_TP == 0
    n_cmp, n_sel, mt, esel, eg, gsum = _nsa_constants(seq)
    rows = seq // CMP_STRIDE
    cos, sin = _rope_tables(jnp.arange(seq))
    cos_c, sin_c = _rope_tables(jnp.arange(rows) * CMP_STRIDE + (CMP_BLOCK - 1))
    c2 = lambda shape: pl.BlockSpec(shape, lambda b, i: (0,) * len(shape))
    n_tiles = seq // NSA_TP
    k_shape = jax.ShapeDtypeStruct((bsz, NSA_KVH, 2, seq, LANES), BF16)
    k_spec = pl.BlockSpec((None, NSA_KVH, 2, NSA_TP, LANES), lambda b, i: (b, 0, 0, i, 0))
    v_shape = jax.ShapeDtypeStruct((bsz, NSA_KVH, n_tiles, NSA_DH, NSA_TP), BF16)
    v_spec = pl.BlockSpec((None, NSA_KVH, None, NSA_DH, NSA_TP), lambda b, i: (b, 0, i, 0, 0))
    raw_shape = jax.ShapeDtypeStruct((bsz, seq, LANES), F32)
    raw_spec = pl.BlockSpec((None, NSA_TP, LANES), lambda b, i: (b, i, 0))
    qn, ksd, vsd, kwd, vwd, kc_raw, vc_raw = pl.pallas_call(
        _nsa_prep_kernel,
        out_shape=(jax.ShapeDtypeStruct((bsz, NSA_W, seq), BF16), k_shape, v_shape, k_shape, v_shape,
                   raw_shape, raw_shape),
        grid=(bsz, n_tiles),
        in_specs=[pl.BlockSpec((None, NSA_TP, NSA_COLS), lambda b, i: (b, i, 0)),
                  pl.BlockSpec((NSA_TP, LANES), lambda b, i: (i, 0)),
                  pl.BlockSpec((NSA_TP, LANES), lambda b, i: (i, 0)),
                  pl.BlockSpec((LANES, NSA_TP), lambda b, i: (0, i)),
                  pl.BlockSpec((LANES, NSA_TP), lambda b, i: (0, i)),
                  c2((LANES, 1)), c2((1, LANES)), c2((1, LANES)), c2((LANES, LANES))],
        out_specs=(pl.BlockSpec((None, NSA_W, NSA_TP), lambda b, i: (b, 0, i)),
                   k_spec, v_spec, k_spec, v_spec, raw_spec, raw_spec),
        compiler_params=_params(("parallel", "parallel")),
        name="nsa_prep",
    )(nsa, cos, sin, cos.T, sin.T, w["nsa_gq"].reshape(LANES, 1), w["nsa_gks"], w["nsa_gkw"], gsum)

    wide = CMP_STRIDE * LANES
    xk = kc_raw.reshape(bsz, rows, wide)
    xv = vc_raw.reshape(bsz, rows, wide)
    c1 = lambda shape: pl.BlockSpec(shape, lambda b: (0,) * len(shape))
    kc_shape = jax.ShapeDtypeStruct((bsz, NSA_KVH, 2, rows, LANES), BF16)
    kc_spec = pl.BlockSpec((None, NSA_KVH, 2, rows, LANES), lambda b: (b, 0, 0, 0, 0))
    vc_shape = jax.ShapeDtypeStruct((bsz, NSA_KVH, NSA_DH, rows), BF16)
    vc_spec = pl.BlockSpec((None, NSA_KVH, NSA_DH, rows), lambda b: (b, 0, 0, 0))
    hid2 = NSA_KVH * CMP_HIDDEN
    kcd, vcd = pl.pallas_call(
        _nsa_cmp_kernel,
        out_shape=(kc_shape, vc_shape),
        grid=(bsz,),
        in_specs=[pl.BlockSpec((None, rows, wide), lambda b: (b, 0, 0)),
                  pl.BlockSpec((None, rows, wide), lambda b: (b, 0, 0)),
                  c1((2, wide)), c1((2, wide)),
                  c1((wide, hid2)), c1((wide, hid2)), c1((wide, hid2)), c1((wide, hid2)),
                  c1((1, hid2)), c1((1, hid2)), c1((hid2, LANES)), c1((LANES, hid2)),
                  c1((1, LANES)), c1((rows, LANES)), c1((rows, LANES)), c1((LANES, LANES))],
        out_specs=(kc_spec, vc_spec),
        compiler_params=_params(("parallel",)),
        name="nsa_compress",
    )(xk, xv, w["cmp_pe_k"], w["cmp_pe_v"], w["cmp_w1k_a"], w["cmp_w1k_b"], w["cmp_w1v_a"], w["cmp_w1v_b"],
      w["cmp_b1_k"], w["cmp_b1_v"], w["cmp_w2_k"], w["cmp_w2_v"].T, w["nsa_gkc"], cos_c, sin_c, gsum)

    TQ = NSA_TQ
    half_w = NSA_GROUP * NSA_DH
    c3 = lambda shape: pl.BlockSpec(shape, lambda b, h, i: (0,) * len(shape))
    assert NSA_TP == TQ
    keys = lambda r: pl.BlockSpec((None, None, 2, r, LANES), lambda b, h, i: (b, h, 0, 0, 0))
    vals = pl.BlockSpec((None, None, seq // TQ, NSA_DH, TQ), lambda b, h, i: (b, h, 0, 0, 0))
    stat = lambda: pltpu.VMEM((1, NSA_GROUP * TQ), F32)
    accu = lambda: pltpu.VMEM((NSA_DH, NSA_GROUP * TQ), F32)
    return pl.pallas_call(
        functools.partial(_nsa_attn_kernel, n_cmp=n_cmp, n_sel=n_sel),
        out_shape=jax.ShapeDtypeStruct((bsz, seq, NSA_W), BF16),
        grid=(bsz, NSA_KVH, seq // TQ),
        in_specs=[pl.BlockSpec((None, half_w, TQ), lambda b, h, i: (b, h, i)),
                  keys(rows), pl.BlockSpec((None, None, NSA_DH, rows), lambda b, h, i: (b, h, 0, 0)),
                  keys(seq), vals, keys(seq), vals,
                  pl.BlockSpec((None, TQ, LANES), lambda b, h, i: (b, i, NSA_OFF_GATE // LANES)),
                  c3((LANES, rows)), c3((seq // TQ, TQ, LANES)),
                  pl.BlockSpec((None, LANES, N_BRANCH * half_w), lambda b, h, i: (h, 0, 0))],
        out_specs=pl.BlockSpec((None, TQ, half_w), lambda b, h, i: (b, i, h)),
        scratch_shapes=[stat(), stat(), accu(), stat(), stat(), accu()],
        compiler_params=_params(("parallel", "parallel", "arbitrary")),
        name="nsa_attention",
    )(qn, kcd, vcd, ksd, vsd, kwd, vwd, nsa, mt, esel, eg)


POST_TM = 512


def _rms(v, gain):
    return v * lax.rsqrt(jnp.mean(v * v, axis=-1, keepdims=True) + EPS) * gain


def _memkv_kernel(mem_ref, g_ref, w_ref, gk_ref, k_out, v_out):
    u = _rms(mem_ref[...], g_ref[...]).astype(BF16)
    kv = _dot(u, w_ref[...])
    for h in range(MEM_HEADS):
        cs = slice(h * MEM_DH, (h + 1) * MEM_DH)
        k_out[:, cs] = _rms(kv[:, cs], gk_ref[...]).astype(BF16)
    v_out[...] = kv[:, MEM_W:2 * MEM_W].astype(BF16)


def _memkv(mem, g_mem, w_mem_kv, g_k):
    bsz, m_len, _ = mem.shape
    c1 = lambda shape: pl.BlockSpec(shape, lambda b: (0,) * len(shape))
    shape = jax.ShapeDtypeStruct((bsz, m_len, MEM_W), BF16)
    spec = pl.BlockSpec((None, m_len, MEM_W), lambda b: (b, 0, 0))
    return pl.pallas_call(
        _memkv_kernel, out_shape=(shape, shape), grid=(bsz,),
        in_specs=[pl.BlockSpec((None, m_len, D_MODEL), lambda b: (b, 0, 0)),
                  c1((1, D_MODEL)), c1((D_MODEL, 2 * MEM_W)), c1((1, MEM_DH))],
        out_specs=(spec, spec),
        compiler_params=_params(("parallel",)),
        name="mem_kv",
    )(mem, g_mem, w_mem_kv, g_k)


def _post_kernel(x_ref, hml_ref, hnsa_ref, mk_ref, mv_ref, gmix_ref, wmq_ref, gmq_ref, wgate_ref,
                 wml_ref, wnsa_ref, wmem_ref, wout_ref, gffn_ref, wrh_ref, wrl_ref, br_ref,
                 x1_out, t_out, logit_out):
    x = x_ref[...]
    ub = _rms(x, gmix_ref[...]).astype(BF16)
    mq = _dot(ub, wmq_ref[...])
    heads = []
    for h in range(MEM_HEADS):
        cs = slice(h * MEM_DH, (h + 1) * MEM_DH)
        qn = (_rms(mq[:, cs], gmq_ref[...]) * (MEM_DH ** -0.5)).astype(BF16)
        s = _dot_nt(qn, mk_ref[:, cs])
        e = jnp.exp(s - jnp.max(s, axis=1, keepdims=True))
        p = e / jnp.sum(e, axis=1, keepdims=True)
        heads.append(_dot(p.astype(BF16), mv_ref[:, cs]))
    hmem = jnp.concatenate(heads, axis=1).astype(BF16)
    gates = _sigmoid(_dot(ub, wgate_ref[...]))
    merged = (gates[:, 0:D_MODEL] * _dot(hml_ref[...], wml_ref[...])
              + gates[:, D_MODEL:2 * D_MODEL] * _dot(hnsa_ref[...], wnsa_ref[...])
              + gates[:, 2 * D_MODEL:3 * D_MODEL] * _dot(hmem, wmem_ref[...]))
    x1 = x + _dot(merged.astype(BF16), wout_ref[...])
    x1_out[...] = x1
    t = _rms(x1, gffn_ref[...])
    t_out[...] = t
    t1, t2, _ = _split3(t)
    logit_out[...] = _dot(t1, wrh_ref[...]) + _dot(t1, wrl_ref[...]) + _dot(t2, wrh_ref[...]) + br_ref[...]


def _post(x2, hml, hnsa, mk, mv, w, seq):
    n = x2.shape[0]
    tm = POST_TM
    assert seq % tm == 0
    per = seq // tm
    m_len = mk.shape[1]
    row = lambda width: pl.BlockSpec((tm, width), lambda i: (i, 0))
    c1 = lambda shape: pl.BlockSpec(shape, lambda i: (0,) * len(shape))
    memspec = pl.BlockSpec((None, m_len, MEM_W), lambda i: (i // per, 0, 0))
    return pl.pallas_call(
        _post_kernel,
        out_shape=(jax.ShapeDtypeStruct((n, D_MODEL), F32), jax.ShapeDtypeStruct((n, D_MODEL), F32),
                   jax.ShapeDtypeStruct((n, LANES), F32)),
        grid=(n // tm,),
        in_specs=[row(D_MODEL), row(ML_W), row(NSA_W), memspec, memspec,
                  c1((1, D_MODEL)), c1((D_MODEL, MEM_W)), c1((1, MEM_DH)), c1((D_MODEL, N_BRANCH * D_MODEL)),
                  c1((ML_W, D_MODEL)), c1((NSA_W, D_MODEL)), c1((MEM_W, D_MODEL)), c1((D_MODEL, D_MODEL)),
                  c1((1, D_MODEL)), c1((D_MODEL, LANES)), c1((D_MODEL, LANES)), c1((1, LANES))],
        out_specs=(row(D_MODEL), row(D_MODEL), row(LANES)),
        compiler_params=_params(("parallel",)),
        name="post",
    )(x2, hml, hnsa, mk, mv, w["g_mix"], w["w_mq"], w["mem_gq"], w["w_gate"], w["w_br_ml"], w["w_br_nsa"],
      w["w_br_mem"], w["w_out"], w["g_ffn"], w["w_router_hi"], w["w_router_lo"], w["b_router"])


ROUTE_T = 512
MOE_BLK = 256
MOE_TD = 256
MOE_DEPTH = 3


def _route_kernel(logit_ref, e_out, wtm_out, cnt_out):
    T = ROUTE_T

    @pl.when(pl.program_id(0) == 0)
    def _():
        cnt_out[...] = jnp.zeros(cnt_out.shape, F32)

    sc = logit_ref[...].T[0:N_EXPERTS, :]
    j_i = lax.broadcasted_iota(jnp.int32, (N_EXPERTS, T), 0)
    rank = jnp.zeros((N_EXPERTS, T), jnp.int32)
    for jp in range(N_EXPERTS):
        row = sc[jp:jp + 1, :]
        beats = (row > sc) | ((row == sc) & (j_i > jp))
        rank = rank + beats.astype(jnp.int32)
    sel = rank < TOP_K
    e = jnp.where(sel, jnp.exp(sc - jnp.max(sc, axis=0, keepdims=True)), 0.0)
    wgt = e / jnp.sum(e, axis=0, keepdims=True)
    cnt_out[...] = cnt_out[...] + jnp.sum(sel.astype(F32), axis=1, keepdims=True)
    e_rows, w_rows = [], []
    for r in range(TOP_K):
        hit = rank == r
        e_rows.append(jnp.sum(jnp.where(hit, j_i, 0), axis=0, keepdims=True))
        w_rows.append(jnp.sum(jnp.where(hit, wgt, 0.0), axis=0, keepdims=True))
    e_out[...] = jnp.concatenate(e_rows, axis=0)
    w4 = jnp.concatenate(w_rows, axis=0)
    wtm_out[...] = jnp.concatenate([w4, jnp.zeros((LANES - TOP_K, T), F32)], axis=0).T


def _route(logits):
    n = logits.shape[0]
    T = ROUTE_T
    assert n % T == 0
    return pl.pallas_call(
        _route_kernel,
        out_shape=(jax.ShapeDtypeStruct((TOP_K, n), jnp.int32), jax.ShapeDtypeStruct((n, LANES), F32),
                   jax.ShapeDtypeStruct((N_EXPERTS, LANES), F32)),
        grid=(n // T,),
        in_specs=[pl.BlockSpec((T, LANES), lambda i: (i, 0))],
        out_specs=(pl.BlockSpec((TOP_K, T), lambda i: (0, i)), pl.BlockSpec((T, LANES), lambda i: (i, 0)),
                   pl.BlockSpec((N_EXPERTS, LANES), lambda i: (0, 0))),
        compiler_params=_params(("arbitrary",)),
        name="moe_route",
    )(logits)


def _ffn_fused_kernel(be_ref, base_ref, nv_ref, order_ref, t_hbm, wgu_ref, bgu_ref, wd_ref, bd_ref, yk_hbm,
                      xbuf, ybuf, gsem, ssem, wgu_s, wd_s, *, n_tok):
    i = pl.program_id(0)
    nb = pl.num_programs(0)
    slot = i % MOE_DEPTH
    last_id = n_tok * TOP_K - 1
    dump0 = n_tok * TOP_K

    def start_gather(blk, buf):
        b0 = base_ref[blk]
        for j in range(MOE_BLK):
            tok = order_ref[jnp.minimum(b0 + j, last_id)] & (n_tok - 1)
            pltpu.make_async_copy(t_hbm.at[pl.ds(tok, 1)], xbuf.at[buf, pl.ds(j, 1)], gsem.at[buf]).start()

    def start_scatter(blk, nv, buf):
        b0 = base_ref[blk]
        for j in range(MOE_BLK):
            dst = jnp.where(j < nv, order_ref[jnp.minimum(b0 + j, last_id)], dump0 + buf * MOE_BLK + j)
            pltpu.make_async_copy(ybuf.at[buf, pl.ds(j, 1)], yk_hbm.at[pl.ds(dst, 1)], ssem.at[buf]).start()

    def wait_gather(buf):
        pltpu.make_async_copy(t_hbm.at[pl.ds(0, MOE_BLK)], xbuf.at[buf], gsem.at[buf]).wait()

    def wait_scatter(buf):
        pltpu.make_async_copy(ybuf.at[buf], yk_hbm.at[pl.ds(0, MOE_BLK)], ssem.at[buf]).wait()

    @pl.when(i == 0)
    def _():
        xbuf[...] = jnp.zeros(xbuf.shape, F32)
        ybuf[...] = jnp.zeros(ybuf.shape, F32)
        for q in range(MOE_DEPTH - 1):
            start_gather(jnp.minimum(q, nb - 1), q)
            start_scatter(0, 0, q)

    @pl.when((i == 0) | (be_ref[i] != be_ref[jnp.maximum(i - 1, 0)]))
    def _():
        wgu_s[...] = wgu_ref[...].astype(BF16)
        bits = lax.shift_right_logical(pltpu.bitcast(wd_ref[...].astype(BF16).astype(F32), jnp.uint32),
                                       jnp.uint32(16))
        wd_s[...] = bits | lax.shift_left(bits, jnp.uint32(16))

    for par in range(MOE_DEPTH):
        ahead = (par + MOE_DEPTH - 1) % MOE_DEPTH

        @pl.when(slot == par)
        def _(par=par, ahead=ahead):
            wait_gather(par)
            start_gather(jnp.minimum(i + MOE_DEPTH - 1, nb - 1), ahead)
            prev = jnp.maximum(i - 1, 0)
            start_scatter(prev, jnp.where(i == 0, 0, nv_ref[prev]), ahead)
            hgu = _dot(xbuf[par].astype(BF16), wgu_s[...]) + bgu_ref[...]
            gate = jnp.minimum(hgu, SWIGLU_LIMIT)
            up1 = jnp.clip(hgu, -SWIGLU_LIMIT, SWIGLU_LIMIT) + 1.0
            even = lax.broadcasted_iota(jnp.int32, (1, 2 * D_FF), 1) % 2 == 0
            act = jnp.where(even, gate * _sigmoid(SWIGLU_ALPHA * gate) * pltpu.roll(up1, 2 * D_FF - 1, axis=1), 0.0)
            y = _dot(act.astype(BF16), pltpu.bitcast(wd_s[...], BF16)) + bd_ref[...]
            wait_scatter(par)
            ybuf[par] = y

            @pl.when(i == nb - 1)
            def _():
                start_scatter(i, nv_ref[i], par)
                for q in range(MOE_DEPTH):
                    if q != par:
                        wait_gather(q)
                    wait_scatter(q)


def _ffn_fused(t, blk_e, base, n_valid, order, w):
    n = t.shape[0]
    nb = blk_e.shape[0]
    espec = lambda r, c: pl.BlockSpec((None, r, c), lambda i, be, *_: (be[i], 0, 0))
    return pl.pallas_call(
        functools.partial(_ffn_fused_kernel, n_tok=n),
        out_shape=jax.ShapeDtypeStruct((n * TOP_K + MOE_DEPTH * MOE_BLK, D_MODEL), F32),
        grid_spec=pltpu.PrefetchScalarGridSpec(
            num_scalar_prefetch=4, grid=(nb,),
            in_specs=[pl.BlockSpec(memory_space=pl.ANY), espec(D_MODEL, 2 * D_FF), espec(1, 2 * D_FF),
                      espec(D_FF, D_MODEL), espec(1, D_MODEL)],
            out_specs=pl.BlockSpec(memory_space=pl.ANY),
            scratch_shapes=[pltpu.VMEM((MOE_DEPTH, MOE_BLK, D_MODEL), F32),
                            pltpu.VMEM((MOE_DEPTH, MOE_BLK, D_MODEL), F32),
                            pltpu.SemaphoreType.DMA((MOE_DEPTH,)), pltpu.SemaphoreType.DMA((MOE_DEPTH,)),
                            pltpu.VMEM((D_MODEL, 2 * D_FF), BF16), pltpu.VMEM((D_FF, D_MODEL), jnp.uint32)]),
        compiler_params=_params(("arbitrary",)),
        name="moe_ffn",
    )(blk_e, base, n_valid, order, t, w["w_gu"], w["b_gu"], w["w_down"], w["b_down"])


def _moe_sum_kernel(x1_ref, wtm_ref, y0_ref, y1_ref, y2_ref, y3_ref, out_ref):
    wtm = wtm_ref[...]
    acc = x1_ref[...]
    for r, y_ref in enumerate((y0_ref, y1_ref, y2_ref, y3_ref)):
        acc = acc + wtm[:, r:r + 1] * y_ref[...]
    out_ref[...] = acc


def _moe_sum(x1, wtm, yk):
    n = x1.shape[0]
    per = n // MOE_TD
    row = lambda width: pl.BlockSpec((MOE_TD, width), lambda i: (i, 0))
    ysp = lambda r: pl.BlockSpec((MOE_TD, D_MODEL), lambda i, r=r: (r * per + i, 0))
    return pl.pallas_call(
        _moe_sum_kernel,
        out_shape=jax.ShapeDtypeStruct((n, D_MODEL), F32),
        grid=(per,),
        in_specs=[row(D_MODEL), row(LANES), ysp(0), ysp(1), ysp(2), ysp(3)],
        out_specs=row(D_MODEL),
        compiler_params=_params(("parallel",)),
        name="moe_sum",
    )(x1, wtm, yk, yk, yk, yk)


def _moe(x1, t, logits, w):
    n = x1.shape[0]
    e_idx, wtm, cnt = _route(logits)
    counts = cnt[:, 0].astype(jnp.int32)
    padded = (counts + MOE_BLK - 1) // MOE_BLK * MOE_BLK
    pad_end = jnp.cumsum(padded)
    pad_start = pad_end - padded
    start = jnp.cumsum(counts) - counts
    nb = -(-(n * TOP_K + N_EXPERTS * (MOE_BLK - 1)) // MOE_BLK)
    first = jnp.arange(nb, dtype=jnp.int32) * MOE_BLK
    blk_e = jnp.minimum(jnp.sum((pad_end[None, :] <= first[:, None]).astype(jnp.int32), axis=1), N_EXPERTS - 1)
    within = first - pad_start[blk_e]
    n_valid = jnp.clip(counts[blk_e] - within, 0, MOE_BLK).astype(jnp.int32)
    base = (start[blk_e] + within).astype(jnp.int32)
    assert n & (n - 1) == 0
    flat = jnp.argsort(e_idx.T.reshape(-1), stable=True).astype(jnp.int32)
    order = (flat % TOP_K) * n + flat // TOP_K
    yk = _ffn_fused(t, blk_e, base, n_valid, order, w)
    return _moe_sum(x1, wtm, yk)


def _cmp_weights(pos, w1, b1, w2):
    eye = jnp.eye(NSA_KVH, dtype=F32)
    halves = []
    for part in (w1[:CMP_STRIDE], w1[CMP_STRIDE:]):
        big = jnp.einsum("ldf,hg->lhdgf", part, eye)
        halves.append(big.reshape(CMP_STRIDE * LANES, NSA_KVH * CMP_HIDDEN).astype(BF16))
    pe = jnp.stack([jnp.broadcast_to(part[:, None, :], (CMP_STRIDE, NSA_KVH, NSA_DH)).reshape(-1)
                    for part in (pos[:CMP_STRIDE], pos[CMP_STRIDE:])])
    w2big = jnp.einsum("fd,hg->hfgd", w2, eye).reshape(NSA_KVH * CMP_HIDDEN, LANES).astype(BF16)
    return halves[0], halves[1], pe, jnp.tile(b1, NSA_KVH).reshape(1, -1), w2big


def _prep_weights(p):
    w = {}
    offs = np.concatenate([[0], np.cumsum(IN_SPLITS)])
    col = lambda k: p["w_in"][:, offs[k]:offs[k + 1]]
    zpad = lambda n: jnp.zeros((D_MODEL, n), F32)
    w["w_ml"] = jnp.concatenate([col(0), col(1), col(2), col(3), col(4), col(5),
                                 zpad(LANES - 2 * ML_HEADS)], axis=1).astype(BF16)
    w["w_nsa"] = jnp.concatenate([col(k) for k in range(6, 14)] + [zpad(LANES - NSA_QH * 3)], axis=1).astype(BF16)
    w["g_mix"] = p["g_mix"].reshape(1, D_MODEL)
    w["conv_w"] = p["ml_conv_w"]
    w["conv_b"] = p["ml_conv_b"].reshape(1, 2 * ML_W)
    w["ml_gate_b"] = jnp.concatenate([p["ml_b_i"], p["ml_b_f"], jnp.zeros((LANES - 2 * ML_HEADS,), F32)]).reshape(1, LANES)
    w["ml_g_out"] = p["ml_g_out"]

    pair = lambda g: jnp.tile(g, LANES // NSA_DH).reshape(1, LANES)
    w["nsa_gq"] = pair(p["nsa_g_q"])
    w["nsa_gkc"], w["nsa_gks"], w["nsa_gkw"] = (pair(p["nsa_g_k"][k]) for k in range(3))
    (w["cmp_w1k_a"], w["cmp_w1k_b"], w["cmp_pe_k"], w["cmp_b1_k"], w["cmp_w2_k"]) = _cmp_weights(
        p["cmp_pos_k"], p["cmp_w1_k"], p["cmp_b1_k"], p["cmp_w2_k"])
    (w["cmp_w1v_a"], w["cmp_w1v_b"], w["cmp_pe_v"], w["cmp_b1_v"], w["cmp_w2_v"]) = _cmp_weights(
        p["cmp_pos_v"], p["cmp_w1_v"], p["cmp_b1_v"], p["cmp_w2_v"])

    w["g_mem"] = p["g_mem"].reshape(1, D_MODEL)
    w["w_mem_kv"] = p["w_mem_kv"].astype(BF16)
    w["mem_gq"] = p["mem_g_q"].reshape(1, MEM_DH)
    w["mem_gk"] = p["mem_g_k"].reshape(1, MEM_DH)
    w["w_mq"] = col(14).astype(BF16)
    w["w_gate"] = col(15).astype(BF16)
    for k in ("w_br_ml", "w_br_nsa", "w_br_mem", "w_out"):
        w[k] = p[k].astype(BF16)
    w["g_ffn"] = p["g_ffn"].reshape(1, D_MODEL)
    w_r = jnp.concatenate([p["w_router"], jnp.zeros((D_MODEL, LANES - N_EXPERTS), F32)], axis=1)
    w["w_router_hi"] = w_r.astype(BF16)
    w["w_router_lo"] = (w_r - w["w_router_hi"].astype(F32)).astype(BF16)
    w["b_router"] = jnp.concatenate([p["b_router"], jnp.zeros((LANES - N_EXPERTS,), F32)]).reshape(1, LANES)
    w["w_gu"] = p["w_gu"]
    w["b_gu"] = p["b_gu"][:, None, :]
    w["w_down"] = p["w_down"]
    w["b_down"] = p["b_down"][:, None, :]
    return w


_PARAM_NAMES = ("g_mix", "w_in", "ml_conv_w", "ml_conv_b", "ml_b_i", "ml_b_f", "ml_g_out", "nsa_g_q", "nsa_g_k",
                "cmp_pos_k", "cmp_w1_k", "cmp_b1_k", "cmp_w2_k", "cmp_pos_v", "cmp_w1_v", "cmp_b1_v", "cmp_w2_v",
                "g_mem", "w_mem_kv", "mem_g_q", "mem_g_k", "w_br_ml", "w_br_nsa", "w_br_mem", "w_out", "g_ffn",
                "w_router", "b_router", "w_gu", "b_gu", "w_down", "b_down")


def _layer(x, mem, p):
    bsz, seq, d = x.shape
    w = _prep_weights(p)
    x2 = x.reshape(bsz * seq, d)
    ml, nsa = _proj(x2, w["g_mix"], w["w_ml"], w["w_nsa"])
    h_ml = _mlstm(ml.reshape(bsz, seq, ML_COLS), w["conv_w"], w["conv_b"], w["ml_gate_b"], w["ml_g_out"], bsz, seq)
    h_nsa = _nsa(nsa.reshape(bsz, seq, NSA_COLS), w, bsz, seq)
    mk, mv = _memkv(mem, w["g_mem"], w["w_mem_kv"], w["mem_gk"])
    x1, t, logits = _post(x2, h_ml.reshape(bsz * seq, ML_W), h_nsa.reshape(bsz * seq, NSA_W), mk, mv, w, seq)
    return _moe(x1, t, logits, w).reshape(bsz, seq, d)


def kernel(x, mem, g_mix, w_in, ml_conv_w, ml_conv_b, ml_b_i, ml_b_f, ml_g_out, nsa_g_q, nsa_g_k, cmp_pos_k,
           cmp_w1_k, cmp_b1_k, cmp_w2_k, cmp_pos_v, cmp_w1_v, cmp_b1_v, cmp_w2_v, g_mem, w_mem_kv, mem_g_q,
           mem_g_k, w_br_ml, w_br_nsa, w_br_mem, w_out, g_ffn, w_router, b_router, w_gu, b_gu, w_down, b_down):
    stacked = (g_mix, w_in, ml_conv_w, ml_conv_b, ml_b_i, ml_b_f, ml_g_out, nsa_g_q, nsa_g_k, cmp_pos_k,
               cmp_w1_k, cmp_b1_k, cmp_w2_k, cmp_pos_v, cmp_w1_v, cmp_b1_v, cmp_w2_v, g_mem, w_mem_kv, mem_g_q,
               mem_g_k, w_br_ml, w_br_nsa, w_br_mem, w_out, g_ffn, w_router, b_router, w_gu, b_gu, w_down, b_down)
    h = x
    for layer in range(g_mix.shape[0]):
        h = _layer(h, mem, {name: v[layer] for name, v in zip(_PARAM_NAMES, stacked)})
    return h
```
